```python
import jax
import jax.numpy as jnp
from jax import lax
import numpy as np

D_MODEL = 4096
BATCH = 4
SEQ = 4096
DEPTH = 2

CTX_LEN = 256
GRID_W = 64
HEAD_DIM = 128
ATT_Q_HEADS = 16
ATT_KV_HEADS = 4
ATT_GROUP = ATT_Q_HEADS // ATT_KV_HEADS
ATT_WIDTH = ATT_Q_HEADS * HEAD_DIM
ATT_KV_WIDTH = ATT_KV_HEADS * HEAD_DIM
Q_BLOCK = 128
ROPE_THETA = 10000.0
GMLP_WIDTH = D_MODEL // 4
GMLP_CHUNK = 128
GMLP_GROUPS = 8
GMLP_GROUP_DIM = GMLP_WIDTH // GMLP_GROUPS
MLSTM_WIDTH = D_MODEL // 4
MLSTM_HEADS = 4
MLSTM_HEAD_DIM = MLSTM_WIDTH // MLSTM_HEADS
MLSTM_CHUNK = 128
MIX_WIDTH = ATT_WIDTH + GMLP_WIDTH + MLSTM_WIDTH
N_EXPERTS = 16
EXPERT_FF = D_MODEL // 4
CAPACITY_FACTOR = 2
N_MOD = 6
ALPHA = (2 * DEPTH) ** 0.25
BETA = (8 * DEPTH) ** -0.25
EPS = 1e-6

PROJ_LAYOUT = (
    ("att_q", ATT_WIDTH), ("att_k", ATT_KV_WIDTH), ("att_v", ATT_KV_WIDTH),
    ("gm_u", GMLP_WIDTH), ("gm_v", GMLP_WIDTH),
    ("ml_q", MLSTM_WIDTH), ("ml_k", MLSTM_WIDTH), ("ml_v", MLSTM_WIDTH), ("ml_o", MLSTM_WIDTH),
    ("ml_gates", 4 * MLSTM_HEADS),
)
PROJ_WIDTH = sum(width for _, width in PROJ_LAYOUT)
CTX_STATE_PARTS = ("att_k", "att_v", "ml_k", "ml_v", "ml_gates")

kernel_name = "hybrid_dit_attn_gmlp_mlstm_ecmoe"


def layer_norm(t, g=None, b=None):
    t32 = t.astype(jnp.float32)
    mu = jnp.mean(t32, axis=-1, keepdims=True)
    var = jnp.mean(jnp.square(t32 - mu), axis=-1, keepdims=True)
    y = (t32 - mu) * lax.rsqrt(var + EPS)
    if g is not None:
        y = y * g.astype(jnp.float32) + b.astype(jnp.float32)
    return y.astype(t.dtype)


def rms_norm(t, g):
    t32 = t.astype(jnp.float32)
    y = t32 * lax.rsqrt(jnp.mean(t32 * t32, axis=-1, keepdims=True) + EPS) * g.astype(jnp.float32)
    return y.astype(t.dtype)


def modulate(t, shift, scale):
    return layer_norm(t) * (1.0 + scale) + shift


def proj_offsets():
    offs, start = {}, 0
    for name, width in PROJ_LAYOUT:
        offs[name] = (start, width)
        start += width
    return offs


def project(h, w, names=None):
    offs = proj_offsets()
    if names is None:
        p = h @ w
        return {nm: p[..., s:s + wd] for nm, (s, wd) in offs.items()}
    return {nm: h @ w[:, offs[nm][0]:offs[nm][0] + offs[nm][1]] for nm in names}


def split_heads(t, n_heads, head_dim):
    return t.reshape(*t.shape[:-1], n_heads, head_dim)


def axial_rope_tables(n):
    rows = n // GRID_W
    row = jnp.repeat(jnp.arange(rows), GRID_W).astype(jnp.float32)
    col = jnp.tile(jnp.arange(GRID_W), rows).astype(jnp.float32)
    n_freq = HEAD_DIM // 4
    inv_freq = ROPE_THETA ** (-jnp.arange(n_freq, dtype=jnp.float32) / n_freq)
    ang = jnp.stack([row[:, None] * inv_freq, col[:, None] * inv_freq], axis=1)
    return jnp.cos(ang)[:, None], jnp.sin(ang)[:, None]


def apply_rope(t, cos, sin):
    tr = t.astype(jnp.float32).reshape(*t.shape[:-1], 2, 2, HEAD_DIM // 4)
    t1, t2 = tr[..., 0, :], tr[..., 1, :]
    out = jnp.stack([t1 * cos - t2 * sin, t2 * cos + t1 * sin], axis=-2)
    return out.reshape(t.shape).astype(t.dtype)


def gqa_heads(t):
    b_, n = t.shape[:2]
    return t.reshape(b_, n, ATT_KV_HEADS, ATT_GROUP, HEAD_DIM).transpose(0, 2, 3, 1, 4)


def kv_heads(t):
    return t.transpose(0, 2, 1, 3)


def merge_gqa(o):
    b_, _, _, n, _ = o.shape
    return o.transpose(0, 3, 1, 2, 4).reshape(b_, n, ATT_WIDTH)


def attend(q, k, v):
    s = jnp.einsum("bkgqd,bksd->bkgqs", q, k).astype(jnp.float32) * HEAD_DIM ** -0.5
    p = jax.nn.softmax(s, axis=-1).astype(v.dtype)
    return jnp.einsum("bkgqs,bksd->bkgqd", p, v)


def blocked_attention(q, k, v):
    bsz, kvh, grp, n, hd = q.shape
    nb = n // Q_BLOCK
    qb = jnp.moveaxis(q.reshape(bsz, kvh, grp, nb, Q_BLOCK, hd), 3, 0)
    o = lax.map(lambda qblk: attend(qblk, k, v), qb)
    return jnp.moveaxis(o, 0, 3).reshape(bsz, kvh, grp, n, hd)


def spatial_gating(pu, pv, ln_g, ln_b, w_s, b_s):
    bsz, n, _ = pu.shape
    u = jax.nn.gelu(pu)
    v = layer_norm(jax.nn.gelu(pv), ln_g, ln_b)
    nc = n // GMLP_CHUNK
    vb = v.reshape(bsz, nc, GMLP_CHUNK, GMLP_GROUPS, GMLP_GROUP_DIM)
    sv = jnp.einsum("gpq,bcqgd->bcpgd", w_s, vb) + jnp.swapaxes(b_s, 0, 1)[:, :, None]
    return u * sv.reshape(bsz, n, GMLP_WIDTH)


def mlstm_heads(t):
    return split_heads(t, MLSTM_HEADS, MLSTM_HEAD_DIM).transpose(0, 2, 1, 3).astype(jnp.float32)


def mlstm_gates(pg, b_g):
    g = jnp.swapaxes((pg + b_g).astype(jnp.float32), 1, 2)
    i_f, f_f, i_b, f_b = jnp.split(g, 4, axis=1)
    return i_f, jax.nn.log_sigmoid(f_f), i_b, jax.nn.log_sigmoid(f_b)


def mlstm_zero_state(bsz):
    return (jnp.zeros((bsz, MLSTM_HEADS, MLSTM_HEAD_DIM, MLSTM_HEAD_DIM), jnp.float32),
            jnp.zeros((bsz, MLSTM_HEADS, MLSTM_HEAD_DIM), jnp.float32),
            jnp.zeros((bsz, MLSTM_HEADS), jnp.float32))


def mlstm_scan(q, k, v, ig, lf, state, with_output):
    b_, h_, n, dh = k.shape
    nc = n // MLSTM_CHUNK

    def chunks(t):
        return jnp.moveaxis(t.reshape(b_, h_, nc, MLSTM_CHUNK, *t.shape[3:]), 2, 0)

    tril = jnp.tril(jnp.ones((MLSTM_CHUNK, MLSTM_CHUNK), dtype=bool))

    def step(carry, inp):
        c_mat, n_vec, m = carry
        kc, vc, ic, fc = inp[:4]
        bcum = jnp.cumsum(fc, axis=-1)
        b_last = bcum[..., -1]
        g = b_last[..., None] - bcum + ic
        m_new = jnp.maximum(b_last + m, jnp.max(g, axis=-1))
        w_prev = jnp.exp(b_last + m - m_new)
        kw = kc * jnp.exp(g - m_new[..., None])[..., None]
        c_new = w_prev[..., None, None] * c_mat + jnp.einsum("bhsd,bhse->bhde", kw, vc)
        n_new = w_prev[..., None] * n_vec + jnp.sum(kw, axis=2)
        if not with_output:
            return (c_new, n_new, m_new), None
        qc = inp[4]
        a = bcum + m[..., None]
        dlog = jnp.where(tril, bcum[..., :, None] - bcum[..., None, :] + ic[..., None, :], -jnp.inf)
        mt = jnp.maximum(a, jnp.max(dlog, axis=-1))
        s = jnp.einsum("bhtd,bhsd->bhts", qc, kc) * jnp.exp(dlog - mt[..., None])
        wa = jnp.exp(a - mt)
        num = wa[..., None] * jnp.einsum("bhtd,bhde->bhte", qc, c_mat) + jnp.einsum("bhts,bhse->bhte", s, vc)
        den = wa * jnp.einsum("bhtd,bhd->bht", qc, n_vec) + jnp.sum(s, axis=-1)
        hc = num / jnp.maximum(jnp.abs(den), jnp.exp(-mt))[..., None]
        return (c_new, n_new, m_new), hc

    xs = (chunks(k), chunks(v), chunks(ig), chunks(lf))
    if with_output:
        xs = xs + (chunks(q),)
    final, hs = lax.scan(step, state, xs)
    if not with_output:
        return final, None
    return final, jnp.moveaxis(hs, 0, 2).reshape(b_, h_, n, dh)


def bidir_mlstm(q, k, v, i_f, lf_f, i_b, lf_b, st_f, st_b, with_output):
    def flip(t):
        return None if t is None else jnp.flip(t, axis=2)
    fin_f, h_f = mlstm_scan(q, k, v, i_f, lf_f, st_f, with_output)
    fin_b, h_b = mlstm_scan(flip(q), flip(k), flip(v), flip(i_b), flip(lf_b), st_b, with_output)
    h = h_f + flip(h_b) if with_output else None
    return h, fin_f, fin_b


def mlstm_output(h, po, gain):
    b_, _, n, _ = h.shape
    hn = h * lax.rsqrt(jnp.mean(h * h, axis=-1, keepdims=True) + EPS)
    hn = hn.transpose(0, 2, 1, 3).reshape(b_, n, MLSTM_WIDTH) * gain.astype(jnp.float32)
    return (jax.nn.sigmoid(po.astype(jnp.float32)) * hn).astype(po.dtype)


def expert_choice_ffn(h, w_router, w_gate, w_up, w_down):
    bsz, n, _ = h.shape
    cap = CAPACITY_FACTOR * n // N_EXPERTS
    aff = jax.nn.softmax((h @ w_router).astype(jnp.float32), axis=-1)
    gate, idx = lax.top_k(jnp.swapaxes(aff, 1, 2), cap)
    bidx = jnp.arange(bsz)[:, None, None]
    xe = h[bidx, idx]
    hid = jax.nn.silu(jnp.einsum("becd,edf->becf", xe, w_gate)) * jnp.einsum("becd,edf->becf", xe, w_up)
    ye = jnp.einsum("becf,efd->becd", hid, w_down) * gate[..., None].astype(h.dtype)
    return jnp.zeros_like(h).at[bidx, idx].add(ye)


def setup_inputs(seed: int = 0) -> dict:
    key = jax.random.key(seed)
    ks = jax.random.split(key, 24)
    f32 = jnp.float32
    d = D_MODEL

    def nrm(k, shape, scale):
        return jax.random.normal(k, shape, f32) * scale

    gate_bias_centre = jnp.repeat(jnp.array([0.0, 3.0, 0.0, 3.0], f32), MLSTM_HEADS)
    return {
        "x": nrm(ks[0], (BATCH, SEQ, d), 1.0),
        "c": nrm(ks[1], (BATCH, d), 1.0),
        "ctx": nrm(ks[2], (BATCH, CTX_LEN, d), 1.0),
        "c_ctx": nrm(ks[3], (d,), 1.0),
        "w_mod": nrm(ks[4], (DEPTH, d, N_MOD * d), 0.5 * d ** -0.5),
        "b_mod": nrm(ks[5], (DEPTH, N_MOD * d), 0.02),
        "w_in": nrm(ks[6], (DEPTH, d, PROJ_WIDTH), d ** -0.5),
        "q_gain": 1.0 + nrm(ks[7], (DEPTH, HEAD_DIM), 0.02),
        "k_gain": 1.0 + nrm(ks[8], (DEPTH, HEAD_DIM), 0.02),
        "gm_ln_g": 1.0 + nrm(ks[9], (DEPTH, GMLP_WIDTH), 0.02),
        "gm_ln_b": nrm(ks[10], (DEPTH, GMLP_WIDTH), 0.02),
        "w_spatial": nrm(ks[11], (DEPTH, GMLP_GROUPS, GMLP_CHUNK, GMLP_CHUNK), GMLP_CHUNK ** -0.5),
        "b_spatial": 1.0 + nrm(ks[12], (DEPTH, GMLP_GROUPS, GMLP_CHUNK), 0.02),
        "b_gates": gate_bias_centre[None] + nrm(ks[13], (DEPTH, 4 * MLSTM_HEADS), 0.1),
        "ml_gain": 1.0 + nrm(ks[14], (DEPTH, MLSTM_WIDTH), 0.02),
        "w_out": nrm(ks[15], (DEPTH, MIX_WIDTH, d), BETA * MIX_WIDTH ** -0.5),
        "ln1_g": 1.0 + nrm(ks[16], (DEPTH, d), 0.02),
        "ln1_b": nrm(ks[17], (DEPTH, d), 0.02),
        "w_router": nrm(ks[18], (DEPTH, d, N_EXPERTS), d ** -0.5),
        "w_e_gate": nrm(ks[19], (DEPTH, N_EXPERTS, d, EXPERT_FF), d ** -0.5),
        "w_e_up": nrm(ks[20], (DEPTH, N_EXPERTS, d, EXPERT_FF), d ** -0.5),
        "w_e_down": nrm(ks[21], (DEPTH, N_EXPERTS, EXPERT_FF, d), BETA * EXPERT_FF ** -0.5),
        "ln2_g": 1.0 + nrm(ks[22], (DEPTH, d), 0.02),
        "ln2_b": nrm(ks[23], (DEPTH, d), 0.02),
    }


def reference(x, c, ctx, c_ctx, w_mod, b_mod, w_in, q_gain, k_gain, gm_ln_g, gm_ln_b, w_spatial,
              b_spatial, b_gates, ml_gain, w_out, ln1_g, ln1_b, w_router, w_e_gate, w_e_up, w_e_down,
              ln2_g, ln2_b):
    bsz, n, _ = x.shape
    cos, sin = axial_rope_tables(n)
    zero = mlstm_zero_state(bsz)
    k_scale = MLSTM_HEAD_DIM ** -0.5
    h_lat, h_ctx = x, ctx
    for l in range(DEPTH):
        last = l == DEPTH - 1
        mod_l = jnp.split((jax.nn.silu(c) @ w_mod[l] + b_mod[l])[:, None, :], N_MOD, axis=-1)
        mod_c = jnp.split(jax.nn.silu(c_ctx) @ w_mod[l] + b_mod[l], N_MOD, axis=-1)

        a_lat = modulate(h_lat, mod_l[0], mod_l[1])
        a_ctx = modulate(h_ctx, mod_c[0], mod_c[1])
        p = project(a_lat, w_in[l])
        pc = project(a_ctx, w_in[l], CTX_STATE_PARTS if last else None)

        k_c = kv_heads(rms_norm(split_heads(pc["att_k"], ATT_KV_HEADS, HEAD_DIM), k_gain[l]))
        v_c = kv_heads(split_heads(pc["att_v"], ATT_KV_HEADS, HEAD_DIM))
        q_l = gqa_heads(apply_rope(rms_norm(split_heads(p["att_q"], ATT_Q_HEADS, HEAD_DIM), q_gain[l]), cos, sin))
        k_l = kv_heads(apply_rope(rms_norm(split_heads(p["att_k"], ATT_KV_HEADS, HEAD_DIM), k_gain[l]), cos, sin))
        v_l = kv_heads(split_heads(p["att_v"], ATT_KV_HEADS, HEAD_DIM))
        att_l = merge_gqa(blocked_attention(q_l, jnp.concatenate([k_c, k_l], axis=2),
                                            jnp.concatenate([v_c, v_l], axis=2)))

        gm_l = spatial_gating(p["gm_u"], p["gm_v"], gm_ln_g[l], gm_ln_b[l], w_spatial[l], b_spatial[l])

        i_fc, lf_fc, i_bc, lf_bc = mlstm_gates(pc["ml_gates"], b_gates[l])
        h_c, st_f, st_b = bidir_mlstm(None if last else mlstm_heads(pc["ml_q"]),
                                      mlstm_heads(pc["ml_k"]) * k_scale, mlstm_heads(pc["ml_v"]),
                                      i_fc, lf_fc, i_bc, lf_bc, zero, zero, not last)
        i_fl, lf_fl, i_bl, lf_bl = mlstm_gates(p["ml_gates"], b_gates[l])
        h_l, _, _ = bidir_mlstm(mlstm_heads(p["ml_q"]), mlstm_heads(p["ml_k"]) * k_scale,
                                mlstm_heads(p["ml_v"]), i_fl, lf_fl, i_bl, lf_bl, st_f, st_b, True)
        ml_l = mlstm_output(h_l, p["ml_o"], ml_gain[l])

        mix_l = jnp.concatenate([att_l, gm_l, ml_l], axis=-1) @ w_out[l]
        new_lat = layer_norm(ALPHA * h_lat + mod_l[2] * mix_l, ln1_g[l], ln1_b[l])
        if not last:
            q_c = gqa_heads(rms_norm(split_heads(pc["att_q"], ATT_Q_HEADS, HEAD_DIM), q_gain[l]))
            att_c = merge_gqa(attend(q_c, k_c, v_c))
            gm_c = spatial_gating(pc["gm_u"], pc["gm_v"], gm_ln_g[l], gm_ln_b[l], w_spatial[l], b_spatial[l])
            ml_c = mlstm_output(h_c, pc["ml_o"], ml_gain[l])
            mix_c = jnp.concatenate([att_c, gm_c, ml_c], axis=-1) @ w_out[l]
            h_ctx = layer_norm(ALPHA * h_ctx + mod_c[2] * mix_c, ln1_g[l], ln1_b[l])
        h_lat = new_lat

        b_lat = modulate(h_lat, mod_l[3], mod_l[4])
        moe_l = expert_choice_ffn(b_lat, w_router[l], w_e_gate[l], w_e_up[l], w_e_down[l])
        h_lat = layer_norm(ALPHA * h_lat + mod_l[5] * moe_l, ln2_g[l], ln2_b[l])
        if not last:
            b_ctx = modulate(h_ctx, mod_c[3], mod_c[4])
            moe_c = expert_choice_ffn(b_ctx, w_router[l], w_e_gate[l], w_e_up[l], w_e_down[l])
            h_ctx = layer_norm(ALPHA * h_ctx + mod_c[5] * moe_c, ln2_g[l], ln2_b[l])
    return h_lat
```

```python
import functools

import jax
import jax.numpy as jnp
from jax import lax
from jax.experimental import pallas as pl
from jax.experimental.pallas import tpu as pltpu

D_MODEL = 4096
BATCH = 4
SEQ = 4096
DEPTH = 2
CTX_LEN = 256
GRID_W = 64
HEAD_DIM = 128
ATT_Q_HEADS = 16
ATT_KV_HEADS = 4
ROPE_THETA = 10000.0
GMLP_CHUNK = 128
GMLP_GROUPS = 8
MLSTM_HEADS = 4
MLSTM_CHUNK = 128
N_EXPERTS = 16
CAPACITY_FACTOR = 2
N_MOD = 6
EPS = 1e-6

F32 = jnp.float32
BF16 = jnp.bfloat16

ROW_TILE = 256
MOD_ROWS = 8
VMEM_LIMIT = 56 * 1024 * 1024


def _dims():
    d = D_MODEL
    att_w = ATT_Q_HEADS * HEAD_DIM
    kv_w = ATT_KV_HEADS * HEAD_DIM
    gm_w = d // 4
    ml_w = d // 4
    offs, start = {}, 0
    for name, width in (("att_q", att_w), ("att_k", kv_w), ("att_v", kv_w), ("gm_u", gm_w), ("gm_v", gm_w),
                        ("ml_q", ml_w), ("ml_k", ml_w), ("ml_v", ml_w), ("ml_o", ml_w),
                        ("ml_gates", 4 * MLSTM_HEADS)):
        offs[name] = start
        start += width
    return dict(d=d, att_w=att_w, kv_w=kv_w, gm_w=gm_w, ml_w=ml_w, offs=offs, proj_w=start,
                main_w=offs["ml_gates"], ml_dh=ml_w // MLSTM_HEADS, ff=d // 4)


def _tile(n, prefs):
    for t in prefs:
        if n % t == 0:
            return t
    return n


def _params(*sem):
    return pltpu.CompilerParams(dimension_semantics=sem, vmem_limit_bytes=VMEM_LIMIT)


def _sigmoid(x):
    return 1.0 / (1.0 + jnp.exp(-x))


def _norm_rows(x):
    mu = jnp.mean(x, axis=-1, keepdims=True)
    xc = x - mu
    var = jnp.mean(xc * xc, axis=-1, keepdims=True)
    return xc * lax.rsqrt(var + EPS)


def _gelu_tanh(x):
    return 0.5 * x * (1.0 + jnp.tanh(0.7978845608028654 * (x + 0.044715 * (x * x * x))))


def _mod_body(x_ref, w_ref, b_ref, o_ref):
    x = x_ref[...]
    xs = (x * _sigmoid(x)).astype(BF16)
    o_ref[...] = jnp.dot(xs, w_ref[...].astype(BF16), preferred_element_type=F32) + b_ref[...]


def _mod_vectors(cc, w_mod, b_mod):
    depth, d, n = w_mod.shape
    tn = _tile(n, (512, 256, 128))
    return pl.pallas_call(
        _mod_body,
        grid=(depth, n // tn),
        in_specs=[pl.BlockSpec((MOD_ROWS, d), lambda l, j: (0, 0)),
                  pl.BlockSpec((None, d, tn), lambda l, j: (l, 0, j)),
                  pl.BlockSpec((None, 1, tn), lambda l, j: (l, 0, j))],
        out_specs=pl.BlockSpec((None, MOD_ROWS, tn), lambda l, j: (l, 0, j)),
        out_shape=jax.ShapeDtypeStruct((depth, MOD_ROWS, n), F32),
        compiler_params=_params("parallel", "parallel"),
        name="mod_vectors",
    )(cc, w_mod, b_mod.reshape(depth, 1, n))


def _mod_spec(which, tiles_per_sample, bsz):
    d = D_MODEL
    return pl.BlockSpec((None, None, 1, d),
                        lambda i, *_: (jnp.minimum(i // tiles_per_sample, bsz), which, 0, 0))


def _ln_mod_body(h_ref, shift_ref, scale_ref, o_ref):
    y = _norm_rows(h_ref[...]) * (1.0 + scale_ref[...]) + shift_ref[...]
    o_ref[...] = y.astype(BF16)


def _ln_mod(h, mod, bsz, n):
    rows, d = h.shape
    tm = ROW_TILE
    tps = n // tm
    return pl.pallas_call(
        _ln_mod_body,
        grid=(rows // tm,),
        in_specs=[pl.BlockSpec((tm, d), lambda i: (i, 0)), _mod_spec(0, tps, bsz), _mod_spec(1, tps, bsz)],
        out_specs=pl.BlockSpec((tm, d), lambda i: (i, 0)),
        out_shape=jax.ShapeDtypeStruct((rows, d), BF16),
        compiler_params=_params("parallel"),
        name="ln_mod",
    )(h, mod, mod)


def _mm_body(a_ref, w_ref, b_ref, o_ref):
    o_ref[...] = jnp.dot(a_ref[...], w_ref[...], preferred_element_type=F32) + b_ref[...]


def _matmul(a, w, bias, name):
    m, k = a.shape
    n = w.shape[1]
    tm = _tile(m, (1024, 512, 256))
    tn = _tile(n, (512, 256, 128))
    return pl.pallas_call(
        _mm_body,
        grid=(m // tm, n // tn),
        in_specs=[pl.BlockSpec((tm, k), lambda i, j: (i, 0)),
                  pl.BlockSpec((k, tn), lambda i, j: (0, j)),
                  pl.BlockSpec((1, tn), lambda i, j: (0, j))],
        out_specs=pl.BlockSpec((tm, tn), lambda i, j: (i, j)),
        out_shape=jax.ShapeDtypeStruct((m, n), F32),
        compiler_params=_params("parallel", "parallel"),
        name=name,
    )(a, w, bias)


def _rope_tables(n, tm):
    rows = n // GRID_W
    row = jnp.repeat(jnp.arange(rows), GRID_W).astype(F32)
    col = jnp.tile(jnp.arange(GRID_W), rows).astype(F32)
    n_freq = HEAD_DIM // 4
    inv_freq = ROPE_THETA ** (-jnp.arange(n_freq, dtype=F32) / n_freq)
    ang_r, ang_c = row[:, None] * inv_freq, col[:, None] * inv_freq
    cos = jnp.concatenate([jnp.cos(ang_r), jnp.cos(ang_r), jnp.cos(ang_c), jnp.cos(ang_c)], axis=1)
    sin = jnp.concatenate([-jnp.sin(ang_r), jnp.sin(ang_r), -jnp.sin(ang_c), jnp.sin(ang_c)], axis=1)
    cos = jnp.concatenate([cos, jnp.ones((tm, HEAD_DIM), F32)], axis=0)
    sin = jnp.concatenate([sin, jnp.zeros((tm, HEAD_DIM), F32)], axis=0)
    return cos, sin


def _qkv_prep_body(q_ref, k_ref, v_ref, cos_ref, sin_ref, qg_ref, kg_ref, qo_ref, ko_ref, vo_ref):
    cos, sin = cos_ref[...], sin_ref[...]
    lane = lax.broadcasted_iota(jnp.int32, cos.shape, 1)
    first_half = (lane % (HEAD_DIM // 2)) < (HEAD_DIM // 4)

    def norm_rope(x, gain):
        y = x * lax.rsqrt(jnp.mean(x * x, axis=-1, keepdims=True) + EPS) * gain
        partner = jnp.where(first_half, pltpu.roll(y, HEAD_DIM - HEAD_DIM // 4, 1), pltpu.roll(y, HEAD_DIM // 4, 1))
        return y * cos + partner * sin

    qg = qg_ref[...] * (HEAD_DIM ** -0.5)
    for h in range(ATT_Q_HEADS):
        sl = slice(h * HEAD_DIM, (h + 1) * HEAD_DIM)
        qo_ref[:, sl] = norm_rope(q_ref[:, sl], qg).astype(BF16)
    for h in range(ATT_KV_HEADS):
        sl = slice(h * HEAD_DIM, (h + 1) * HEAD_DIM)
        ko_ref[:, sl] = norm_rope(k_ref[:, sl], kg_ref[...]).astype(BF16)
    vo_ref[...] = v_ref[...].astype(BF16)


def _qkv_prep(p, cos, sin, q_gain, k_gain, bsz, n, ctx_len):
    dm = _dims()
    rows = p.shape[0]
    tm = ROW_TILE
    tps, tpc = n // tm, ctx_len // tm
    nlat = bsz * tps
    att_w, kv_w = dm["att_w"], dm["kv_w"]
    ntot = n + ctx_len

    def table_map(i):
        return (jnp.where(i < nlat, i % tps, tps), 0)

    def kv_map(i):
        j = i - nlat
        return (jnp.where(i < nlat, i // tps, j // tpc), jnp.where(i < nlat, tpc + i % tps, j % tpc), 0)

    return pl.pallas_call(
        _qkv_prep_body,
        grid=(rows // tm,),
        in_specs=[pl.BlockSpec((tm, att_w), lambda i: (i, dm["offs"]["att_q"] // att_w)),
                  pl.BlockSpec((tm, kv_w), lambda i: (i, dm["offs"]["att_k"] // kv_w)),
                  pl.BlockSpec((tm, kv_w), lambda i: (i, dm["offs"]["att_v"] // kv_w)),
                  pl.BlockSpec((tm, HEAD_DIM), table_map),
                  pl.BlockSpec((tm, HEAD_DIM), table_map),
                  pl.BlockSpec((1, HEAD_DIM), lambda i: (0, 0)),
                  pl.BlockSpec((1, HEAD_DIM), lambda i: (0, 0))],
        out_specs=[pl.BlockSpec((tm, att_w), lambda i: (i, 0)),
                   pl.BlockSpec((None, tm, kv_w), kv_map),
                   pl.BlockSpec((None, tm, kv_w), kv_map)],
        out_shape=[jax.ShapeDtypeStruct((rows, att_w), BF16),
                   jax.ShapeDtypeStruct((bsz, ntot, kv_w), BF16),
                   jax.ShapeDtypeStruct((bsz, ntot, kv_w), BF16)],
        compiler_params=_params("parallel"),
        name="qkv_prep",
    )(p, p, p, cos, sin, q_gain.reshape(1, HEAD_DIM), k_gain.reshape(1, HEAD_DIM))


def _attn_heads(q_ref, k, v, o_ref):
    group = ATT_Q_HEADS // ATT_KV_HEADS
    for h in range(group):
        sl = slice(h * HEAD_DIM, (h + 1) * HEAD_DIM)
        s = lax.dot_general(q_ref[:, sl], k, (((1,), (1,)), ((), ())), preferred_element_type=F32)
        e = jnp.exp(s - jnp.max(s, axis=-1, keepdims=True))
        o = jnp.dot(e.astype(BF16), v, preferred_element_type=F32)
        o_ref[:, sl] = (o / jnp.sum(e, axis=-1, keepdims=True)).astype(BF16)


def _attn_body(q_ref, k_ref, v_ref, o_ref, *, lat_q_tiles, ctx_len):
    qi = pl.program_id(2)

    @pl.when(qi < lat_q_tiles)
    def _():
        _attn_heads(q_ref, k_ref[...], v_ref[...], o_ref)

    @pl.when(qi >= lat_q_tiles)
    def _():
        _attn_heads(q_ref, k_ref[0:ctx_len, :], v_ref[0:ctx_len, :], o_ref)


def _attention(qb, kb, vb, bsz, n, ctx_len, with_ctx):
    rows_all = qb.shape[0]
    tq = ROW_TILE
    tps, tpc = n // tq, ctx_len // tq
    nlat = bsz * tps
    group_w = (ATT_Q_HEADS // ATT_KV_HEADS) * HEAD_DIM
    ntot = n + ctx_len
    q_tiles = tps + (tpc if with_ctx else 0)
    rows = rows_all if with_ctx else bsz * n

    def q_map(b, g, qi):
        return (jnp.where(qi < tps, b * tps + qi, nlat + b * tpc + (qi - tps)), g)

    return pl.pallas_call(
        functools.partial(_attn_body, lat_q_tiles=tps, ctx_len=ctx_len),
        grid=(bsz, ATT_KV_HEADS, q_tiles),
        in_specs=[pl.BlockSpec((tq, group_w), q_map),
                  pl.BlockSpec((None, ntot, HEAD_DIM), lambda b, g, qi: (b, 0, g)),
                  pl.BlockSpec((None, ntot, HEAD_DIM), lambda b, g, qi: (b, 0, g))],
        out_specs=pl.BlockSpec((tq, group_w), q_map),
        out_shape=jax.ShapeDtypeStruct((rows, ATT_Q_HEADS * HEAD_DIM), BF16),
        compiler_params=_params("parallel", "parallel", "parallel"),
        name="attention",
    )(qb, kb, vb)


def _gmlp_body(u_ref, v_ref, g_ref, b_ref, ws_ref, bs_ref, o_ref):
    u = _gelu_tanh(u_ref[...])
    v = (_norm_rows(_gelu_tanh(v_ref[...])) * g_ref[...] + b_ref[...]).astype(BF16)
    gd = v.shape[1] // GMLP_GROUPS
    for c in range(v.shape[0] // GMLP_CHUNK):
        rs = slice(c * GMLP_CHUNK, (c + 1) * GMLP_CHUNK)
        for g in range(GMLP_GROUPS):
            cs = slice(g * gd, (g + 1) * gd)
            sv = jnp.dot(ws_ref[g].astype(BF16), v[rs, cs], preferred_element_type=F32) + bs_ref[:, g:g + 1]
            o_ref[rs, cs] = (u[rs, cs] * sv).astype(BF16)


def _gmlp(p, ln_g, ln_b, w_s, b_s):
    dm = _dims()
    rows = p.shape[0]
    tm = ROW_TILE
    gm_w = dm["gm_w"]
    return pl.pallas_call(
        _gmlp_body,
        grid=(rows // tm,),
        in_specs=[pl.BlockSpec((tm, gm_w), lambda i: (i, dm["offs"]["gm_u"] // gm_w)),
                  pl.BlockSpec((tm, gm_w), lambda i: (i, dm["offs"]["gm_v"] // gm_w)),
                  pl.BlockSpec((1, gm_w), lambda i: (0, 0)),
                  pl.BlockSpec((1, gm_w), lambda i: (0, 0)),
                  pl.BlockSpec((GMLP_GROUPS, GMLP_CHUNK, GMLP_CHUNK), lambda i: (0, 0, 0)),
                  pl.BlockSpec((GMLP_CHUNK, GMLP_GROUPS), lambda i: (0, 0))],
        out_specs=pl.BlockSpec((tm, gm_w), lambda i: (i, 0)),
        out_shape=jax.ShapeDtypeStruct((rows, gm_w), BF16),
        compiler_params=_params("parallel"),
        name="gmlp",
    )(p, p, ln_g.reshape(1, gm_w), ln_b.reshape(1, gm_w), w_s, b_s.T)


def _mlstm_body(q_ref, k_ref, v_ref, i_ref, f_ref, o_ref, c_ref, n_ref, m_ref):
    direction = pl.program_id(2)

    @pl.when(pl.program_id(3) == 0)
    def _():
        c_ref[...] = jnp.zeros_like(c_ref)
        n_ref[...] = jnp.zeros_like(n_ref)
        m_ref[...] = jnp.zeros_like(m_ref)

    chunk = q_ref.shape[0]
    row = lax.broadcasted_iota(jnp.int32, (chunk, chunk), 0)
    col = lax.broadcasted_iota(jnp.int32, (chunk, chunk), 1)
    sign = 1 - 2 * direction
    seen = (col - row) * sign <= 0
    seen_t = (row - col) * sign <= 0
    eye = row == col

    def to_col(x_row):
        return jnp.sum(jnp.where(eye, x_row, 0.0), axis=1, keepdims=True)

    f_raw = f_ref[...]
    lf_row = jnp.minimum(f_raw, 0.0) - jnp.log1p(jnp.exp(-jnp.abs(f_raw)))
    i_row = i_ref[...]
    lf_col, i_col = to_col(lf_row), to_col(i_row)
    bcum_col = jnp.sum(jnp.where(seen, lf_row, 0.0), axis=1, keepdims=True)
    bcum_row = jnp.sum(jnp.where(seen_t, lf_col, 0.0), axis=0, keepdims=True)
    b_last = jnp.sum(lf_row, axis=1, keepdims=True)

    m_old = m_ref[...]
    g_row = b_last - bcum_row + i_row
    g_col = b_last - bcum_col + i_col
    m_new = jnp.maximum(b_last + m_old, jnp.max(g_row, axis=1, keepdims=True))
    w_prev = jnp.exp(b_last + m_old - m_new)

    q = q_ref[...]
    k = k_ref[...] * (k_ref.shape[1] ** -0.5)
    vb = v_ref[...].astype(BF16)
    qb, kb = q.astype(BF16), k.astype(BF16)
    c_old, n_old = c_ref[...], n_ref[...]

    a_col = bcum_col + m_old
    dlog = jnp.where(seen, bcum_col - bcum_row + i_row, -jnp.inf)
    mt = jnp.maximum(a_col, jnp.max(dlog, axis=1, keepdims=True))
    s = lax.dot_general(qb, kb, (((1,), (1,)), ((), ())), preferred_element_type=F32) * jnp.exp(dlog - mt)
    wa = jnp.exp(a_col - mt)
    num = (wa * jnp.dot(qb, c_old.astype(BF16), preferred_element_type=F32)
           + jnp.dot(s.astype(BF16), vb, preferred_element_type=F32))
    den = wa * jnp.sum(q * n_old, axis=1, keepdims=True) + jnp.sum(s, axis=1, keepdims=True)
    o_ref[...] = num / jnp.maximum(jnp.abs(den), jnp.exp(-mt))

    kw = k * jnp.exp(g_col - m_new)
    c_ref[...] = w_prev * c_old + lax.dot_general(kw.astype(BF16), vb, (((0,), (0,)), ((), ())),
                                                  preferred_element_type=F32)
    n_ref[...] = w_prev * n_old + jnp.sum(kw, axis=0, keepdims=True)
    m_ref[...] = m_new


def _mlstm(p, gates_rows, bsz, n, ctx_len):
    dm = _dims()
    rows = p.shape[0]
    lc = MLSTM_CHUNK
    dh = dm["ml_dh"]
    heads = MLSTM_HEADS
    ncc, nlc = ctx_len // lc, n // lc
    steps = ncc + nlc
    lat_chunks = bsz * nlc

    def seq_chunk(dr, j):
        fwd = j
        bwd = jnp.where(j < ncc, ncc - 1 - j, ncc + nlc - 1 - (j - ncc))
        return jnp.where(dr == 0, fwd, bwd)

    def row_chunk(b, dr, j):
        c = seq_chunk(dr, j)
        return jnp.where(c < ncc, lat_chunks + b * ncc + c, b * nlc + (c - ncc))

    def proj_spec(name):
        base = dm["offs"][name] // dh
        return pl.BlockSpec((lc, dh), lambda b, h, dr, j: (row_chunk(b, dr, j), base + h))

    def gate_spec(kind):
        return pl.BlockSpec((None, None, 1, lc),
                            lambda b, h, dr, j: (b, (2 * dr + kind) * heads + h, 0, seq_chunk(dr, j)))

    return pl.pallas_call(
        _mlstm_body,
        grid=(bsz, heads, 2, steps),
        in_specs=[proj_spec("ml_q"), proj_spec("ml_k"), proj_spec("ml_v"), gate_spec(0), gate_spec(1)],
        out_specs=pl.BlockSpec((None, lc, dh), lambda b, h, dr, j: (dr, row_chunk(b, dr, j), h)),
        out_shape=jax.ShapeDtypeStruct((2, rows, dm["ml_w"]), F32),
        scratch_shapes=[pltpu.VMEM((dh, dh), F32), pltpu.VMEM((1, dh), F32), pltpu.VMEM((1, 1), F32)],
        compiler_params=_params("parallel", "parallel", "parallel", "arbitrary"),
        name="mlstm_scan",
    )(p, p, p, gates_rows, gates_rows)


def _mlstm_out_body(h_ref, o_ref, gain_ref, out_ref):
    dh = out_ref.shape[1] // MLSTM_HEADS
    for h in range(MLSTM_HEADS):
        sl = slice(h * dh, (h + 1) * dh)
        hs = h_ref[0, :, sl] + h_ref[1, :, sl]
        hn = hs * lax.rsqrt(jnp.mean(hs * hs, axis=-1, keepdims=True) + EPS) * gain_ref[:, sl]
        out_ref[:, sl] = (_sigmoid(o_ref[:, sl]) * hn).astype(BF16)


def _mlstm_out(hd, p, gain, rows):
    dm = _dims()
    tm = ROW_TILE
    ml_w = dm["ml_w"]
    return pl.pallas_call(
        _mlstm_out_body,
        grid=(rows // tm,),
        in_specs=[pl.BlockSpec((2, tm, ml_w), lambda i: (0, i, 0)),
                  pl.BlockSpec((tm, ml_w), lambda i: (i, dm["offs"]["ml_o"] // ml_w)),
                  pl.BlockSpec((1, ml_w), lambda i: (0, 0))],
        out_specs=pl.BlockSpec((tm, ml_w), lambda i: (i, 0)),
        out_shape=jax.ShapeDtypeStruct((rows, ml_w), BF16),
        compiler_params=_params("parallel"),
        name="mlstm_out",
    )(hd, p, gain.reshape(1, ml_w))


def _outproj_body(att_ref, gm_ref, ml_ref, w1_ref, w2_ref, w3_ref, h_ref, gate_ref, g_ref, b_ref,
                  shift_ref, scale_ref, hn_ref, act_ref, acc_ref, *, alpha, tn):
    j = pl.program_id(1)
    mix = (jnp.dot(att_ref[...], w1_ref[...], preferred_element_type=F32)
           + jnp.dot(gm_ref[...], w2_ref[...], preferred_element_type=F32)
           + jnp.dot(ml_ref[...], w3_ref[...], preferred_element_type=F32))
    acc_ref[:, pl.ds(pl.multiple_of(j * tn, tn), tn)] = mix

    @pl.when(j == pl.num_programs(1) - 1)
    def _():
        y = alpha * h_ref[...] + gate_ref[...] * acc_ref[...]
        hn = _norm_rows(y) * g_ref[...] + b_ref[...]
        hn_ref[...] = hn
        act_ref[...] = (_norm_rows(hn) * (1.0 + scale_ref[...]) + shift_ref[...]).astype(BF16)


def _outproj(att, gm, ml, w_out, h, mod, ln_g, ln_b, rows, bsz, n, alpha):
    dm = _dims()
    d = dm["d"]
    tm = ROW_TILE
    tn = _tile(d, (512, 256, 128))
    tps = n // tm
    att_w, gm_w, ml_w = dm["att_w"], dm["gm_w"], dm["ml_w"]
    vec = pl.BlockSpec((1, d), lambda i, j: (0, 0))
    return pl.pallas_call(
        functools.partial(_outproj_body, alpha=alpha, tn=tn),
        grid=(rows // tm, d // tn),
        in_specs=[pl.BlockSpec((tm, att_w), lambda i, j: (i, 0)),
                  pl.BlockSpec((tm, gm_w), lambda i, j: (i, 0)),
                  pl.BlockSpec((tm, ml_w), lambda i, j: (i, 0)),
                  pl.BlockSpec((att_w, tn), lambda i, j: (0, j)),
                  pl.BlockSpec((gm_w, tn), lambda i, j: (att_w // gm_w, j)),
                  pl.BlockSpec((ml_w, tn), lambda i, j: ((att_w + gm_w) // ml_w, j)),
                  pl.BlockSpec((tm, d), lambda i, j: (i, 0)),
                  _mod_spec(2, tps, bsz), vec, vec, _mod_spec(3, tps, bsz), _mod_spec(4, tps, bsz)],
        out_specs=[pl.BlockSpec((tm, d), lambda i, j: (i, 0)), pl.BlockSpec((tm, d), lambda i, j: (i, 0))],
        out_shape=[jax.ShapeDtypeStruct((rows, d), F32), jax.ShapeDtypeStruct((rows, d), BF16)],
        scratch_shapes=[pltpu.VMEM((tm, d), F32)],
        compiler_params=_params("parallel", "arbitrary"),
        name="outproj_ln",
    )(att, gm, ml, w_out, w_out, w_out, h, mod, ln_g.reshape(1, d), ln_b.reshape(1, d), mod, mod)


def _expert_up_body(x_ref, wg_ref, wu_ref, o_ref):
    x = x_ref[...]
    g = jnp.dot(x, wg_ref[...], preferred_element_type=F32)
    u = jnp.dot(x, wu_ref[...], preferred_element_type=F32)
    o_ref[...] = (g * _sigmoid(g) * u).astype(BF16)


def _expert_up(xe, wg, wu):
    bsz, ne, cap, d = xe.shape
    ff = wg.shape[2]
    tf = _tile(ff, (512, 256, 128))
    return pl.pallas_call(
        _expert_up_body,
        grid=(ne, ff // tf, bsz),
        in_specs=[pl.BlockSpec((None, None, cap, d), lambda e, f, b: (b, e, 0, 0)),
                  pl.BlockSpec((None, d, tf), lambda e, f, b: (e, 0, f)),
                  pl.BlockSpec((None, d, tf), lambda e, f, b: (e, 0, f))],
        out_specs=pl.BlockSpec((None, None, cap, tf), lambda e, f, b: (b, e, 0, f)),
        out_shape=jax.ShapeDtypeStruct((bsz, ne, cap, ff), BF16),
        compiler_params=_params("parallel", "parallel", "parallel"),
        name="expert_up",
    )(xe, wg, wu)


def _expert_down_body(x_ref, w_ref, gate_ref, o_ref):
    o_ref[...] = jnp.dot(x_ref[...], w_ref[...], preferred_element_type=F32) * gate_ref[...]


def _expert_down(hid, wd, gate):
    bsz, ne, cap, ff = hid.shape
    d = wd.shape[2]
    return pl.pallas_call(
        _expert_down_body,
        grid=(ne, bsz),
        in_specs=[pl.BlockSpec((None, None, cap, ff), lambda e, b: (b, e, 0, 0)),
                  pl.BlockSpec((None, ff, d), lambda e, b: (e, 0, 0)),
                  pl.BlockSpec((None, None, cap, 1), lambda e, b: (b, e, 0, 0))],
        out_specs=pl.BlockSpec((None, None, cap, d), lambda e, b: (b, e, 0, 0)),
        out_shape=jax.ShapeDtypeStruct((bsz, ne, cap, d), F32),
        compiler_params=_params("parallel", "parallel"),
        name="expert_down",
    )(hid, wd, gate[..., None])


def _resid_body(h_ref, moe_ref, gate_ref, g_ref, b_ref, shift_ref, scale_ref, hn_ref, act_ref, *, alpha):
    hn = _norm_rows(alpha * h_ref[...] + gate_ref[...] * moe_ref[...]) * g_ref[...] + b_ref[...]
    hn_ref[...] = hn
    act_ref[...] = (_norm_rows(hn) * (1.0 + scale_ref[...]) + shift_ref[...]).astype(BF16)


def _resid(h, moe, mod, next_mod, ln_g, ln_b, bsz, n, alpha):
    rows, d = h.shape
    tm = ROW_TILE
    tps = n // tm
    vec = pl.BlockSpec((1, d), lambda i: (0, 0))
    blk = pl.BlockSpec((tm, d), lambda i: (i, 0))
    return pl.pallas_call(
        functools.partial(_resid_body, alpha=alpha),
        grid=(rows // tm,),
        in_specs=[blk, blk, _mod_spec(5, tps, bsz), vec, vec, _mod_spec(0, tps, bsz), _mod_spec(1, tps, bsz)],
        out_specs=[blk, blk],
        out_shape=[jax.ShapeDtypeStruct((rows, d), F32), jax.ShapeDtypeStruct((rows, d), BF16)],
        compiler_params=_params("parallel"),
        name="resid_ln",
    )(h, moe, mod, ln_g.reshape(1, d), ln_b.reshape(1, d), next_mod, next_mod)


def _route(logits, bsz, n, ctx_len, with_ctx):
    ne = N_EXPERTS

    def choose(lg, count, row0):
        aff = jax.nn.softmax(lg.reshape(bsz, count, ne), axis=-1)
        gate, idx = lax.top_k(jnp.swapaxes(aff, 1, 2), CAPACITY_FACTOR * count // ne)
        return gate, idx + (row0 + count * jnp.arange(bsz, dtype=idx.dtype))[:, None, None]

    gate, idx = choose(logits[:bsz * n, :ne], n, 0)
    if with_ctx:
        gate_c, idx_c = choose(logits[bsz * n:, :ne], ctx_len, bsz * n)
        gate, idx = jnp.concatenate([gate, gate_c], axis=2), jnp.concatenate([idx, idx_c], axis=2)
    return gate, idx


def kernel(x, c, ctx, c_ctx, w_mod, b_mod, w_in, q_gain, k_gain, gm_ln_g, gm_ln_b, w_spatial, b_spatial, b_gates,
           ml_gain, w_out, ln1_g, ln1_b, w_router, w_e_gate, w_e_up, w_e_down, ln2_g, ln2_b):
    dm = _dims()
    bsz, n, d = x.shape
    ctx_len = ctx.shape[1]
    depth = w_mod.shape[0]
    alpha = (2 * depth) ** 0.25
    lat_rows = bsz * n
    main_w = dm["main_w"]
    n_gates = 4 * MLSTM_HEADS

    cc = jnp.concatenate([c, c_ctx[None], jnp.zeros((MOD_ROWS - bsz - 1, d), F32)], axis=0)
    mods = _mod_vectors(cc, w_mod, b_mod).reshape(depth, MOD_ROWS, N_MOD, 1, d)
    cos, sin = _rope_tables(n, ROW_TILE)

    h = jnp.concatenate([x.reshape(lat_rows, d), ctx.reshape(bsz * ctx_len, d)], axis=0)
    act = _ln_mod(h, mods[0], bsz, n)
    for l in range(depth):
        last = l == depth - 1
        rows = lat_rows if last else h.shape[0]
        mod = mods[l]

        w_in_b = w_in[l].astype(BF16)
        p = _matmul(act, w_in_b[:, :main_w], jnp.zeros((1, main_w), F32), "in_proj")
        w_g = jnp.pad(w_in_b[:, main_w:], ((0, 0), (0, HEAD_DIM - n_gates)))
        b_g = jnp.pad(b_gates[l], (0, HEAD_DIM - n_gates)).reshape(1, HEAD_DIM)
        gates = _matmul(act, w_g, b_g, "gate_proj")[:, :n_gates]
        gates_seq = jnp.concatenate([gates[lat_rows:].reshape(bsz, ctx_len, n_gates),
                                     gates[:lat_rows].reshape(bsz, n, n_gates)], axis=1)
        gates_rows = jnp.swapaxes(gates_seq, 1, 2).reshape(bsz, n_gates, 1, ctx_len + n)

        qb, kb, vb = _qkv_prep(p, cos, sin, q_gain[l], k_gain[l], bsz, n, ctx_len)
        att = _attention(qb, kb, vb, bsz, n, ctx_len, not last)
        gm = _gmlp(p, gm_ln_g[l], gm_ln_b[l], w_spatial[l], b_spatial[l])
        hd = _mlstm(p, gates_rows, bsz, n, ctx_len)
        ml = _mlstm_out(hd, p, ml_gain[l], rows)
        h, act = _outproj(att, gm, ml, w_out[l].astype(BF16), h, mod, ln1_g[l], ln1_b[l], rows, bsz, n, alpha)

        w_r = jnp.pad(w_router[l].astype(BF16), ((0, 0), (0, HEAD_DIM - N_EXPERTS)))
        logits = _matmul(act, w_r, jnp.zeros((1, HEAD_DIM), F32), "router")
        gate, idx = _route(logits, bsz, n, ctx_len, not last)
        xe = act[idx]
        hid = _expert_up(xe, w_e_gate[l].astype(BF16), w_e_up[l].astype(BF16))
        ye = _expert_down(hid, w_e_down[l].astype(BF16), gate)
        moe = jnp.zeros((rows, d), F32).at[idx.reshape(-1)].add(ye.reshape(-1, d))
        h, act = _resid(h, moe, mod, mods[(l + 1) % depth], ln2_g[l], ln2_b[l], bsz, n, alpha)
    return h[:lat_rows].reshape(bsz, n, d)
```

```python
import functools

import jax
import jax.numpy as jnp
from jax import lax
from jax.experimental import pallas as pl
from jax.experimental.pallas import tpu as pltpu

D_MODEL = 4096
BATCH = 4
SEQ = 4096
DEPTH = 2
CTX_LEN = 256
GRID_W = 64
HEAD_DIM = 128
ATT_Q_HEADS = 16
ATT_KV_HEADS = 4
ROPE_THETA = 10000.0
GMLP_CHUNK = 128
GMLP_GROUPS = 8
MLSTM_HEADS = 4
MLSTM_CHUNK = 128
N_EXPERTS = 16
CAPACITY_FACTOR = 2
N_MOD = 6
EPS = 1e-6
LOG2_E = 1.4426950408889634

F32 = jnp.float32
BF16 = jnp.bfloat16

ROW_TILE = 256
MOD_ROWS = 8
VMEM_LIMIT = 56 * 1024 * 1024


def _dims():
    d = D_MODEL
    att_w = ATT_Q_HEADS * HEAD_DIM
    kv_w = ATT_KV_HEADS * HEAD_DIM
    gm_w = d // 4
    ml_w = d // 4
    offs, start = {}, 0
    for name, width in (("att_q", att_w), ("att_k", kv_w), ("att_v", kv_w), ("gm_u", gm_w), ("gm_v", gm_w),
                        ("ml_q", ml_w), ("ml_k", ml_w), ("ml_v", ml_w), ("ml_o", ml_w),
                        ("ml_gates", 4 * MLSTM_HEADS)):
        offs[name] = start
        start += width
    return dict(d=d, att_w=att_w, kv_w=kv_w, gm_w=gm_w, ml_w=ml_w, offs=offs, proj_w=start,
                main_w=offs["ml_gates"], ml_dh=ml_w // MLSTM_HEADS, ff=d // 4)


def _tile(n, prefs):
    for t in prefs:
        if n % t == 0:
            return t
    return n


def _params(*sem):
    return pltpu.CompilerParams(dimension_semantics=sem, vmem_limit_bytes=VMEM_LIMIT)


def _sigmoid(x):
    return 1.0 / (1.0 + jnp.exp(-x))


def _norm_rows(x):
    mu = jnp.mean(x, axis=-1, keepdims=True)
    xc = x - mu
    var = jnp.mean(xc * xc, axis=-1, keepdims=True)
    return xc * lax.rsqrt(var + EPS)


def _gelu_tanh(x):
    return 0.5 * x * (1.0 + jnp.tanh(0.7978845608028654 * (x + 0.044715 * (x * x * x))))


def _mod_body(x_ref, w_ref, b_ref, o_ref):
    x = x_ref[...]
    xs = (x * _sigmoid(x)).astype(BF16)
    o_ref[...] = jnp.dot(xs, w_ref[...].astype(BF16), preferred_element_type=F32) + b_ref[...]


def _mod_vectors(cc, w_mod, b_mod):
    depth, d, n = w_mod.shape
    tn = _tile(n, (512, 256, 128))
    return pl.pallas_call(
        _mod_body,
        grid=(depth, n // tn),
        in_specs=[pl.BlockSpec((MOD_ROWS, d), lambda l, j: (0, 0)),
                  pl.BlockSpec((None, d, tn), lambda l, j: (l, 0, j)),
                  pl.BlockSpec((None, 1, tn), lambda l, j: (l, 0, j))],
        out_specs=pl.BlockSpec((None, MOD_ROWS, tn), lambda l, j: (l, 0, j)),
        out_shape=jax.ShapeDtypeStruct((depth, MOD_ROWS, n), F32),
        compiler_params=_params("parallel", "parallel"),
        name="mod_vectors",
    )(cc, w_mod, b_mod.reshape(depth, 1, n))


def _mod_spec(which, tiles_per_sample, bsz):
    d = D_MODEL
    return pl.BlockSpec((None, None, 1, d),
                        lambda i, *_: (jnp.minimum(i // tiles_per_sample, bsz), which, 0, 0))


def _ln_mod_body(h_ref, shift_ref, scale_ref, o_ref):
    y = _norm_rows(h_ref[...]) * (1.0 + scale_ref[...]) + shift_ref[...]
    o_ref[...] = y.astype(BF16)


def _ln_mod(h, mod, bsz, n):
    rows, d = h.shape
    tm = ROW_TILE
    tps = n // tm
    return pl.pallas_call(
        _ln_mod_body,
        grid=(rows // tm,),
        in_specs=[pl.BlockSpec((tm, d), lambda i: (i, 0)), _mod_spec(0, tps, bsz), _mod_spec(1, tps, bsz)],
        out_specs=pl.BlockSpec((tm, d), lambda i: (i, 0)),
        out_shape=jax.ShapeDtypeStruct((rows, d), BF16),
        compiler_params=_params("parallel"),
        name="ln_mod",
    )(h, mod, mod)


def _mm_body(a_ref, w_ref, b_ref, o_ref):
    o_ref[...] = jnp.dot(a_ref[...], w_ref[...], preferred_element_type=F32) + b_ref[...]


def _matmul(a, w, bias, name):
    m, k = a.shape
    n = w.shape[1]
    tm = _tile(m, (1024, 512, 256))
    tn = _tile(n, (512, 256, 128))
    return pl.pallas_call(
        _mm_body,
        grid=(m // tm, n // tn),
        in_specs=[pl.BlockSpec((tm, k), lambda i, j: (i, 0)),
                  pl.BlockSpec((k, tn), lambda i, j: (0, j)),
                  pl.BlockSpec((1, tn), lambda i, j: (0, j))],
        out_specs=pl.BlockSpec((tm, tn), lambda i, j: (i, j)),
        out_shape=jax.ShapeDtypeStruct((m, n), F32),
        compiler_params=_params("parallel", "parallel"),
        name=name,
    )(a, w, bias)


def _in_proj_body(a_ref, w_ref, o_ref, wb_ref):
    @pl.when(pl.program_id(1) == 0)
    def _():
        wb_ref[...] = w_ref[...].astype(BF16)

    o_ref[...] = jnp.dot(a_ref[...], wb_ref[...], preferred_element_type=F32)


def _in_proj(a, w_in, layer, n_out):
    m, k = a.shape
    tm = _tile(m, (1024, 512, 256))
    tn = _tile(n_out, (512, 256, 128))
    return pl.pallas_call(
        _in_proj_body,
        grid=(n_out // tn, m // tm),
        in_specs=[pl.BlockSpec((tm, k), lambda j, i: (i, 0)),
                  pl.BlockSpec((None, k, tn), lambda j, i: (layer, 0, j))],
        out_specs=pl.BlockSpec((tm, tn), lambda j, i: (i, j)),
        out_shape=jax.ShapeDtypeStruct((m, n_out), F32),
        scratch_shapes=[pltpu.VMEM((k, tn), BF16)],
        compiler_params=_params("parallel", "arbitrary"),
        name="in_proj",
    )(a, w_in)


def _rope_tables(n, tm):
    rows = n // GRID_W
    row = jnp.repeat(jnp.arange(rows), GRID_W).astype(F32)
    col = jnp.tile(jnp.arange(GRID_W), rows).astype(F32)
    n_freq = HEAD_DIM // 4
    inv_freq = ROPE_THETA ** (-jnp.arange(n_freq, dtype=F32) / n_freq)
    ang_r, ang_c = row[:, None] * inv_freq, col[:, None] * inv_freq
    cos = jnp.concatenate([jnp.cos(ang_r), jnp.cos(ang_r), jnp.cos(ang_c), jnp.cos(ang_c)], axis=1)
    sin = jnp.concatenate([-jnp.sin(ang_r), jnp.sin(ang_r), -jnp.sin(ang_c), jnp.sin(ang_c)], axis=1)
    cos = jnp.concatenate([cos, jnp.ones((tm, HEAD_DIM), F32)], axis=0)
    sin = jnp.concatenate([sin, jnp.zeros((tm, HEAD_DIM), F32)], axis=0)
    return cos, sin


def _qkv_prep_body(q_ref, k_ref, v_ref, cos_ref, sin_ref, qg_ref, kg_ref, qo_ref, ko_ref, vo_ref):
    cos, sin = cos_ref[...], sin_ref[...]
    lane = lax.broadcasted_iota(jnp.int32, cos.shape, 1)
    first_half = (lane % (HEAD_DIM // 2)) < (HEAD_DIM // 4)

    def norm_rope(x, gain):
        y = x * lax.rsqrt(jnp.mean(x * x, axis=-1, keepdims=True) + EPS) * gain
        partner = jnp.where(first_half, pltpu.roll(y, HEAD_DIM - HEAD_DIM // 4, 1), pltpu.roll(y, HEAD_DIM // 4, 1))
        return y * cos + partner * sin

    qg = qg_ref[...] * (HEAD_DIM ** -0.5 * LOG2_E)
    for h in range(ATT_Q_HEADS):
        sl = slice(h * HEAD_DIM, (h + 1) * HEAD_DIM)
        qo_ref[:, sl] = norm_rope(q_ref[:, sl], qg).astype(BF16)
    for h in range(ATT_KV_HEADS):
        sl = slice(h * HEAD_DIM, (h + 1) * HEAD_DIM)
        ko_ref[:, sl] = norm_rope(k_ref[:, sl], kg_ref[...]).astype(BF16)
    vo_ref[...] = v_ref[...].astype(BF16)


def _qkv_prep(p, cos, sin, q_gain, k_gain, bsz, n, ctx_len):
    dm = _dims()
    rows = p.shape[0]
    tm = ROW_TILE
    tps, tpc = n // tm, ctx_len // tm
    nlat = bsz * tps
    att_w, kv_w = dm["att_w"], dm["kv_w"]
    ntot = n + ctx_len

    def table_map(i):
        return (jnp.where(i < nlat, i % tps, tps), 0)

    def kv_map(i):
        j = i - nlat
        return (jnp.where(i < nlat, i // tps, j // tpc), jnp.where(i < nlat, tpc + i % tps, j % tpc), 0)

    return pl.pallas_call(
        _qkv_prep_body,
        grid=(rows // tm,),
        in_specs=[pl.BlockSpec((tm, att_w), lambda i: (i, dm["offs"]["att_q"] // att_w)),
                  pl.BlockSpec((tm, kv_w), lambda i: (i, dm["offs"]["att_k"] // kv_w)),
                  pl.BlockSpec((tm, kv_w), lambda i: (i, dm["offs"]["att_v"] // kv_w)),
                  pl.BlockSpec((tm, HEAD_DIM), table_map),
                  pl.BlockSpec((tm, HEAD_DIM), table_map),
                  pl.BlockSpec((1, HEAD_DIM), lambda i: (0, 0)),
                  pl.BlockSpec((1, HEAD_DIM), lambda i: (0, 0))],
        out_specs=[pl.BlockSpec((tm, att_w), lambda i: (i, 0)),
                   pl.BlockSpec((None, tm, kv_w), kv_map),
                   pl.BlockSpec((None, tm, kv_w), kv_map)],
        out_shape=[jax.ShapeDtypeStruct((rows, att_w), BF16),
                   jax.ShapeDtypeStruct((bsz, ntot, kv_w), BF16),
                   jax.ShapeDtypeStruct((bsz, ntot, kv_w), BF16)],
        compiler_params=_params("parallel"),
        name="qkv_prep",
    )(p, p, p, cos, sin, q_gain.reshape(1, HEAD_DIM), k_gain.reshape(1, HEAD_DIM))


def _attn_heads(q_ref, k, v, o_ref):
    group = ATT_Q_HEADS // ATT_KV_HEADS
    for h in range(group):
        sl = slice(h * HEAD_DIM, (h + 1) * HEAD_DIM)
        s = lax.dot_general(q_ref[:, sl], k, (((1,), (1,)), ((), ())), preferred_element_type=F32)
        e = jnp.exp2(s - jnp.max(s, axis=-1, keepdims=True))
        o = jnp.dot(e.astype(BF16), v, preferred_element_type=F32)
        o_ref[:, sl] = (o / jnp.sum(e, axis=-1, keepdims=True)).astype(BF16)


def _attn_body(q_ref, k_ref, v_ref, o_ref, *, lat_q_tiles, ctx_len):
    qi = pl.program_id(2)

    @pl.when(qi < lat_q_tiles)
    def _():
        _attn_heads(q_ref, k_ref[...], v_ref[...], o_ref)

    @pl.when(qi >= lat_q_tiles)
    def _():
        _attn_heads(q_ref, k_ref[0:ctx_len, :], v_ref[0:ctx_len, :], o_ref)


def _attention(qb, kb, vb, bsz, n, ctx_len, with_ctx):
    rows_all = qb.shape[0]
    tq = ROW_TILE
    tps, tpc = n // tq, ctx_len // tq
    nlat = bsz * tps
    group_w = (ATT_Q_HEADS // ATT_KV_HEADS) * HEAD_DIM
    ntot = n + ctx_len
    q_tiles = tps + (tpc if with_ctx else 0)
    rows = rows_all if with_ctx else bsz * n

    def q_map(b, g, qi):
        return (jnp.where(qi < tps, b * tps + qi, nlat + b * tpc + (qi - tps)), g)

    return pl.pallas_call(
        functools.partial(_attn_body, lat_q_tiles=tps, ctx_len=ctx_len),
        grid=(bsz, ATT_KV_HEADS, q_tiles),
        in_specs=[pl.BlockSpec((tq, group_w), q_map),
                  pl.BlockSpec((None, ntot, HEAD_DIM), lambda b, g, qi: (b, 0, g)),
                  pl.BlockSpec((None, ntot, HEAD_DIM), lambda b, g, qi: (b, 0, g))],
        out_specs=pl.BlockSpec((tq, group_w), q_map),
        out_shape=jax.ShapeDtypeStruct((rows, ATT_Q_HEADS * HEAD_DIM), BF16),
        compiler_params=_params("parallel", "parallel", "parallel"),
        name="attention",
    )(qb, kb, vb)


def _gmlp_body(u_ref, v_ref, g_ref, b_ref, ws_ref, bs_ref, o_ref):
    u = _gelu_tanh(u_ref[...])
    v = (_norm_rows(_gelu_tanh(v_ref[...])) * g_ref[...] + b_ref[...]).astype(BF16)
    gd = v.shape[1] // GMLP_GROUPS
    for c in range(v.shape[0] // GMLP_CHUNK):
        rs = slice(c * GMLP_CHUNK, (c + 1) * GMLP_CHUNK)
        for g in range(GMLP_GROUPS):
            cs = slice(g * gd, (g + 1) * gd)
            sv = jnp.dot(ws_ref[g].astype(BF16), v[rs, cs], preferred_element_type=F32) + bs_ref[:, g:g + 1]
            o_ref[rs, cs] = (u[rs, cs] * sv).astype(BF16)


def _gmlp(p, ln_g, ln_b, w_s, b_s):
    dm = _dims()
    rows = p.shape[0]
    tm = ROW_TILE
    gm_w = dm["gm_w"]
    return pl.pallas_call(
        _gmlp_body,
        grid=(rows // tm,),
        in_specs=[pl.BlockSpec((tm, gm_w), lambda i: (i, dm["offs"]["gm_u"] // gm_w)),
                  pl.BlockSpec((tm, gm_w), lambda i: (i, dm["offs"]["gm_v"] // gm_w)),
                  pl.BlockSpec((1, gm_w), lambda i: (0, 0)),
                  pl.BlockSpec((1, gm_w), lambda i: (0, 0)),
                  pl.BlockSpec((GMLP_GROUPS, GMLP_CHUNK, GMLP_CHUNK), lambda i: (0, 0, 0)),
                  pl.BlockSpec((GMLP_CHUNK, GMLP_GROUPS), lambda i: (0, 0))],
        out_specs=pl.BlockSpec((tm, gm_w), lambda i: (i, 0)),
        out_shape=jax.ShapeDtypeStruct((rows, gm_w), BF16),
        compiler_params=_params("parallel"),
        name="gmlp",
    )(p, p, ln_g.reshape(1, gm_w), ln_b.reshape(1, gm_w), w_s, b_s.T)


def _mlstm_chain_step(q_ref, k_ref, v_ref, g_ref, o_ref, c_ref, n_ref, m_ref, direction, head, cols, masks):
    seen, seen_t, eye = masks
    heads = MLSTM_HEADS

    def to_col(x_row):
        return jnp.sum(jnp.where(eye, x_row, 0.0), axis=1, keepdims=True)

    gi = 2 * direction * heads + head
    i_row = g_ref[gi:gi + 1, :]
    f_raw = g_ref[gi + heads:gi + heads + 1, :]
    lf_row = jnp.minimum(f_raw, 0.0) - jnp.log1p(jnp.exp(-jnp.abs(f_raw)))
    lf_col, i_col = to_col(lf_row), to_col(i_row)
    bcum_col = jnp.sum(jnp.where(seen, lf_row, 0.0), axis=1, keepdims=True)
    bcum_row = jnp.sum(jnp.where(seen_t, lf_col, 0.0), axis=0, keepdims=True)
    b_last = jnp.sum(lf_row, axis=1, keepdims=True)

    m_old = m_ref[direction, head]
    g_row = b_last - bcum_row + i_row
    g_col = b_last - bcum_col + i_col
    m_new = jnp.maximum(b_last + m_old, jnp.max(g_row, axis=1, keepdims=True))
    w_prev = jnp.exp(b_last + m_old - m_new)

    q = q_ref[:, cols]
    k = k_ref[:, cols] * ((cols.stop - cols.start) ** -0.5)
    vb = v_ref[:, cols].astype(BF16)
    qb, kb = q.astype(BF16), k.astype(BF16)
    c_old, n_old = c_ref[direction, head], n_ref[direction, head]

    a_col = bcum_col + m_old
    dlog = jnp.where(seen, bcum_col - bcum_row + i_row, -jnp.inf)
    mt = jnp.maximum(a_col, jnp.max(dlog, axis=1, keepdims=True))
    s = lax.dot_general(qb, kb, (((1,), (1,)), ((), ())), preferred_element_type=F32) * jnp.exp(dlog - mt)
    wa = jnp.exp(a_col - mt)
    num = (wa * jnp.dot(qb, c_old.astype(BF16), preferred_element_type=F32)
           + jnp.dot(s.astype(BF16), vb, preferred_element_type=F32))
    den = wa * jnp.sum(q * n_old, axis=1, keepdims=True) + jnp.sum(s, axis=1, keepdims=True)
    o_ref[:, cols] = num / jnp.maximum(jnp.abs(den), jnp.exp(-mt))

    kw = k * jnp.exp(g_col - m_new)
    c_ref[direction, head] = w_prev * c_old + lax.dot_general(kw.astype(BF16), vb, (((0,), (0,)), ((), ())),
                                                              preferred_element_type=F32)
    n_ref[direction, head] = w_prev * n_old + jnp.sum(kw, axis=0, keepdims=True)
    m_ref[direction, head] = m_new


def _mlstm_body(qf_ref, kf_ref, vf_ref, gf_ref, qb_ref, kb_ref, vb_ref, gb_ref, of_ref, ob_ref,
                c_ref, n_ref, m_ref):
    @pl.when(pl.program_id(1) == 0)
    def _():
        c_ref[...] = jnp.zeros_like(c_ref)
        n_ref[...] = jnp.zeros_like(n_ref)
        m_ref[...] = jnp.zeros_like(m_ref)

    chunk = qf_ref.shape[0]
    dh = qf_ref.shape[1] // MLSTM_HEADS
    row = lax.broadcasted_iota(jnp.int32, (chunk, chunk), 0)
    col = lax.broadcasted_iota(jnp.int32, (chunk, chunk), 1)
    eye = row == col
    fwd_masks = (col <= row, row <= col, eye)
    bwd_masks = (col >= row, row >= col, eye)
    for head in range(MLSTM_HEADS):
        cols = slice(head * dh, (head + 1) * dh)
        _mlstm_chain_step(qf_ref, kf_ref, vf_ref, gf_ref, of_ref, c_ref, n_ref, m_ref, 0, head, cols, fwd_masks)
        _mlstm_chain_step(qb_ref, kb_ref, vb_ref, gb_ref, ob_ref, c_ref, n_ref, m_ref, 1, head, cols, bwd_masks)


def _mlstm(p, gates_rows, bsz, n, ctx_len):
    dm = _dims()
    rows = p.shape[0]
    lc = MLSTM_CHUNK
    dh, ml_w = dm["ml_dh"], dm["ml_w"]
    heads = MLSTM_HEADS
    ncc, nlc = ctx_len // lc, n // lc
    steps = ncc + nlc
    lat_chunks = bsz * nlc

    def seq_chunk(dr, j):
        return j if dr == 0 else jnp.where(j < ncc, ncc - 1 - j, ncc + nlc - 1 - (j - ncc))

    def row_chunk(b, dr, j):
        c = seq_chunk(dr, j)
        return jnp.where(c < ncc, lat_chunks + b * ncc + c, b * nlc + (c - ncc))

    def proj_spec(name, dr):
        return pl.BlockSpec((lc, ml_w), lambda b, j: (row_chunk(b, dr, j), dm["offs"][name] // ml_w))

    def dir_specs(dr):
        return [proj_spec("ml_q", dr), proj_spec("ml_k", dr), proj_spec("ml_v", dr),
                pl.BlockSpec((None, 4 * heads, lc), lambda b, j: (b, 0, seq_chunk(dr, j)))]

    return pl.pallas_call(
        _mlstm_body,
        grid=(bsz, steps),
        in_specs=dir_specs(0) + dir_specs(1),
        out_specs=[pl.BlockSpec((lc, ml_w), lambda b, j: (row_chunk(b, 0, j), 0)),
                   pl.BlockSpec((lc, ml_w), lambda b, j: (row_chunk(b, 1, j), 0))],
        out_shape=[jax.ShapeDtypeStruct((rows, ml_w), F32), jax.ShapeDtypeStruct((rows, ml_w), F32)],
        scratch_shapes=[pltpu.VMEM((2, heads, dh, dh), F32), pltpu.VMEM((2, heads, 1, dh), F32),
                        pltpu.VMEM((2, heads, 1, 1), F32)],
        compiler_params=_params("parallel", "arbitrary"),
        name="mlstm_scan",
    )(p, p, p, gates_rows, p, p, p, gates_rows)


def _mlstm_out_body(hf_ref, hb_ref, o_ref, gain_ref, out_ref):
    dh = out_ref.shape[1] // MLSTM_HEADS
    for h in range(MLSTM_HEADS):
        sl = slice(h * dh, (h + 1) * dh)
        hs = hf_ref[:, sl] + hb_ref[:, sl]
        hn = hs * lax.rsqrt(jnp.mean(hs * hs, axis=-1, keepdims=True) + EPS) * gain_ref[:, sl]
        out_ref[:, sl] = (_sigmoid(o_ref[:, sl]) * hn).astype(BF16)


def _mlstm_out(hf, hb, p, gain, rows):
    dm = _dims()
    tm = ROW_TILE
    ml_w = dm["ml_w"]
    blk = pl.BlockSpec((tm, ml_w), lambda i: (i, 0))
    return pl.pallas_call(
        _mlstm_out_body,
        grid=(rows // tm,),
        in_specs=[blk, blk,
                  pl.BlockSpec((tm, ml_w), lambda i: (i, dm["offs"]["ml_o"] // ml_w)),
                  pl.BlockSpec((1, ml_w), lambda i: (0, 0))],
        out_specs=blk,
        out_shape=jax.ShapeDtypeStruct((rows, ml_w), BF16),
        compiler_params=_params("parallel"),
        name="mlstm_out",
    )(hf, hb, p, gain.reshape(1, ml_w))


def _pack_halves(y):
    half = y.shape[1] // 2
    lo = lax.bitcast_convert_type(y[:, :half].astype(BF16).astype(F32), jnp.uint32) >> 16
    hi = lax.bitcast_convert_type(y[:, half:].astype(BF16).astype(F32), jnp.uint32) & jnp.uint32(0xFFFF0000)
    return lo | hi


def _unpack_halves(w):
    lo = lax.bitcast_convert_type(w << 16, F32).astype(BF16)
    hi = lax.bitcast_convert_type(w & jnp.uint32(0xFFFF0000), F32).astype(BF16)
    return lo, hi


def _outproj_body(att_ref, gm_ref, ml_ref, w1_ref, w2_ref, w3_ref, h_ref, gate_ref, g_ref, b_ref,
                  shift_ref, scale_ref, hn_ref, act_ref, *, alpha, tn, sub):
    j = pl.program_id(1)
    cols = pl.ds(pl.multiple_of(j * tn, tn), tn)
    mix = (jnp.dot(att_ref[...], w1_ref[...], preferred_element_type=F32)
           + jnp.dot(gm_ref[...], w2_ref[...], preferred_element_type=F32)
           + jnp.dot(ml_ref[...], w3_ref[...], preferred_element_type=F32))
    hn_ref[:, cols] = alpha * h_ref[...] + gate_ref[:, cols] * mix

    @pl.when(j == pl.num_programs(1) - 1)
    def _():
        for r in range(hn_ref.shape[0] // sub):
            rs = slice(r * sub, (r + 1) * sub)
            hn = _norm_rows(hn_ref[rs, :]) * g_ref[...] + b_ref[...]
            hn_ref[rs, :] = hn
            act_ref[rs, :] = _pack_halves(_norm_rows(hn) * (1.0 + scale_ref[...]) + shift_ref[...])


def _outproj(att, gm, ml, w_out, h, mod, ln_g, ln_b, rows, bsz, n, alpha):
    dm = _dims()
    d = dm["d"]
    tm = _tile(n, (512, 256))
    tn = _tile(d, (512, 256, 128))
    tps = n // tm
    att_w, gm_w, ml_w = dm["att_w"], dm["gm_w"], dm["ml_w"]
    vec = pl.BlockSpec((1, d), lambda i, j: (0, 0))
    return pl.pallas_call(
        functools.partial(_outproj_body, alpha=alpha, tn=tn, sub=min(tm, 128)),
        grid=(rows // tm, d // tn),
        in_specs=[pl.BlockSpec((tm, att_w), lambda i, j: (i, 0)),
                  pl.BlockSpec((tm, gm_w), lambda i, j: (i, 0)),
                  pl.BlockSpec((tm, ml_w), lambda i, j: (i, 0)),
                  pl.BlockSpec((att_w, tn), lambda i, j: (0, j)),
                  pl.BlockSpec((gm_w, tn), lambda i, j: (att_w // gm_w, j)),
                  pl.BlockSpec((ml_w, tn), lambda i, j: ((att_w + gm_w) // ml_w, j)),
                  pl.BlockSpec((tm, tn), lambda i, j: (i, j)),
                  _mod_spec(2, tps, bsz), vec, vec, _mod_spec(3, tps, bsz), _mod_spec(4, tps, bsz)],
        out_specs=[pl.BlockSpec((tm, d), lambda i, j: (i, 0)), pl.BlockSpec((tm, d // 2), lambda i, j: (i, 0))],
        out_shape=[jax.ShapeDtypeStruct((rows, d), F32), jax.ShapeDtypeStruct((rows, d // 2), jnp.uint32)],
        compiler_params=_params("parallel", "arbitrary"),
        name="outproj_ln",
    )(att, gm, ml, w_out, w_out, w_out, h, mod, ln_g.reshape(1, d), ln_b.reshape(1, d), mod, mod)


def _expert_up_body(x_ref, wg_ref, wu_ref, o_ref, wgb_ref, wub_ref):
    @pl.when(pl.program_id(2) == 0)
    def _():
        wgb_ref[...] = wg_ref[...].astype(BF16)
        wub_ref[...] = wu_ref[...].astype(BF16)

    x = x_ref[...]
    g = jnp.dot(x, wgb_ref[...], preferred_element_type=F32)
    u = jnp.dot(x, wub_ref[...], preferred_element_type=F32)
    o_ref[...] = (g * _sigmoid(g) * u).astype(BF16)


def _expert_up(xe, wg, wu, layer):
    bsz, ne, cap, d = xe.shape
    ff = wg.shape[3]
    tf = _tile(ff, (256, 128))
    w_spec = pl.BlockSpec((None, None, d, tf), lambda e, f, b: (layer, e, 0, f))
    return pl.pallas_call(
        _expert_up_body,
        grid=(ne, ff // tf, bsz),
        in_specs=[pl.BlockSpec((None, None, cap, d), lambda e, f, b: (b, e, 0, 0)), w_spec, w_spec],
        out_specs=pl.BlockSpec((None, None, cap, tf), lambda e, f, b: (b, e, 0, f)),
        out_shape=jax.ShapeDtypeStruct((bsz, ne, cap, ff), BF16),
        scratch_shapes=[pltpu.VMEM((d, tf), BF16), pltpu.VMEM((d, tf), BF16)],
        compiler_params=_params("parallel", "parallel", "arbitrary"),
        name="expert_up",
    )(xe, wg, wu)


def _expert_down_body(x_ref, w_ref, gate_ref, o_ref, wb_ref):
    @pl.when(pl.program_id(2) == 0)
    def _():
        wb_ref[...] = w_ref[...].astype(BF16)

    o_ref[...] = (jnp.dot(x_ref[...], wb_ref[...], preferred_element_type=F32) * gate_ref[...]).astype(BF16)


def _expert_down(hid, wd, gate, layer):
    bsz, ne, cap, ff = hid.shape
    d = wd.shape[3]
    tn = _tile(d, (2048, 1024, 512))
    return pl.pallas_call(
        _expert_down_body,
        grid=(ne, d // tn, bsz),
        in_specs=[pl.BlockSpec((None, None, cap, ff), lambda e, j, b: (b, e, 0, 0)),
                  pl.BlockSpec((None, None, ff, tn), lambda e, j, b: (layer, e, 0, j)),
                  pl.BlockSpec((None, None, cap, 1), lambda e, j, b: (b, e, 0, 0))],
        out_specs=pl.BlockSpec((None, None, cap, tn), lambda e, j, b: (b, e, 0, j)),
        out_shape=jax.ShapeDtypeStruct((bsz, ne, cap, d), BF16),
        scratch_shapes=[pltpu.VMEM((ff, tn), BF16)],
        compiler_params=_params("parallel", "parallel", "arbitrary"),
        name="expert_down",
    )(hid, wd, gate)


def _resid_last_body(h_ref, moe_ref, gate_ref, g_ref, b_ref, hn_ref, *, alpha):
    hn_ref[...] = _norm_rows(alpha * h_ref[...] + gate_ref[...] * moe_ref[...]) * g_ref[...] + b_ref[...]


def _resid_body(h_ref, moe_ref, moe_ctx_ref, gate_ref, g_ref, b_ref, shift_ref, scale_ref, hn_ref, act_ref, *,
                alpha, lat_tiles):
    moe = jnp.where(pl.program_id(0) < lat_tiles, moe_ref[...], moe_ctx_ref[...])
    hn = _norm_rows(alpha * h_ref[...] + gate_ref[...] * moe) * g_ref[...] + b_ref[...]
    hn_ref[...] = hn
    act_ref[...] = (_norm_rows(hn) * (1.0 + scale_ref[...]) + shift_ref[...]).astype(BF16)


def _resid(h, moe, moe_ctx, mod, next_mod, ln_g, ln_b, bsz, n, alpha):
    d = h.shape[1]
    tm = ROW_TILE
    tps = n // tm
    lat_tiles = bsz * tps
    vec = pl.BlockSpec((1, d), lambda i: (0, 0))
    blk = pl.BlockSpec((tm, d), lambda i: (i, 0))
    if moe_ctx is None:
        return pl.pallas_call(
            functools.partial(_resid_last_body, alpha=alpha),
            grid=(lat_tiles,),
            in_specs=[blk, blk, _mod_spec(5, tps, bsz), vec, vec],
            out_specs=blk,
            out_shape=jax.ShapeDtypeStruct((lat_tiles * tm, d), F32),
            compiler_params=_params("parallel"),
            name="resid_ln",
        )(h, moe, mod, ln_g.reshape(1, d), ln_b.reshape(1, d))
    rows = h.shape[0]
    return pl.pallas_call(
        functools.partial(_resid_body, alpha=alpha, lat_tiles=lat_tiles),
        grid=(rows // tm,),
        in_specs=[blk,
                  pl.BlockSpec((tm, d), lambda i: (jnp.minimum(i, lat_tiles - 1), 0)),
                  pl.BlockSpec((tm, d), lambda i: (jnp.maximum(i - lat_tiles, 0), 0)),
                  _mod_spec(5, tps, bsz), vec, vec, _mod_spec(0, tps, bsz), _mod_spec(1, tps, bsz)],
        out_specs=[blk, blk],
        out_shape=[jax.ShapeDtypeStruct((rows, d), F32), jax.ShapeDtypeStruct((rows, d), BF16)],
        compiler_params=_params("parallel"),
        name="resid_ln",
    )(h, moe, moe_ctx, mod, ln_g.reshape(1, d), ln_b.reshape(1, d), next_mod, next_mod)


def _router_body(a_ref, w_ref, o_ref):
    lo, hi = _unpack_halves(a_ref[...])
    half = lo.shape[1]
    logits = (jnp.dot(lo, w_ref[0:half, :], preferred_element_type=F32)
              + jnp.dot(hi, w_ref[half:, :], preferred_element_type=F32))
    o_ref[...] = logits.T


def _router(actp, w_r):
    rows, half = actp.shape
    tm = _tile(rows, (512, 256))
    return pl.pallas_call(
        _router_body,
        grid=(rows // tm,),
        in_specs=[pl.BlockSpec((tm, half), lambda i: (i, 0)), pl.BlockSpec((2 * half, HEAD_DIM), lambda i: (0, 0))],
        out_specs=pl.BlockSpec((HEAD_DIM, tm), lambda i: (0, i)),
        out_shape=jax.ShapeDtypeStruct((HEAD_DIM, rows), F32),
        compiler_params=_params("parallel"),
        name="router",
    )(actp, w_r)


def _prefix_excl(mask):
    lanes = 128
    x = jnp.where(mask, 1.0, 0.0)
    r = lax.broadcasted_iota(jnp.int32, (lanes, lanes), 0)
    c = lax.broadcasted_iota(jnp.int32, (lanes, lanes), 1)
    tri = jnp.where(r <= c, 1.0, 0.0).astype(BF16)
    carry = jnp.zeros((x.shape[0], 1), F32)
    out = []
    for t in range(x.shape[1] // lanes):
        xt = x[:, t * lanes:(t + 1) * lanes]
        inc = jnp.dot(xt.astype(BF16), tri, preferred_element_type=F32)
        out.append(inc - xt + carry)
        carry = carry + inc[:, lanes - 1:lanes]
    return jnp.concatenate(out, axis=1)


def _route_select_body(lg_ref, pos_ref, idx_ref, gate_ref, *, cap, chunk):
    lg = lg_ref[...]
    ne, n = lg.shape
    ex = jnp.exp(lg - jnp.max(lg, axis=0, keepdims=True))
    aff = ex / jnp.sum(ex, axis=0, keepdims=True)
    bits = lax.bitcast_convert_type(aff, jnp.int32)

    thr = jnp.zeros((ne, 1), jnp.int32)
    for bit in range(30, -1, -1):
        cand = thr | (1 << bit)
        count = jnp.sum(jnp.where(bits >= cand, 1.0, 0.0), axis=1, keepdims=True)
        thr = jnp.where(count >= cap, cand, thr)
    above = bits > thr
    tied = bits == thr
    need = cap - jnp.sum(jnp.where(above, 1.0, 0.0), axis=1, keepdims=True)
    chosen = above | (tied & (_prefix_excl(tied) < need))
    pos = jnp.where(chosen, _prefix_excl(chosen), -1.0).astype(jnp.int32)
    pos_ref[...] = pos

    tok = lax.broadcasted_iota(jnp.int32, (chunk, n), 1).astype(F32)
    slot = lax.broadcasted_iota(jnp.int32, (chunk, n), 0)
    for e in range(ne):
        for c0 in range(0, cap, chunk):
            hit = pos[e:e + 1, :] == slot + c0
            idx_ref[e, c0:c0 + chunk, :] = jnp.sum(jnp.where(hit, tok, 0.0), axis=1, keepdims=True).astype(jnp.int32)
            gate_ref[e, c0:c0 + chunk, :] = jnp.sum(jnp.where(hit, aff[e:e + 1, :], 0.0), axis=1, keepdims=True)


def _route_select(lt, bsz, n, col_block0):
    ne = N_EXPERTS
    cap = CAPACITY_FACTOR * n // ne
    chunk = min(cap, 128)
    return pl.pallas_call(
        functools.partial(_route_select_body, cap=cap, chunk=chunk),
        grid=(bsz,),
        in_specs=[pl.BlockSpec((ne, n), lambda b: (0, col_block0 + b))],
        out_specs=[pl.BlockSpec((None, ne, n), lambda b: (b, 0, 0)),
                   pl.BlockSpec((None, ne, cap, 1), lambda b: (b, 0, 0, 0)),
                   pl.BlockSpec((None, ne, cap, 1), lambda b: (b, 0, 0, 0))],
        out_shape=[jax.ShapeDtypeStruct((bsz, ne, n), jnp.int32),
                   jax.ShapeDtypeStruct((bsz, ne, cap, 1), jnp.int32),
                   jax.ShapeDtypeStruct((bsz, ne, cap, 1), F32)],
        compiler_params=_params("parallel"),
        name="route_select",
    )(lt)


def _gather_body(idx_ref, src_ref, o_ref, buf_ref, sem):
    b, e = pl.program_id(0), pl.program_id(1)
    cap = buf_ref.shape[0]

    def row_copy(s):
        return pltpu.make_async_copy(src_ref.at[pl.ds(idx_ref[b, e, s], 1)], buf_ref.at[pl.ds(s, 1)], sem)

    def start(s, carry):
        row_copy(s).start()
        return carry

    def wait(s, carry):
        row_copy(s).wait()
        return carry

    unroll = 8 if cap % 8 == 0 else 1
    lax.fori_loop(0, cap, start, 0, unroll=unroll)
    lax.fori_loop(0, cap, wait, 0, unroll=unroll)
    lo, hi = _unpack_halves(buf_ref[...])
    half = lo.shape[1]
    o_ref[:, 0:half] = lo
    o_ref[:, half:] = hi


def _gather(idx, actp):
    bsz, ne, cap = idx.shape
    half = actp.shape[1]
    return pl.pallas_call(
        _gather_body,
        grid_spec=pltpu.PrefetchScalarGridSpec(
            num_scalar_prefetch=1,
            grid=(bsz, ne),
            in_specs=[pl.BlockSpec(memory_space=pl.ANY)],
            out_specs=pl.BlockSpec((None, None, cap, 2 * half), lambda b, e, idx_ref: (b, e, 0, 0)),
            scratch_shapes=[pltpu.VMEM((cap, half), jnp.uint32), pltpu.SemaphoreType.DMA(())]),
        out_shape=jax.ShapeDtypeStruct((bsz, ne, cap, 2 * half), BF16),
        compiler_params=_params("arbitrary", "arbitrary"),
        name="moe_gather",
    )(idx, actp)


def _combine_body(pos_ref, ye_ref, o_ref, *, chunk):
    @pl.when(pl.program_id(2) == 0)
    def _():
        o_ref[...] = jnp.zeros_like(o_ref)

    pos = pos_ref[...]
    slot = lax.broadcasted_iota(jnp.int32, (chunk, pos.shape[1]), 0)
    for c0 in range(0, ye_ref.shape[0], chunk):
        inside = (pos >= c0) & (pos < c0 + chunk)

        @pl.when(jnp.max(jnp.where(inside, 1, 0)) > 0)
        def _():
            onehot = jnp.where(pos - c0 == slot, 1.0, 0.0).astype(BF16)
            o_ref[...] += lax.dot_general(onehot, ye_ref[c0:c0 + chunk, :], (((0,), (0,)), ((), ())),
                                          preferred_element_type=F32)


def _combine(pos, ye, cap, n):
    bsz, ne = pos.shape[:2]
    d = ye.shape[3]
    tt = _tile(n, (512, 256))
    nt = n // tt
    return pl.pallas_call(
        functools.partial(_combine_body, chunk=min(cap, 256)),
        grid=(bsz, nt, ne),
        in_specs=[pl.BlockSpec((None, None, 1, tt), lambda b, t, e: (b, e, 0, t)),
                  pl.BlockSpec((None, None, cap, d), lambda b, t, e: (b, e, 0, 0))],
        out_specs=pl.BlockSpec((tt, d), lambda b, t, e: (b * nt + t, 0)),
        out_shape=jax.ShapeDtypeStruct((bsz * n, d), F32),
        compiler_params=_params("parallel", "parallel", "arbitrary"),
        name="moe_combine",
    )(pos, ye)


def kernel(x, c, ctx, c_ctx, w_mod, b_mod, w_in, q_gain, k_gain, gm_ln_g, gm_ln_b, w_spatial, b_spatial, b_gates,
           ml_gain, w_out, ln1_g, ln1_b, w_router, w_e_gate, w_e_up, w_e_down, ln2_g, ln2_b):
    dm = _dims()
    bsz, n, d = x.shape
    ctx_len = ctx.shape[1]
    depth = w_mod.shape[0]
    alpha = (2 * depth) ** 0.25
    lat_rows = bsz * n
    main_w = dm["main_w"]
    n_gates = 4 * MLSTM_HEADS
    cap_l = CAPACITY_FACTOR * n // N_EXPERTS
    cap_c = CAPACITY_FACTOR * ctx_len // N_EXPERTS
    sample = jnp.arange(bsz, dtype=jnp.int32)[:, None, None]

    cc = jnp.concatenate([c, c_ctx[None], jnp.zeros((MOD_ROWS - bsz - 1, d), F32)], axis=0)
    mods = _mod_vectors(cc, w_mod, b_mod).reshape(depth, MOD_ROWS, N_MOD, 1, d)
    cos, sin = _rope_tables(n, ROW_TILE)

    h = jnp.concatenate([x.reshape(lat_rows, d), ctx.reshape(bsz * ctx_len, d)], axis=0)
    act = _ln_mod(h, mods[0], bsz, n)
    for l in range(depth):
        last = l == depth - 1
        rows = lat_rows if last else h.shape[0]
        mod = mods[l]

        p = _in_proj(act, w_in, l, main_w)
        w_g = jnp.pad(w_in[l][:, main_w:].astype(BF16), ((0, 0), (0, HEAD_DIM - n_gates)))
        b_g = jnp.pad(b_gates[l], (0, HEAD_DIM - n_gates)).reshape(1, HEAD_DIM)
        gates = _matmul(act, w_g, b_g, "gate_proj")[:, :n_gates]
        gates_seq = jnp.concatenate([gates[lat_rows:].reshape(bsz, ctx_len, n_gates),
                                     gates[:lat_rows].reshape(bsz, n, n_gates)], axis=1)
        gates_rows = jnp.swapaxes(gates_seq, 1, 2)

        qb, kb, vb = _qkv_prep(p, cos, sin, q_gain[l], k_gain[l], bsz, n, ctx_len)
        att = _attention(qb, kb, vb, bsz, n, ctx_len, not last)
        gm = _gmlp(p, gm_ln_g[l], gm_ln_b[l], w_spatial[l], b_spatial[l])
        hf, hb = _mlstm(p, gates_rows, bsz, n, ctx_len)
        ml = _mlstm_out(hf, hb, p, ml_gain[l], rows)
        h, actp = _outproj(att, gm, ml, w_out[l].astype(BF16), h, mod, ln1_g[l], ln1_b[l], rows, bsz, n, alpha)

        w_r = jnp.pad(w_router[l].astype(BF16), ((0, 0), (0, HEAD_DIM - N_EXPERTS)))
        lt = _router(actp, w_r)
        pos_l, idx_l, gate = _route_select(lt, bsz, n, 0)
        idx = idx_l[..., 0] + n * sample
        if not last:
            pos_c, idx_c, gate_c = _route_select(lt, bsz, ctx_len, lat_rows // ctx_len)
            idx = jnp.concatenate([idx, idx_c[..., 0] + lat_rows + ctx_len * sample], axis=2)
            gate = jnp.concatenate([gate, gate_c], axis=2)
        xe = _gather(idx, actp)
        hid = _expert_up(xe, w_e_gate, w_e_up, l)
        ye = _expert_down(hid, w_e_down, gate, l)
        moe = _combine(pos_l.reshape(bsz, N_EXPERTS, 1, n), ye, cap_l, n)
        if not last:
            moe_ctx = _combine(pos_c.reshape(bsz, N_EXPERTS, 1, ctx_len), ye[:, :, cap_l:], cap_c, ctx_len)
            h, act = _resid(h, moe, moe_ctx, mod, mods[l + 1], ln2_g[l], ln2_b[l], bsz, n, alpha)
        else:
            h = _resid(h, moe, None, mod, None, ln2_g[l], ln2_b[l], bsz, n, alpha)
    return h.reshape(bsz, n, d)
```

```python
import functools

import jax
import jax.numpy as jnp
from jax import lax
from jax.experimental import pallas as pl
from jax.experimental.pallas import tpu as pltpu

D_MODEL = 4096
BATCH = 4
SEQ = 4096
DEPTH = 2
CTX_LEN = 256
GRID_W = 64
HEAD_DIM = 128
ATT_Q_HEADS = 16
ATT_KV_HEADS = 4
ROPE_THETA = 10000.0
GMLP_CHUNK = 128
GMLP_GROUPS = 8
MLSTM_HEADS = 4
MLSTM_CHUNK = 128
N_EXPERTS = 16
CAPACITY_FACTOR = 2
N_MOD = 6
EPS = 1e-6
LOG2_E = 1.4426950408889634

F32 = jnp.float32
BF16 = jnp.bfloat16

ROW_TILE = 256
MOD_ROWS = 8
COMBINE_WINDOW = 64
VMEM_LIMIT = 56 * 1024 * 1024


def _dims():
    d = D_MODEL
    att_w = ATT_Q_HEADS * HEAD_DIM
    kv_w = ATT_KV_HEADS * HEAD_DIM
    gm_w = d // 4
    ml_w = d // 4
    offs, start = {}, 0
    for name, width in (("att_q", att_w), ("att_k", kv_w), ("att_v", kv_w), ("gm_u", gm_w), ("gm_v", gm_w),
                        ("ml_q", ml_w), ("ml_k", ml_w), ("ml_v", ml_w), ("ml_o", ml_w),
                        ("ml_gates", 4 * MLSTM_HEADS)):
        offs[name] = start
        start += width
    return dict(d=d, att_w=att_w, kv_w=kv_w, gm_w=gm_w, ml_w=ml_w, offs=offs, proj_w=start,
                main_w=offs["ml_gates"], ml_dh=ml_w // MLSTM_HEADS, ff=d // 4)


def _tile(n, prefs):
    for t in prefs:
        if n % t == 0:
            return t
    return n


def _params(*sem):
    return pltpu.CompilerParams(dimension_semantics=sem, vmem_limit_bytes=VMEM_LIMIT)


def _sigmoid(x):
    return 1.0 / (1.0 + jnp.exp(-x))


def _norm_rows(x):
    mu = jnp.mean(x, axis=-1, keepdims=True)
    xc = x - mu
    var = jnp.mean(xc * xc, axis=-1, keepdims=True)
    return xc * lax.rsqrt(var + EPS)


def _gelu_tanh(x):
    return 0.5 * x * (1.0 + jnp.tanh(0.7978845608028654 * (x + 0.044715 * (x * x * x))))


def _mod_body(x_ref, w_ref, b_ref, o_ref):
    x = x_ref[...]
    xs = (x * _sigmoid(x)).astype(BF16)
    o_ref[...] = jnp.dot(xs, w_ref[...].astype(BF16), preferred_element_type=F32) + b_ref[...]


def _mod_vectors(cc, w_mod, b_mod):
    depth, d, n = w_mod.shape
    tn = _tile(n, (512, 256, 128))
    return pl.pallas_call(
        _mod_body,
        grid=(depth, n // tn),
        in_specs=[pl.BlockSpec((MOD_ROWS, d), lambda l, j: (0, 0)),
                  pl.BlockSpec((None, d, tn), lambda l, j: (l, 0, j)),
                  pl.BlockSpec((None, 1, tn), lambda l, j: (l, 0, j))],
        out_specs=pl.BlockSpec((None, MOD_ROWS, tn), lambda l, j: (l, 0, j)),
        out_shape=jax.ShapeDtypeStruct((depth, MOD_ROWS, n), F32),
        compiler_params=_params("parallel", "parallel"),
        name="mod_vectors",
    )(cc, w_mod, b_mod.reshape(depth, 1, n))


def _mod_spec(which, tiles_per_sample, bsz):
    d = D_MODEL
    return pl.BlockSpec((None, None, 1, d),
                        lambda i, *_: (jnp.minimum(i // tiles_per_sample, bsz), which, 0, 0))


def _ln_mod_body(h_ref, shift_ref, scale_ref, o_ref):
    y = _norm_rows(h_ref[...]) * (1.0 + scale_ref[...]) + shift_ref[...]
    o_ref[...] = y.astype(BF16)


def _ln_mod(h, mod, bsz, n):
    rows, d = h.shape
    tm = ROW_TILE
    tps = n // tm
    return pl.pallas_call(
        _ln_mod_body,
        grid=(rows // tm,),
        in_specs=[pl.BlockSpec((tm, d), lambda i: (i, 0)), _mod_spec(0, tps, bsz), _mod_spec(1, tps, bsz)],
        out_specs=pl.BlockSpec((tm, d), lambda i: (i, 0)),
        out_shape=jax.ShapeDtypeStruct((rows, d), BF16),
        compiler_params=_params("parallel"),
        name="ln_mod",
    )(h, mod, mod)


def _mm_body(a_ref, w_ref, b_ref, o_ref):
    o_ref[...] = jnp.dot(a_ref[...], w_ref[...], preferred_element_type=F32) + b_ref[...]


def _matmul(a, w, bias, name):
    m, k = a.shape
    n = w.shape[1]
    tm = _tile(m, (1024, 512, 256))
    tn = _tile(n, (512, 256, 128))
    return pl.pallas_call(
        _mm_body,
        grid=(m // tm, n // tn),
        in_specs=[pl.BlockSpec((tm, k), lambda i, j: (i, 0)),
                  pl.BlockSpec((k, tn), lambda i, j: (0, j)),
                  pl.BlockSpec((1, tn), lambda i, j: (0, j))],
        out_specs=pl.BlockSpec((tm, tn), lambda i, j: (i, j)),
        out_shape=jax.ShapeDtypeStruct((m, n), F32),
        compiler_params=_params("parallel", "parallel"),
        name=name,
    )(a, w, bias)


def _in_proj_body(a_ref, w_ref, o_ref, wb_ref):
    @pl.when(pl.program_id(1) == 0)
    def _():
        wb_ref[...] = w_ref[...].astype(BF16)

    o_ref[...] = jnp.dot(a_ref[...], wb_ref[...], preferred_element_type=F32)


def _in_proj(a, w_in, layer, n_out):
    m, k = a.shape
    tm = _tile(m, (1024, 512, 256))
    tn = _tile(n_out, (512, 256, 128))
    return pl.pallas_call(
        _in_proj_body,
        grid=(n_out // tn, m // tm),
        in_specs=[pl.BlockSpec((tm, k), lambda j, i: (i, 0)),
                  pl.BlockSpec((None, k, tn), lambda j, i: (layer, 0, j))],
        out_specs=pl.BlockSpec((tm, tn), lambda j, i: (i, j)),
        out_shape=jax.ShapeDtypeStruct((m, n_out), F32),
        scratch_shapes=[pltpu.VMEM((k, tn), BF16)],
        compiler_params=_params("parallel", "arbitrary"),
        name="in_proj",
    )(a, w_in)


def _rope_tables(n, tm):
    rows = n // GRID_W
    row = jnp.repeat(jnp.arange(rows), GRID_W).astype(F32)
    col = jnp.tile(jnp.arange(GRID_W), rows).astype(F32)
    n_freq = HEAD_DIM // 4
    inv_freq = ROPE_THETA ** (-jnp.arange(n_freq, dtype=F32) / n_freq)
    ang_r, ang_c = row[:, None] * inv_freq, col[:, None] * inv_freq
    cos = jnp.concatenate([jnp.cos(ang_r), jnp.cos(ang_r), jnp.cos(ang_c), jnp.cos(ang_c)], axis=1)
    sin = jnp.concatenate([-jnp.sin(ang_r), jnp.sin(ang_r), -jnp.sin(ang_c), jnp.sin(ang_c)], axis=1)
    cos = jnp.concatenate([cos, jnp.ones((tm, HEAD_DIM), F32)], axis=0)
    sin = jnp.concatenate([sin, jnp.zeros((tm, HEAD_DIM), F32)], axis=0)
    return cos, sin


def _qkv_prep_body(q_ref, k_ref, v_ref, cos_ref, sin_ref, qg_ref, kg_ref, qo_ref, ko_ref, vo_ref):
    cos, sin = cos_ref[...], sin_ref[...]
    lane = lax.broadcasted_iota(jnp.int32, cos.shape, 1)
    first_half = (lane % (HEAD_DIM // 2)) < (HEAD_DIM // 4)

    def norm_rope(x, gain):
        y = x * lax.rsqrt(jnp.mean(x * x, axis=-1, keepdims=True) + EPS) * gain
        partner = jnp.where(first_half, pltpu.roll(y, HEAD_DIM - HEAD_DIM // 4, 1), pltpu.roll(y, HEAD_DIM // 4, 1))
        return y * cos + partner * sin

    qg = qg_ref[...] * (HEAD_DIM ** -0.5 * LOG2_E)
    for h in range(ATT_Q_HEADS):
        sl = slice(h * HEAD_DIM, (h + 1) * HEAD_DIM)
        qo_ref[:, sl] = norm_rope(q_ref[:, sl], qg).astype(BF16)
    for h in range(ATT_KV_HEADS):
        sl = slice(h * HEAD_DIM, (h + 1) * HEAD_DIM)
        ko_ref[:, sl] = norm_rope(k_ref[:, sl], kg_ref[...]).astype(BF16)
    vo_ref[...] = v_ref[...].astype(BF16)


def _qkv_prep(p, cos, sin, q_gain, k_gain, bsz, n, ctx_len):
    dm = _dims()
    rows = p.shape[0]
    tm = ROW_TILE
    tps, tpc = n // tm, ctx_len // tm
    nlat = bsz * tps
    att_w, kv_w = dm["att_w"], dm["kv_w"]
    ntot = n + ctx_len

    def table_map(i):
        return (jnp.where(i < nlat, i % tps, tps), 0)

    def kv_map(i):
        j = i - nlat
        return (jnp.where(i < nlat, i // tps, j // tpc), jnp.where(i < nlat, tpc + i % tps, j % tpc), 0)

    return pl.pallas_call(
        _qkv_prep_body,
        grid=(rows // tm,),
        in_specs=[pl.BlockSpec((tm, att_w), lambda i: (i, dm["offs"]["att_q"] // att_w)),
                  pl.BlockSpec((tm, kv_w), lambda i: (i, dm["offs"]["att_k"] // kv_w)),
                  pl.BlockSpec((tm, kv_w), lambda i: (i, dm["offs"]["att_v"] // kv_w)),
                  pl.BlockSpec((tm, HEAD_DIM), table_map),
                  pl.BlockSpec((tm, HEAD_DIM), table_map),
                  pl.BlockSpec((1, HEAD_DIM), lambda i: (0, 0)),
                  pl.BlockSpec((1, HEAD_DIM), lambda i: (0, 0))],
        out_specs=[pl.BlockSpec((tm, att_w), lambda i: (i, 0)),
                   pl.BlockSpec((None, tm, kv_w), kv_map),
                   pl.BlockSpec((None, tm, kv_w), kv_map)],
        out_shape=[jax.ShapeDtypeStruct((rows, att_w), BF16),
                   jax.ShapeDtypeStruct((bsz, ntot, kv_w), BF16),
                   jax.ShapeDtypeStruct((bsz, ntot, kv_w), BF16)],
        compiler_params=_params("parallel"),
        name="qkv_prep",
    )(p, p, p, cos, sin, q_gain.reshape(1, HEAD_DIM), k_gain.reshape(1, HEAD_DIM))


def _attn_heads(q_ref, k, v, o_ref):
    group = ATT_Q_HEADS // ATT_KV_HEADS
    for h in range(group):
        sl = slice(h * HEAD_DIM, (h + 1) * HEAD_DIM)
        s = lax.dot_general(q_ref[:, sl], k, (((1,), (1,)), ((), ())), preferred_element_type=F32)
        e = jnp.exp2(s - jnp.max(s, axis=-1, keepdims=True))
        o = jnp.dot(e.astype(BF16), v, preferred_element_type=F32)
        o_ref[:, sl] = (o / jnp.sum(e, axis=-1, keepdims=True)).astype(BF16)


def _attn_body(q_ref, k_ref, v_ref, o_ref, *, lat_q_tiles, ctx_len):
    qi = pl.program_id(2)

    @pl.when(qi < lat_q_tiles)
    def _():
        _attn_heads(q_ref, k_ref[...], v_ref[...], o_ref)

    @pl.when(qi >= lat_q_tiles)
    def _():
        _attn_heads(q_ref, k_ref[0:ctx_len, :], v_ref[0:ctx_len, :], o_ref)


def _attention(qb, kb, vb, bsz, n, ctx_len, with_ctx):
    rows_all = qb.shape[0]
    tq = ROW_TILE
    tps, tpc = n // tq, ctx_len // tq
    nlat = bsz * tps
    group_w = (ATT_Q_HEADS // ATT_KV_HEADS) * HEAD_DIM
    ntot = n + ctx_len
    q_tiles = tps + (tpc if with_ctx else 0)
    rows = rows_all if with_ctx else bsz * n

    def q_map(b, g, qi):
        return (jnp.where(qi < tps, b * tps + qi, nlat + b * tpc + (qi - tps)), g)

    return pl.pallas_call(
        functools.partial(_attn_body, lat_q_tiles=tps, ctx_len=ctx_len),
        grid=(bsz, ATT_KV_HEADS, q_tiles),
        in_specs=[pl.BlockSpec((tq, group_w), q_map),
                  pl.BlockSpec((None, ntot, HEAD_DIM), lambda b, g, qi: (b, 0, g)),
                  pl.BlockSpec((None, ntot, HEAD_DIM), lambda b, g, qi: (b, 0, g))],
        out_specs=pl.BlockSpec((tq, group_w), q_map),
        out_shape=jax.ShapeDtypeStruct((rows, ATT_Q_HEADS * HEAD_DIM), BF16),
        compiler_params=_params("parallel", "parallel", "parallel"),
        name="attention",
    )(qb, kb, vb)


def _gmlp_body(u_ref, v_ref, g_ref, b_ref, ws_ref, bs_ref, o_ref):
    u = _gelu_tanh(u_ref[...])
    v = (_norm_rows(_gelu_tanh(v_ref[...])) * g_ref[...] + b_ref[...]).astype(BF16)
    gd = v.shape[1] // GMLP_GROUPS
    for c in range(v.shape[0] // GMLP_CHUNK):
        rs = slice(c * GMLP_CHUNK, (c + 1) * GMLP_CHUNK)
        for g in range(GMLP_GROUPS):
            cs = slice(g * gd, (g + 1) * gd)
            sv = jnp.dot(ws_ref[g].astype(BF16), v[rs, cs], preferred_element_type=F32) + bs_ref[:, g:g + 1]
            o_ref[rs, cs] = (u[rs, cs] * sv).astype(BF16)


def _gmlp(p, ln_g, ln_b, w_s, b_s):
    dm = _dims()
    rows = p.shape[0]
    tm = ROW_TILE
    gm_w = dm["gm_w"]
    return pl.pallas_call(
        _gmlp_body,
        grid=(rows // tm,),
        in_specs=[pl.BlockSpec((tm, gm_w), lambda i: (i, dm["offs"]["gm_u"] // gm_w)),
                  pl.BlockSpec((tm, gm_w), lambda i: (i, dm["offs"]["gm_v"] // gm_w)),
                  pl.BlockSpec((1, gm_w), lambda i: (0, 0)),
                  pl.BlockSpec((1, gm_w), lambda i: (0, 0)),
                  pl.BlockSpec((GMLP_GROUPS, GMLP_CHUNK, GMLP_CHUNK), lambda i: (0, 0, 0)),
                  pl.BlockSpec((GMLP_CHUNK, GMLP_GROUPS), lambda i: (0, 0))],
        out_specs=pl.BlockSpec((tm, gm_w), lambda i: (i, 0)),
        out_shape=jax.ShapeDtypeStruct((rows, gm_w), BF16),
        compiler_params=_params("parallel"),
        name="gmlp",
    )(p, p, ln_g.reshape(1, gm_w), ln_b.reshape(1, gm_w), w_s, b_s.T)


def _mlstm_chain_step(q_ref, k_ref, v_ref, g_ref, o_ref, c_ref, n_ref, m_ref, direction, head, cols, masks):
    seen, seen_t, eye = masks
    heads = MLSTM_HEADS

    def to_col(x_row):
        return jnp.sum(jnp.where(eye, x_row, 0.0), axis=1, keepdims=True)

    gi = 2 * direction * heads + head
    i_row = g_ref[gi:gi + 1, :]
    f_raw = g_ref[gi + heads:gi + heads + 1, :]
    lf_row = jnp.minimum(f_raw, 0.0) - jnp.log1p(jnp.exp(-jnp.abs(f_raw)))
    lf_col, i_col = to_col(lf_row), to_col(i_row)
    bcum_col = jnp.sum(jnp.where(seen, lf_row, 0.0), axis=1, keepdims=True)
    bcum_row = jnp.sum(jnp.where(seen_t, lf_col, 0.0), axis=0, keepdims=True)
    b_last = jnp.sum(lf_row, axis=1, keepdims=True)

    m_old = m_ref[direction, head]
    g_row = b_last - bcum_row + i_row
    g_col = b_last - bcum_col + i_col
    m_new = jnp.maximum(b_last + m_old, jnp.max(g_row, axis=1, keepdims=True))
    w_prev = jnp.exp(b_last + m_old - m_new)

    q = q_ref[:, cols]
    k = k_ref[:, cols] * ((cols.stop - cols.start) ** -0.5)
    vb = v_ref[:, cols].astype(BF16)
    qb, kb = q.astype(BF16), k.astype(BF16)
    c_old, n_old = c_ref[direction, head], n_ref[direction, head]

    a_col = bcum_col + m_old
    dlog = jnp.where(seen, bcum_col - bcum_row + i_row, -jnp.inf)
    mt = jnp.maximum(a_col, jnp.max(dlog, axis=1, keepdims=True))
    s = lax.dot_general(qb, kb, (((1,), (1,)), ((), ())), preferred_element_type=F32) * jnp.exp(dlog - mt)
    wa = jnp.exp(a_col - mt)
    num = (wa * jnp.dot(qb, c_old.astype(BF16), preferred_element_type=F32)
           + jnp.dot(s.astype(BF16), vb, preferred_element_type=F32))
    den = wa * jnp.sum(q * n_old, axis=1, keepdims=True) + jnp.sum(s, axis=1, keepdims=True)
    o_ref[:, cols] = num / jnp.maximum(jnp.abs(den), jnp.exp(-mt))

    kw = k * jnp.exp(g_col - m_new)
    c_ref[direction, head] = w_prev * c_old + lax.dot_general(kw.astype(BF16), vb, (((0,), (0,)), ((), ())),
                                                              preferred_element_type=F32)
    n_ref[direction, head] = w_prev * n_old + jnp.sum(kw, axis=0, keepdims=True)
    m_ref[direction, head] = m_new


def _mlstm_body(qf_ref, kf_ref, vf_ref, gf_ref, qb_ref, kb_ref, vb_ref, gb_ref, of_ref, ob_ref,
                c_ref, n_ref, m_ref):
    @pl.when(pl.program_id(1) == 0)
    def _():
        c_ref[...] = jnp.zeros_like(c_ref)
        n_ref[...] = jnp.zeros_like(n_ref)
        m_ref[...] = jnp.zeros_like(m_ref)

    chunk = qf_ref.shape[0]
    dh = qf_ref.shape[1] // MLSTM_HEADS
    row = lax.broadcasted_iota(jnp.int32, (chunk, chunk), 0)
    col = lax.broadcasted_iota(jnp.int32, (chunk, chunk), 1)
    eye = row == col
    fwd_masks = (col <= row, row <= col, eye)
    bwd_masks = (col >= row, row >= col, eye)
    for head in range(MLSTM_HEADS):
        cols = slice(head * dh, (head + 1) * dh)
        _mlstm_chain_step(qf_ref, kf_ref, vf_ref, gf_ref, of_ref, c_ref, n_ref, m_ref, 0, head, cols, fwd_masks)
        _mlstm_chain_step(qb_ref, kb_ref, vb_ref, gb_ref, ob_ref, c_ref, n_ref, m_ref, 1, head, cols, bwd_masks)


def _mlstm(p, gates_rows, bsz, n, ctx_len):
    dm = _dims()
    rows = p.shape[0]
    lc = MLSTM_CHUNK
    dh, ml_w = dm["ml_dh"], dm["ml_w"]
    heads = MLSTM_HEADS
    ncc, nlc = ctx_len // lc, n // lc
    steps = ncc + nlc
    lat_chunks = bsz * nlc

    def seq_chunk(dr, j):
        return j if dr == 0 else jnp.where(j < ncc, ncc - 1 - j, ncc + nlc - 1 - (j - ncc))

    def row_chunk(b, dr, j):
        c = seq_chunk(dr, j)
        return jnp.where(c < ncc, lat_chunks + b * ncc + c, b * nlc + (c - ncc))

    def proj_spec(name, dr):
        return pl.BlockSpec((lc, ml_w), lambda b, j: (row_chunk(b, dr, j), dm["offs"][name] // ml_w))

    def dir_specs(dr):
        return [proj_spec("ml_q", dr), proj_spec("ml_k", dr), proj_spec("ml_v", dr),
                pl.BlockSpec((None, 4 * heads, lc), lambda b, j: (b, 0, seq_chunk(dr, j)))]

    return pl.pallas_call(
        _mlstm_body,
        grid=(bsz, steps),
        in_specs=dir_specs(0) + dir_specs(1),
        out_specs=[pl.BlockSpec((lc, ml_w), lambda b, j: (row_chunk(b, 0, j), 0)),
                   pl.BlockSpec((lc, ml_w), lambda b, j: (row_chunk(b, 1, j), 0))],
        out_shape=[jax.ShapeDtypeStruct((rows, ml_w), F32), jax.ShapeDtypeStruct((rows, ml_w), F32)],
        scratch_shapes=[pltpu.VMEM((2, heads, dh, dh), F32), pltpu.VMEM((2, heads, 1, dh), F32),
                        pltpu.VMEM((2, heads, 1, 1), F32)],
        compiler_params=_params("parallel", "arbitrary"),
        name="mlstm_scan",
    )(p, p, p, gates_rows, p, p, p, gates_rows)


def _mlstm_out_body(hf_ref, hb_ref, o_ref, gain_ref, out_ref):
    dh = out_ref.shape[1] // MLSTM_HEADS
    for h in range(MLSTM_HEADS):
        sl = slice(h * dh, (h + 1) * dh)
        hs = hf_ref[:, sl] + hb_ref[:, sl]
        hn = hs * lax.rsqrt(jnp.mean(hs * hs, axis=-1, keepdims=True) + EPS) * gain_ref[:, sl]
        out_ref[:, sl] = (_sigmoid(o_ref[:, sl]) * hn).astype(BF16)


def _mlstm_out(hf, hb, p, gain, rows):
    dm = _dims()
    tm = ROW_TILE
    ml_w = dm["ml_w"]
    blk = pl.BlockSpec((tm, ml_w), lambda i: (i, 0))
    return pl.pallas_call(
        _mlstm_out_body,
        grid=(rows // tm,),
        in_specs=[blk, blk,
                  pl.BlockSpec((tm, ml_w), lambda i: (i, dm["offs"]["ml_o"] // ml_w)),
                  pl.BlockSpec((1, ml_w), lambda i: (0, 0))],
        out_specs=blk,
        out_shape=jax.ShapeDtypeStruct((rows, ml_w), BF16),
        compiler_params=_params("parallel"),
        name="mlstm_out",
    )(hf, hb, p, gain.reshape(1, ml_w))


def _pack_halves(y):
    half = y.shape[1] // 2
    lo = lax.bitcast_convert_type(y[:, :half].astype(BF16).astype(F32), jnp.uint32) >> 16
    hi = lax.bitcast_convert_type(y[:, half:].astype(BF16).astype(F32), jnp.uint32) & jnp.uint32(0xFFFF0000)
    return lo | hi


def _unpack_halves(w):
    lo = lax.bitcast_convert_type(w << 16, F32).astype(BF16)
    hi = lax.bitcast_convert_type(w & jnp.uint32(0xFFFF0000), F32).astype(BF16)
    return lo, hi


def _outproj_body(att_ref, gm_ref, ml_ref, w1_ref, w2_ref, w3_ref, h_ref, gate_ref, g_ref, b_ref,
                  shift_ref, scale_ref, hn_ref, act_ref, *, alpha, tn, sub):
    j = pl.program_id(1)
    cols = pl.ds(pl.multiple_of(j * tn, tn), tn)
    mix = (jnp.dot(att_ref[...], w1_ref[...], preferred_element_type=F32)
           + jnp.dot(gm_ref[...], w2_ref[...], preferred_element_type=F32)
           + jnp.dot(ml_ref[...], w3_ref[...], preferred_element_type=F32))
    hn_ref[:, cols] = alpha * h_ref[...] + gate_ref[:, cols] * mix

    @pl.when(j == pl.num_programs(1) - 1)
    def _():
        for r in range(hn_ref.shape[0] // sub):
            rs = slice(r * sub, (r + 1) * sub)
            hn = _norm_rows(hn_ref[rs, :]) * g_ref[...] + b_ref[...]
            hn_ref[rs, :] = hn
            act_ref[rs, :] = _pack_halves(_norm_rows(hn) * (1.0 + scale_ref[...]) + shift_ref[...])


def _outproj(att, gm, ml, w_out, h, mod, ln_g, ln_b, rows, bsz, n, alpha):
    dm = _dims()
    d = dm["d"]
    tm = _tile(n, (512, 256))
    tn = _tile(d, (512, 256, 128))
    tps = n // tm
    att_w, gm_w, ml_w = dm["att_w"], dm["gm_w"], dm["ml_w"]
    vec = pl.BlockSpec((1, d), lambda i, j: (0, 0))
    return pl.pallas_call(
        functools.partial(_outproj_body, alpha=alpha, tn=tn, sub=min(tm, 128)),
        grid=(rows // tm, d // tn),
        in_specs=[pl.BlockSpec((tm, att_w), lambda i, j: (i, 0)),
                  pl.BlockSpec((tm, gm_w), lambda i, j: (i, 0)),
                  pl.BlockSpec((tm, ml_w), lambda i, j: (i, 0)),
                  pl.BlockSpec((att_w, tn), lambda i, j: (0, j)),
                  pl.BlockSpec((gm_w, tn), lambda i, j: (att_w // gm_w, j)),
                  pl.BlockSpec((ml_w, tn), lambda i, j: ((att_w + gm_w) // ml_w, j)),
                  pl.BlockSpec((tm, tn), lambda i, j: (i, j)),
                  _mod_spec(2, tps, bsz), vec, vec, _mod_spec(3, tps, bsz), _mod_spec(4, tps, bsz)],
        out_specs=[pl.BlockSpec((tm, d), lambda i, j: (i, 0)), pl.BlockSpec((tm, d // 2), lambda i, j: (i, 0))],
        out_shape=[jax.ShapeDtypeStruct((rows, d), F32), jax.ShapeDtypeStruct((rows, d // 2), jnp.uint32)],
        compiler_params=_params("parallel", "arbitrary"),
        name="outproj_ln",
    )(att, gm, ml, w_out, w_out, w_out, h, mod, ln_g.reshape(1, d), ln_b.reshape(1, d), mod, mod)


def _expert_up_body(x_ref, wg_ref, wu_ref, o_ref, wgb_ref, wub_ref):
    @pl.when(pl.program_id(2) == 0)
    def _():
        wgb_ref[...] = wg_ref[...].astype(BF16)
        wub_ref[...] = wu_ref[...].astype(BF16)

    x = x_ref[...]
    g = jnp.dot(x, wgb_ref[...], preferred_element_type=F32)
    u = jnp.dot(x, wub_ref[...], preferred_element_type=F32)
    o_ref[...] = (g * _sigmoid(g) * u).astype(BF16)


def _expert_up(xe, wg, wu, layer):
    bsz, ne, cap, d = xe.shape
    ff = wg.shape[3]
    tf = _tile(ff, (256, 128))
    w_spec = pl.BlockSpec((None, None, d, tf), lambda e, f, b: (layer, e, 0, f))
    return pl.pallas_call(
        _expert_up_body,
        grid=(ne, ff // tf, bsz),
        in_specs=[pl.BlockSpec((None, None, cap, d), lambda e, f, b: (b, e, 0, 0)), w_spec, w_spec],
        out_specs=pl.BlockSpec((None, None, cap, tf), lambda e, f, b: (b, e, 0, f)),
        out_shape=jax.ShapeDtypeStruct((bsz, ne, cap, ff), BF16),
        scratch_shapes=[pltpu.VMEM((d, tf), BF16), pltpu.VMEM((d, tf), BF16)],
        compiler_params=_params("parallel", "parallel", "arbitrary"),
        name="expert_up",
    )(xe, wg, wu)


def _expert_down_body(x_ref, w_ref, gate_ref, o_ref, wb_ref):
    @pl.when(pl.program_id(2) == 0)
    def _():
        wb_ref[...] = w_ref[...].astype(BF16)

    o_ref[...] = (jnp.dot(x_ref[...], wb_ref[...], preferred_element_type=F32) * gate_ref[...]).astype(BF16)


def _expert_down(hid, wd, gate, layer):
    bsz, ne, cap, ff = hid.shape
    d = wd.shape[3]
    tn = _tile(d, (2048, 1024, 512))
    return pl.pallas_call(
        _expert_down_body,
        grid=(ne, d // tn, bsz),
        in_specs=[pl.BlockSpec((None, None, cap, ff), lambda e, j, b: (b, e, 0, 0)),
                  pl.BlockSpec((None, None, ff, tn), lambda e, j, b: (layer, e, 0, j)),
                  pl.BlockSpec((None, None, cap, 1), lambda e, j, b: (b, e, 0, 0))],
        out_specs=pl.BlockSpec((None, None, cap, tn), lambda e, j, b: (b, e, 0, j)),
        out_shape=jax.ShapeDtypeStruct((bsz, ne, cap, d), BF16),
        scratch_shapes=[pltpu.VMEM((ff, tn), BF16)],
        compiler_params=_params("parallel", "parallel", "arbitrary"),
        name="expert_down",
    )(hid, wd, gate)


def _router_body(a_ref, w_ref, o_ref):
    lo, hi = _unpack_halves(a_ref[...])
    half = lo.shape[1]
    logits = (jnp.dot(lo, w_ref[0:half, :], preferred_element_type=F32)
              + jnp.dot(hi, w_ref[half:, :], preferred_element_type=F32))
    o_ref[...] = logits.T


def _router(actp, w_r):
    rows, half = actp.shape
    tm = _tile(rows, (512, 256))
    return pl.pallas_call(
        _router_body,
        grid=(rows // tm,),
        in_specs=[pl.BlockSpec((tm, half), lambda i: (i, 0)), pl.BlockSpec((2 * half, HEAD_DIM), lambda i: (0, 0))],
        out_specs=pl.BlockSpec((HEAD_DIM, tm), lambda i: (0, i)),
        out_shape=jax.ShapeDtypeStruct((HEAD_DIM, rows), F32),
        compiler_params=_params("parallel"),
        name="router",
    )(actp, w_r)


def _prefix_excl(mask):
    lanes = 128
    x = jnp.where(mask, 1.0, 0.0)
    r = lax.broadcasted_iota(jnp.int32, (lanes, lanes), 0)
    c = lax.broadcasted_iota(jnp.int32, (lanes, lanes), 1)
    tri = jnp.where(r <= c, 1.0, 0.0).astype(BF16)
    carry = jnp.zeros((x.shape[0], 1), F32)
    out = []
    for t in range(x.shape[1] // lanes):
        xt = x[:, t * lanes:(t + 1) * lanes]
        inc = jnp.dot(xt.astype(BF16), tri, preferred_element_type=F32)
        out.append(inc - xt + carry)
        carry = carry + inc[:, lanes - 1:lanes]
    return jnp.concatenate(out, axis=1)


def _route_select_body(lg_ref, pos_ref, idx_ref, gate_ref, ts_ref, *, cap, chunk, tile):
    lg = lg_ref[...]
    ne, n = lg.shape
    ex = jnp.exp(lg - jnp.max(lg, axis=0, keepdims=True))
    aff = ex / jnp.sum(ex, axis=0, keepdims=True)
    bits = lax.bitcast_convert_type(aff, jnp.int32)

    thr = jnp.zeros((ne, 1), jnp.int32)
    for bit in range(30, -1, -1):
        cand = thr | (1 << bit)
        count = jnp.sum(jnp.where(bits >= cand, 1.0, 0.0), axis=1, keepdims=True)
        thr = jnp.where(count >= cap, cand, thr)
    above = bits > thr
    tied = bits == thr
    need = cap - jnp.sum(jnp.where(above, 1.0, 0.0), axis=1, keepdims=True)
    chosen = above | (tied & (_prefix_excl(tied) < need))
    before = _prefix_excl(chosen)
    pos = jnp.where(chosen, before, -1.0).astype(jnp.int32)
    pos_ref[...] = pos
    for t in range(n // tile):
        ts_ref[:, t:t + 1] = before[:, t * tile:t * tile + 1].astype(jnp.int32)

    tok = lax.broadcasted_iota(jnp.int32, (chunk, n), 1).astype(F32)
    slot = lax.broadcasted_iota(jnp.int32, (chunk, n), 0)
    for e in range(ne):
        for c0 in range(0, cap, chunk):
            hit = pos[e:e + 1, :] == slot + c0
            idx_ref[e, c0:c0 + chunk, :] = jnp.sum(jnp.where(hit, tok, 0.0), axis=1, keepdims=True).astype(jnp.int32)
            gate_ref[e, c0:c0 + chunk, :] = jnp.sum(jnp.where(hit, aff[e:e + 1, :], 0.0), axis=1, keepdims=True)


def _route_select(lt, bsz, n, col_block0):
    ne = N_EXPERTS
    cap = CAPACITY_FACTOR * n // ne
    chunk = min(cap, 128)
    nt = n // ROW_TILE
    return pl.pallas_call(
        functools.partial(_route_select_body, cap=cap, chunk=chunk, tile=ROW_TILE),
        grid=(bsz,),
        in_specs=[pl.BlockSpec((ne, n), lambda b: (0, col_block0 + b))],
        out_specs=[pl.BlockSpec((None, ne, n), lambda b: (b, 0, 0)),
                   pl.BlockSpec((None, ne, cap, 1), lambda b: (b, 0, 0, 0)),
                   pl.BlockSpec((None, ne, cap, 1), lambda b: (b, 0, 0, 0)),
                   pl.BlockSpec((None, ne, nt), lambda b: (b, 0, 0))],
        out_shape=[jax.ShapeDtypeStruct((bsz, ne, n), jnp.int32),
                   jax.ShapeDtypeStruct((bsz, ne, cap, 1), jnp.int32),
                   jax.ShapeDtypeStruct((bsz, ne, cap, 1), F32),
                   jax.ShapeDtypeStruct((bsz, ne, nt), jnp.int32)],
        compiler_params=_params("parallel"),
        name="route_select",
    )(lt)


def _gather_body(idx_ref, src_ref, o_ref, buf_ref, sem):
    b, e = pl.program_id(0), pl.program_id(1)
    cap = buf_ref.shape[0]

    def row_copy(s):
        return pltpu.make_async_copy(src_ref.at[pl.ds(idx_ref[b, e, s], 1)], buf_ref.at[pl.ds(s, 1)], sem)

    def start(s, carry):
        row_copy(s).start()
        return carry

    def wait(s, carry):
        row_copy(s).wait()
        return carry

    unroll = 8 if cap % 8 == 0 else 1
    lax.fori_loop(0, cap, start, 0, unroll=unroll)
    lax.fori_loop(0, cap, wait, 0, unroll=unroll)
    lo, hi = _unpack_halves(buf_ref[...])
    half = lo.shape[1]
    o_ref[:, 0:half] = lo
    o_ref[:, half:] = hi


def _gather(idx, actp):
    bsz, ne, cap = idx.shape
    half = actp.shape[1]
    return pl.pallas_call(
        _gather_body,
        grid_spec=pltpu.PrefetchScalarGridSpec(
            num_scalar_prefetch=1,
            grid=(bsz, ne),
            in_specs=[pl.BlockSpec(memory_space=pl.ANY)],
            out_specs=pl.BlockSpec((None, None, cap, 2 * half), lambda b, e, idx_ref: (b, e, 0, 0)),
            scratch_shapes=[pltpu.VMEM((cap, half), jnp.uint32), pltpu.SemaphoreType.DMA(())]),
        out_shape=jax.ShapeDtypeStruct((bsz, ne, cap, 2 * half), BF16),
        compiler_params=_params("arbitrary", "arbitrary"),
        name="moe_gather",
    )(idx, actp)


def _combine_body(ts_ref, pos_ref, ye_ref, h_ref, gate_ref, g_ref, b_ref, *rest, alpha, window, chunk, tiles,
                  with_next):
    if with_next:
        shift_ref, scale_ref, hn_ref, act_ref, stage_ref, slow_ref, sems, slow_sem = rest
    else:
        hn_ref, stage_ref, slow_ref, sems, slow_sem = rest
    ne, tt = pos_ref.shape
    cap_total = ye_ref.shape[2]
    i = pl.program_id(0)
    buf = i % 2

    def window_start(step, e):
        b = step // tiles
        s = jnp.minimum((ts_ref[b, e, step % tiles] // 16) * 16, cap_total - window)
        return b, pl.multiple_of(s, 16)

    def window_copy(step, e, slot):
        b, s = window_start(step, e)
        return pltpu.make_async_copy(ye_ref.at[b, e, pl.ds(s, window)],
                                     stage_ref.at[slot, pl.ds(e * window, window)], sems.at[slot])

    @pl.when(i == 0)
    def _():
        for e in range(ne):
            window_copy(i, e, buf).start()

    @pl.when(i + 1 < pl.num_programs(0))
    def _():
        for e in range(ne):
            window_copy(i + 1, e, 1 - buf).start()

    for e in range(ne):
        window_copy(i, e, buf).wait()

    pos = pos_ref[...]
    slot_iota = lax.broadcasted_iota(jnp.int32, (window, tt), 0)
    starts, blocks, late = [], [], None
    for e in range(ne):
        starts.append(window_start(i, e)[1])
        rel = pos[e:e + 1, :] - starts[e]
        blocks.append(jnp.where(rel == slot_iota, 1.0, 0.0).astype(BF16))
        late = (rel >= window) if late is None else (late | (rel >= window))
    onehot = jnp.concatenate(blocks, axis=0)
    hn_ref[...] = lax.dot_general(onehot, stage_ref[buf], (((0,), (0,)), ((), ())), preferred_element_type=F32)

    @pl.when(jnp.max(jnp.where(late, 1, 0)) > 0)
    def _():
        chunk_iota = lax.broadcasted_iota(jnp.int32, (chunk, tt), 0)
        for e in range(ne):
            beyond = pos[e:e + 1, :] - starts[e] >= window
            for c0 in range(0, cap_total, chunk):
                copy = pltpu.make_async_copy(ye_ref.at[i // tiles, e, pl.ds(c0, chunk)], slow_ref, slow_sem)
                copy.start()
                copy.wait()
                hit = jnp.where(beyond & (pos[e:e + 1, :] - c0 == chunk_iota), 1.0, 0.0).astype(BF16)
                hn_ref[...] += lax.dot_general(hit, slow_ref[...], (((0,), (0,)), ((), ())),
                                               preferred_element_type=F32)

    sub = min(tt, 128)
    for r in range(tt // sub):
        rs = slice(r * sub, (r + 1) * sub)
        hn = _norm_rows(alpha * h_ref[rs, :] + gate_ref[...] * hn_ref[rs, :]) * g_ref[...] + b_ref[...]
        hn_ref[rs, :] = hn
        if with_next:
            act_ref[rs, :] = (_norm_rows(hn) * (1.0 + scale_ref[...]) + shift_ref[...]).astype(BF16)


def _slow_chunk(cap_total):
    for c in range(min(cap_total, 272) // 16 * 16, 0, -16):
        if cap_total % c == 0:
            return c
    raise ValueError(cap_total)


def _combine_resid(ts, pos, ye, h, mod, next_mod, ln_g, ln_b, bsz, n, ctx_len, alpha):
    ne, cap_total, d = ye.shape[1:]
    tt = ROW_TILE
    nt, ntc = n // tt, ctx_len // tt
    tiles = nt + ntc
    with_next = next_mod is not None
    window = min(COMBINE_WINDOW, cap_total)

    def sample(i):
        return i // tiles

    def row_tile(i):
        b, t = i // tiles, i % tiles
        return jnp.where(t < nt, b * nt + t, bsz * nt + b * ntc + (t - nt))

    def mod_spec(which):
        return pl.BlockSpec((None, None, 1, d),
                            lambda i, ts_ref: (jnp.where(i % tiles < nt, sample(i), bsz), which, 0, 0))

    vec = pl.BlockSpec((1, d), lambda i, ts_ref: (0, 0))
    blk = pl.BlockSpec((tt, d), lambda i, ts_ref: (row_tile(i), 0))
    in_specs = [pl.BlockSpec((None, ne, tt), lambda i, ts_ref: (sample(i), 0, i % tiles)),
                pl.BlockSpec(memory_space=pl.ANY), blk, mod_spec(5), vec, vec]
    args = [pos, ye, h, mod, ln_g.reshape(1, d), ln_b.reshape(1, d)]
    rows = bsz * tiles * tt
    out_specs, out_shape = [blk], [jax.ShapeDtypeStruct((rows, d), F32)]
    if with_next:
        in_specs += [mod_spec(0), mod_spec(1)]
        args += [next_mod, next_mod]
        out_specs.append(blk)
        out_shape.append(jax.ShapeDtypeStruct((rows, d), BF16))
    return pl.pallas_call(
        functools.partial(_combine_body, alpha=alpha, window=window, chunk=_slow_chunk(cap_total), tiles=tiles,
                          with_next=with_next),
        grid_spec=pltpu.PrefetchScalarGridSpec(
            num_scalar_prefetch=1,
            grid=(bsz * tiles,),
            in_specs=in_specs,
            out_specs=out_specs,
            scratch_shapes=[pltpu.VMEM((2, ne * window, d), BF16), pltpu.VMEM((_slow_chunk(cap_total), d), BF16),
                            pltpu.SemaphoreType.DMA((2,)), pltpu.SemaphoreType.DMA(())]),
        out_shape=out_shape,
        compiler_params=_params("arbitrary"),
        name="moe_combine",
    )(ts, *args)


def kernel(x, c, ctx, c_ctx, w_mod, b_mod, w_in, q_gain, k_gain, gm_ln_g, gm_ln_b, w_spatial, b_spatial, b_gates,
           ml_gain, w_out, ln1_g, ln1_b, w_router, w_e_gate, w_e_up, w_e_down, ln2_g, ln2_b):
    dm = _dims()
    bsz, n, d = x.shape
    ctx_len = ctx.shape[1]
    depth = w_mod.shape[0]
    alpha = (2 * depth) ** 0.25
    lat_rows = bsz * n
    main_w = dm["main_w"]
    n_gates = 4 * MLSTM_HEADS
    cap_l = CAPACITY_FACTOR * n // N_EXPERTS
    cap_c = CAPACITY_FACTOR * ctx_len // N_EXPERTS
    sample = jnp.arange(bsz, dtype=jnp.int32)[:, None, None]

    cc = jnp.concatenate([c, c_ctx[None], jnp.zeros((MOD_ROWS - bsz - 1, d), F32)], axis=0)
    mods = _mod_vectors(cc, w_mod, b_mod).reshape(depth, MOD_ROWS, N_MOD, 1, d)
    cos, sin = _rope_tables(n, ROW_TILE)

    h = jnp.concatenate([x.reshape(lat_rows, d), ctx.reshape(bsz * ctx_len, d)], axis=0)
    act = _ln_mod(h, mods[0], bsz, n)
    for l in range(depth):
        last = l == depth - 1
        rows = lat_rows if last else h.shape[0]
        mod = mods[l]

        p = _in_proj(act, w_in, l, main_w)
        w_g = jnp.pad(w_in[l][:, main_w:].astype(BF16), ((0, 0), (0, HEAD_DIM - n_gates)))
        b_g = jnp.pad(b_gates[l], (0, HEAD_DIM - n_gates)).reshape(1, HEAD_DIM)
        gates = _matmul(act, w_g, b_g, "gate_proj")[:, :n_gates]
        gates_seq = jnp.concatenate([gates[lat_rows:].reshape(bsz, ctx_len, n_gates),
                                     gates[:lat_rows].reshape(bsz, n, n_gates)], axis=1)
        gates_rows = jnp.swapaxes(gates_seq, 1, 2)

        qb, kb, vb = _qkv_prep(p, cos, sin, q_gain[l], k_gain[l], bsz, n, ctx_len)
        att = _attention(qb, kb, vb, bsz, n, ctx_len, not last)
        gm = _gmlp(p, gm_ln_g[l], gm_ln_b[l], w_spatial[l], b_spatial[l])
        hf, hb = _mlstm(p, gates_rows, bsz, n, ctx_len)
        ml = _mlstm_out(hf, hb, p, ml_gain[l], rows)
        h, actp = _outproj(att, gm, ml, w_out[l].astype(BF16), h, mod, ln1_g[l], ln1_b[l], rows, bsz, n, alpha)

        w_r = jnp.pad(w_router[l].astype(BF16), ((0, 0), (0, HEAD_DIM - N_EXPERTS)))
        lt = _router(actp, w_r)
        pos, idx_l, gate, ts = _route_select(lt, bsz, n, 0)
        idx = idx_l[..., 0] + n * sample
        if not last:
            pos_c, idx_c, gate_c, ts_c = _route_select(lt, bsz, ctx_len, lat_rows // ctx_len)
            idx = jnp.concatenate([idx, idx_c[..., 0] + lat_rows + ctx_len * sample], axis=2)
            gate = jnp.concatenate([gate, gate_c], axis=2)
            pos = jnp.concatenate([pos, jnp.where(pos_c >= 0, pos_c + cap_l, -1)], axis=2)
            ts = jnp.concatenate([ts, ts_c + cap_l], axis=2)
        xe = _gather(idx, actp)
        hid = _expert_up(xe, w_e_gate, w_e_up, l)
        ye = _expert_down(hid, w_e_down, gate, l)
        if not last:
            h, act = _combine_resid(ts, pos, ye, h, mod, mods[l + 1], ln2_g[l], ln2_b[l], bsz, n, ctx_len, alpha)
        else:
            h, = _combine_resid(ts, pos, ye, h, mod, None, ln2_g[l], ln2_b[l], bsz, n, 0, alpha)
    return h.reshape(bsz, n, d)
```

```python
import functools

import jax
import jax.numpy as jnp
from jax import lax
from jax.experimental import pallas as pl
from jax.experimental.pallas import tpu as pltpu

D_MODEL = 4096
BATCH = 4
SEQ = 4096
DEPTH = 2
CTX_LEN = 256
GRID_W = 64
HEAD_DIM = 128
ATT_Q_HEADS = 16
ATT_KV_HEADS = 4
ROPE_THETA = 10000.0
GMLP_CHUNK = 128
GMLP_GROUPS = 8
MLSTM_HEADS = 4
MLSTM_CHUNK = 128
N_EXPERTS = 16
CAPACITY_FACTOR = 2
N_MOD = 6
EPS = 1e-6
LOG2_E = 1.4426950408889634

F32 = jnp.float32
BF16 = jnp.bfloat16

ROW_TILE = 256
MOD_ROWS = 8
COMBINE_WINDOW = 64
VMEM_LIMIT = 56 * 1024 * 1024


def _dims():
    d = D_MODEL
    att_w = ATT_Q_HEADS * HEAD_DIM
    kv_w = ATT_KV_HEADS * HEAD_DIM
    gm_w = d // 4
    ml_w = d // 4
    offs, start = {}, 0
    for name, width in (("att_q", att_w), ("att_k", kv_w), ("att_v", kv_w), ("gm_u", gm_w), ("gm_v", gm_w),
                        ("ml_q", ml_w), ("ml_k", ml_w), ("ml_v", ml_w), ("ml_o", ml_w),
                        ("ml_gates", 4 * MLSTM_HEADS)):
        offs[name] = start
        start += width
    return dict(d=d, att_w=att_w, kv_w=kv_w, gm_w=gm_w, ml_w=ml_w, offs=offs, proj_w=start,
                main_w=offs["ml_gates"], ml_dh=ml_w // MLSTM_HEADS, ff=d // 4)


def _tile(n, prefs):
    for t in prefs:
        if n % t == 0:
            return t
    return n


def _params(*sem):
    return pltpu.CompilerParams(dimension_semantics=sem, vmem_limit_bytes=VMEM_LIMIT)


def _sigmoid(x):
    return 1.0 / (1.0 + jnp.exp(-x))


def _norm_rows(x):
    mu = jnp.mean(x, axis=-1, keepdims=True)
    xc = x - mu
    var = jnp.mean(xc * xc, axis=-1, keepdims=True)
    return xc * lax.rsqrt(var + EPS)


def _gelu_tanh(x):
    return 0.5 * x * (1.0 + jnp.tanh(0.7978845608028654 * (x + 0.044715 * (x * x * x))))


def _mod_body(x_ref, w_ref, b_ref, o_ref):
    x = x_ref[...]
    xs = (x * _sigmoid(x)).astype(BF16)
    o_ref[...] = jnp.dot(xs, w_ref[...].astype(BF16), preferred_element_type=F32) + b_ref[...]


def _mod_vectors(cc, w_mod, b_mod):
    depth, d, n = w_mod.shape
    tn = _tile(n, (512, 256, 128))
    return pl.pallas_call(
        _mod_body,
        grid=(depth, n // tn),
        in_specs=[pl.BlockSpec((MOD_ROWS, d), lambda l, j: (0, 0)),
                  pl.BlockSpec((None, d, tn), lambda l, j: (l, 0, j)),
                  pl.BlockSpec((None, 1, tn), lambda l, j: (l, 0, j))],
        out_specs=pl.BlockSpec((None, MOD_ROWS, tn), lambda l, j: (l, 0, j)),
        out_shape=jax.ShapeDtypeStruct((depth, MOD_ROWS, n), F32),
        compiler_params=_params("parallel", "parallel"),
        name="mod_vectors",
    )(cc, w_mod, b_mod.reshape(depth, 1, n))


def _mod_spec(which, tiles_per_sample, bsz):
    d = D_MODEL
    return pl.BlockSpec((None, None, 1, d),
                        lambda i, *_: (jnp.minimum(i // tiles_per_sample, bsz), which, 0, 0))


def _ln_mod_body(h_ref, h_ctx_ref, shift_ref, scale_ref, o_ref, *, lat_tiles):
    h = jnp.where(pl.program_id(0) < lat_tiles, h_ref[...], h_ctx_ref[...])
    o_ref[...] = (_norm_rows(h) * (1.0 + scale_ref[...]) + shift_ref[...]).astype(BF16)


def _ln_mod(h_lat, h_ctx, mod, bsz, n):
    d = h_lat.shape[1]
    rows = h_lat.shape[0] + h_ctx.shape[0]
    tm = ROW_TILE
    tps = n // tm
    return pl.pallas_call(
        functools.partial(_ln_mod_body, lat_tiles=bsz * tps),
        grid=(rows // tm,),
        in_specs=[*_stream_specs((tm, d), bsz * tps), _mod_spec(0, tps, bsz), _mod_spec(1, tps, bsz)],
        out_specs=pl.BlockSpec((tm, d), lambda i: (i, 0)),
        out_shape=jax.ShapeDtypeStruct((rows, d), BF16),
        compiler_params=_params("parallel"),
        name="ln_mod",
    )(h_lat, h_ctx, mod, mod)


def _mm_body(a_ref, w_ref, b_ref, o_ref):
    o_ref[...] = jnp.dot(a_ref[...], w_ref[...], preferred_element_type=F32) + b_ref[...]


def _matmul(a, w, bias, name):
    m, k = a.shape
    n = w.shape[1]
    tm = _tile(m, (1024, 512, 256))
    tn = _tile(n, (512, 256, 128))
    return pl.pallas_call(
        _mm_body,
        grid=(m // tm, n // tn),
        in_specs=[pl.BlockSpec((tm, k), lambda i, j: (i, 0)),
                  pl.BlockSpec((k, tn), lambda i, j: (0, j)),
                  pl.BlockSpec((1, tn), lambda i, j: (0, j))],
        out_specs=pl.BlockSpec((tm, tn), lambda i, j: (i, j)),
        out_shape=jax.ShapeDtypeStruct((m, n), F32),
        compiler_params=_params("parallel", "parallel"),
        name=name,
    )(a, w, bias)


def _in_proj_body(a_ref, w_ref, o_ref, wb_ref):
    @pl.when(pl.program_id(1) == 0)
    def _():
        wb_ref[...] = w_ref[...].astype(BF16)

    o_ref[...] = lax.dot_general(a_ref[...], wb_ref[...], (((1,), (1,)), ((), ())), preferred_element_type=F32)


def _in_proj(a, w_in_t, layer, n_out):
    m, k = a.shape
    tm = _tile(m, (1024, 512, 256))
    tn = _tile(n_out, (512, 256, 128))
    return pl.pallas_call(
        _in_proj_body,
        grid=(n_out // tn, m // tm),
        in_specs=[pl.BlockSpec((tm, k), lambda j, i: (i, 0)),
                  pl.BlockSpec((None, tn, k), lambda j, i: (layer, j, 0))],
        out_specs=pl.BlockSpec((tm, tn), lambda j, i: (i, j)),
        out_shape=jax.ShapeDtypeStruct((m, n_out), F32),
        scratch_shapes=[pltpu.VMEM((tn, k), BF16)],
        compiler_params=_params("parallel", "arbitrary"),
        name="in_proj",
    )(a, w_in_t)


def _rope_tables(n, tm):
    rows = n // GRID_W
    row = jnp.repeat(jnp.arange(rows), GRID_W).astype(F32)
    col = jnp.tile(jnp.arange(GRID_W), rows).astype(F32)
    n_freq = HEAD_DIM // 4
    inv_freq = ROPE_THETA ** (-jnp.arange(n_freq, dtype=F32) / n_freq)
    ang_r, ang_c = row[:, None] * inv_freq, col[:, None] * inv_freq
    cos = jnp.concatenate([jnp.cos(ang_r), jnp.cos(ang_r), jnp.cos(ang_c), jnp.cos(ang_c)], axis=1)
    sin = jnp.concatenate([-jnp.sin(ang_r), jnp.sin(ang_r), -jnp.sin(ang_c), jnp.sin(ang_c)], axis=1)
    cos = jnp.concatenate([cos, jnp.ones((tm, HEAD_DIM), F32)], axis=0)
    sin = jnp.concatenate([sin, jnp.zeros((tm, HEAD_DIM), F32)], axis=0)
    return cos, sin


def _qkv_prep_body(q_ref, k_ref, v_ref, cos_ref, sin_ref, qg_ref, kg_ref, qo_ref, ko_ref, vo_ref):
    cos, sin = cos_ref[...], sin_ref[...]
    lane = lax.broadcasted_iota(jnp.int32, cos.shape, 1)
    first_half = (lane % (HEAD_DIM // 2)) < (HEAD_DIM // 4)

    def norm_rope(x, gain):
        y = x * lax.rsqrt(jnp.mean(x * x, axis=-1, keepdims=True) + EPS) * gain
        partner = jnp.where(first_half, pltpu.roll(y, HEAD_DIM - HEAD_DIM // 4, 1), pltpu.roll(y, HEAD_DIM // 4, 1))
        return y * cos + partner * sin

    qg = qg_ref[...] * (HEAD_DIM ** -0.5 * LOG2_E)
    for h in range(ATT_Q_HEADS):
        sl = slice(h * HEAD_DIM, (h + 1) * HEAD_DIM)
        qo_ref[:, sl] = norm_rope(q_ref[:, sl], qg).astype(BF16)
    for h in range(ATT_KV_HEADS):
        sl = slice(h * HEAD_DIM, (h + 1) * HEAD_DIM)
        ko_ref[:, sl] = norm_rope(k_ref[:, sl], kg_ref[...]).astype(BF16)
        vo_ref[:, 2 * h * HEAD_DIM:(2 * h + 1) * HEAD_DIM] = v_ref[:, sl].astype(BF16)
        vo_ref[:, (2 * h + 1) * HEAD_DIM:(2 * h + 2) * HEAD_DIM] = jnp.where(lane == 0, 1.0, 0.0).astype(BF16)


def _qkv_prep(p, cos, sin, q_gain, k_gain, bsz, n, ctx_len):
    dm = _dims()
    rows = p.shape[0]
    tm = ROW_TILE
    tps, tpc = n // tm, ctx_len // tm
    nlat = bsz * tps
    att_w, kv_w = dm["att_w"], dm["kv_w"]
    ntot = n + ctx_len

    def table_map(i):
        return (jnp.where(i < nlat, i % tps, tps), 0)

    def kv_map(i):
        j = i - nlat
        return (jnp.where(i < nlat, i // tps, j // tpc), jnp.where(i < nlat, tpc + i % tps, j % tpc), 0)

    return pl.pallas_call(
        _qkv_prep_body,
        grid=(rows // tm,),
        in_specs=[pl.BlockSpec((tm, att_w), lambda i: (i, dm["offs"]["att_q"] // att_w)),
                  pl.BlockSpec((tm, kv_w), lambda i: (i, dm["offs"]["att_k"] // kv_w)),
                  pl.BlockSpec((tm, kv_w), lambda i: (i, dm["offs"]["att_v"] // kv_w)),
                  pl.BlockSpec((tm, HEAD_DIM), table_map),
                  pl.BlockSpec((tm, HEAD_DIM), table_map),
                  pl.BlockSpec((1, HEAD_DIM), lambda i: (0, 0)),
                  pl.BlockSpec((1, HEAD_DIM), lambda i: (0, 0))],
        out_specs=[pl.BlockSpec((tm, att_w), lambda i: (i, 0)),
                   pl.BlockSpec((None, tm, kv_w), kv_map),
                   pl.BlockSpec((None, tm, 2 * kv_w), kv_map)],
        out_shape=[jax.ShapeDtypeStruct((rows, att_w), BF16),
                   jax.ShapeDtypeStruct((bsz, ntot, kv_w), BF16),
                   jax.ShapeDtypeStruct((bsz, ntot, 2 * kv_w), BF16)],
        compiler_params=_params("parallel"),
        name="qkv_prep",
    )(p, p, p, cos, sin, q_gain.reshape(1, HEAD_DIM), k_gain.reshape(1, HEAD_DIM))


def _attn_heads(q_ref, k, v, o_ref):
    group = ATT_Q_HEADS // ATT_KV_HEADS

    def scores(h):
        return lax.dot_general(q_ref[:, h * HEAD_DIM:(h + 1) * HEAD_DIM], k, (((1,), (1,)), ((), ())),
                               preferred_element_type=F32)

    s = scores(0)
    for h in range(group):
        s_next = scores(h + 1) if h + 1 < group else None
        e = jnp.exp2(s - jnp.max(s, axis=-1, keepdims=True))
        o = jnp.dot(e.astype(BF16), v, preferred_element_type=F32)
        o_ref[:, h * HEAD_DIM:(h + 1) * HEAD_DIM] = (o[:, 0:HEAD_DIM] / o[:, HEAD_DIM:HEAD_DIM + 1]).astype(BF16)
        s = s_next


def _attn_body(q_ref, k_ref, v_ref, o_ref, *, lat_q_tiles, ctx_len):
    qi = pl.program_id(2)

    @pl.when(qi < lat_q_tiles)
    def _():
        _attn_heads(q_ref, k_ref[...], v_ref[...], o_ref)

    @pl.when(qi >= lat_q_tiles)
    def _():
        _attn_heads(q_ref, k_ref[0:ctx_len, :], v_ref[0:ctx_len, :], o_ref)


def _attention(qb, kb, vb, bsz, n, ctx_len, with_ctx):
    rows_all = qb.shape[0]
    tq = ROW_TILE
    tps, tpc = n // tq, ctx_len // tq
    nlat = bsz * tps
    group_w = (ATT_Q_HEADS // ATT_KV_HEADS) * HEAD_DIM
    ntot = n + ctx_len
    q_tiles = tps + (tpc if with_ctx else 0)
    rows = rows_all if with_ctx else bsz * n

    def q_map(b, g, qi):
        return (jnp.where(qi < tps, b * tps + qi, nlat + b * tpc + (qi - tps)), g)

    return pl.pallas_call(
        functools.partial(_attn_body, lat_q_tiles=tps, ctx_len=ctx_len),
        grid=(bsz, ATT_KV_HEADS, q_tiles),
        in_specs=[pl.BlockSpec((tq, group_w), q_map),
                  pl.BlockSpec((None, ntot, HEAD_DIM), lambda b, g, qi: (b, 0, g)),
                  pl.BlockSpec((None, ntot, 2 * HEAD_DIM), lambda b, g, qi: (b, 0, g))],
        out_specs=pl.BlockSpec((tq, group_w), q_map),
        out_shape=jax.ShapeDtypeStruct((rows, ATT_Q_HEADS * HEAD_DIM), BF16),
        compiler_params=_params("parallel", "parallel", "parallel"),
        name="attention",
    )(qb, kb, vb)


def _gmlp_body(u_ref, v_ref, g_ref, b_ref, ws_ref, bs_ref, o_ref):
    u = _gelu_tanh(u_ref[...])
    v = (_norm_rows(_gelu_tanh(v_ref[...])) * g_ref[...] + b_ref[...]).astype(BF16)
    gd = v.shape[1] // GMLP_GROUPS
    for c in range(v.shape[0] // GMLP_CHUNK):
        rs = slice(c * GMLP_CHUNK, (c + 1) * GMLP_CHUNK)
        for g in range(GMLP_GROUPS):
            cs = slice(g * gd, (g + 1) * gd)
            sv = jnp.dot(ws_ref[g].astype(BF16), v[rs, cs], preferred_element_type=F32) + bs_ref[:, g:g + 1]
            o_ref[rs, cs] = (u[rs, cs] * sv).astype(BF16)


def _gmlp(p, ln_g, ln_b, w_s, b_s):
    dm = _dims()
    rows = p.shape[0]
    tm = ROW_TILE
    gm_w = dm["gm_w"]
    return pl.pallas_call(
        _gmlp_body,
        grid=(rows // tm,),
        in_specs=[pl.BlockSpec((tm, gm_w), lambda i: (i, dm["offs"]["gm_u"] // gm_w)),
                  pl.BlockSpec((tm, gm_w), lambda i: (i, dm["offs"]["gm_v"] // gm_w)),
                  pl.BlockSpec((1, gm_w), lambda i: (0, 0)),
                  pl.BlockSpec((1, gm_w), lambda i: (0, 0)),
                  pl.BlockSpec((GMLP_GROUPS, GMLP_CHUNK, GMLP_CHUNK), lambda i: (0, 0, 0)),
                  pl.BlockSpec((GMLP_CHUNK, GMLP_GROUPS), lambda i: (0, 0))],
        out_specs=pl.BlockSpec((tm, gm_w), lambda i: (i, 0)),
        out_shape=jax.ShapeDtypeStruct((rows, gm_w), BF16),
        compiler_params=_params("parallel"),
        name="gmlp",
    )(p, p, ln_g.reshape(1, gm_w), ln_b.reshape(1, gm_w), w_s, b_s.T)


def _mlstm_chain_step(q_ref, k_ref, v_ref, g_ref, o_ref, c_ref, n_ref, m_ref, direction, head, cols, masks):
    seen, seen_t, eye = masks
    heads = MLSTM_HEADS

    def to_col(x_row):
        return jnp.sum(jnp.where(eye, x_row, 0.0), axis=1, keepdims=True)

    gi = 2 * direction * heads + head
    i_row = g_ref[gi:gi + 1, :]
    f_raw = g_ref[gi + heads:gi + heads + 1, :]
    lf_row = jnp.minimum(f_raw, 0.0) - jnp.log1p(jnp.exp(-jnp.abs(f_raw)))
    lf_col, i_col = to_col(lf_row), to_col(i_row)
    bcum_col = jnp.sum(jnp.where(seen, lf_row, 0.0), axis=1, keepdims=True)
    bcum_row = jnp.sum(jnp.where(seen_t, lf_col, 0.0), axis=0, keepdims=True)
    b_last = jnp.sum(lf_row, axis=1, keepdims=True)

    m_old = m_ref[direction, head]
    g_row = b_last - bcum_row + i_row
    g_col = b_last - bcum_col + i_col
    m_new = jnp.maximum(b_last + m_old, jnp.max(g_row, axis=1, keepdims=True))
    w_prev = jnp.exp(b_last + m_old - m_new)

    q = q_ref[:, cols]
    k = k_ref[:, cols] * ((cols.stop - cols.start) ** -0.5)
    vb = v_ref[:, cols].astype(BF16)
    qb, kb = q.astype(BF16), k.astype(BF16)
    c_old, n_old = c_ref[direction, head], n_ref[direction, head]

    a_col = bcum_col + m_old
    dlog = jnp.where(seen, bcum_col - bcum_row + i_row, -jnp.inf)
    mt = jnp.maximum(a_col, jnp.max(dlog, axis=1, keepdims=True))
    s = lax.dot_general(qb, kb, (((1,), (1,)), ((), ())), preferred_element_type=F32) * jnp.exp(dlog - mt)
    wa = jnp.exp(a_col - mt)
    num = (wa * jnp.dot(qb, c_old.astype(BF16), preferred_element_type=F32)
           + jnp.dot(s.astype(BF16), vb, preferred_element_type=F32))
    den = wa * jnp.sum(q * n_old, axis=1, keepdims=True) + jnp.sum(s, axis=1, keepdims=True)
    o_ref[:, cols] = num / jnp.maximum(jnp.abs(den), jnp.exp(-mt))

    kw = k * jnp.exp(g_col - m_new)
    c_ref[direction, head] = w_prev * c_old + lax.dot_general(kw.astype(BF16), vb, (((0,), (0,)), ((), ())),
                                                              preferred_element_type=F32)
    n_ref[direction, head] = w_prev * n_old + jnp.sum(kw, axis=0, keepdims=True)
    m_ref[direction, head] = m_new


def _mlstm_body(qf_ref, kf_ref, vf_ref, gf_ref, qb_ref, kb_ref, vb_ref, gb_ref, of_ref, ob_ref,
                c_ref, n_ref, m_ref):
    @pl.when(pl.program_id(1) == 0)
    def _():
        c_ref[...] = jnp.zeros_like(c_ref)
        n_ref[...] = jnp.zeros_like(n_ref)
        m_ref[...] = jnp.zeros_like(m_ref)

    chunk = qf_ref.shape[0]
    dh = qf_ref.shape[1] // MLSTM_HEADS
    row = lax.broadcasted_iota(jnp.int32, (chunk, chunk), 0)
    col = lax.broadcasted_iota(jnp.int32, (chunk, chunk), 1)
    eye = row == col
    fwd_masks = (col <= row, row <= col, eye)
    bwd_masks = (col >= row, row >= col, eye)
    for head in range(MLSTM_HEADS):
        cols = slice(head * dh, (head + 1) * dh)
        _mlstm_chain_step(qf_ref, kf_ref, vf_ref, gf_ref, of_ref, c_ref, n_ref, m_ref, 0, head, cols, fwd_masks)
        _mlstm_chain_step(qb_ref, kb_ref, vb_ref, gb_ref, ob_ref, c_ref, n_ref, m_ref, 1, head, cols, bwd_masks)


def _mlstm(p, gates_rows, bsz, n, ctx_len):
    dm = _dims()
    rows = p.shape[0]
    lc = MLSTM_CHUNK
    dh, ml_w = dm["ml_dh"], dm["ml_w"]
    heads = MLSTM_HEADS
    ncc, nlc = ctx_len // lc, n // lc
    steps = ncc + nlc
    lat_chunks = bsz * nlc

    def seq_chunk(dr, j):
        return j if dr == 0 else jnp.where(j < ncc, ncc - 1 - j, ncc + nlc - 1 - (j - ncc))

    def row_chunk(b, dr, j):
        c = seq_chunk(dr, j)
        return jnp.where(c < ncc, lat_chunks + b * ncc + c, b * nlc + (c - ncc))

    def proj_spec(name, dr):
        return pl.BlockSpec((lc, ml_w), lambda b, j: (row_chunk(b, dr, j), dm["offs"][name] // ml_w))

    def dir_specs(dr):
        return [proj_spec("ml_q", dr), proj_spec("ml_k", dr), proj_spec("ml_v", dr),
                pl.BlockSpec((None, 4 * heads, lc), lambda b, j: (b, 0, seq_chunk(dr, j)))]

    return pl.pallas_call(
        _mlstm_body,
        grid=(bsz, steps),
        in_specs=dir_specs(0) + dir_specs(1),
        out_specs=[pl.BlockSpec((lc, ml_w), lambda b, j: (row_chunk(b, 0, j), 0)),
                   pl.BlockSpec((lc, ml_w), lambda b, j: (row_chunk(b, 1, j), 0))],
        out_shape=[jax.ShapeDtypeStruct((rows, ml_w), F32), jax.ShapeDtypeStruct((rows, ml_w), F32)],
        scratch_shapes=[pltpu.VMEM((2, heads, dh, dh), F32), pltpu.VMEM((2, heads, 1, dh), F32),
                        pltpu.VMEM((2, heads, 1, 1), F32)],
        compiler_params=_params("parallel", "arbitrary"),
        name="mlstm_scan",
    )(p, p, p, gates_rows, p, p, p, gates_rows)


def _mlstm_out_body(hf_ref, hb_ref, o_ref, gain_ref, out_ref):
    dh = out_ref.shape[1] // MLSTM_HEADS
    for h in range(MLSTM_HEADS):
        sl = slice(h * dh, (h + 1) * dh)
        hs = hf_ref[:, sl] + hb_ref[:, sl]
        hn = hs * lax.rsqrt(jnp.mean(hs * hs, axis=-1, keepdims=True) + EPS) * gain_ref[:, sl]
        out_ref[:, sl] = (_sigmoid(o_ref[:, sl]) * hn).astype(BF16)


def _mlstm_out(hf, hb, p, gain, rows):
    dm = _dims()
    tm = ROW_TILE
    ml_w = dm["ml_w"]
    blk = pl.BlockSpec((tm, ml_w), lambda i: (i, 0))
    return pl.pallas_call(
        _mlstm_out_body,
        grid=(rows // tm,),
        in_specs=[blk, blk,
                  pl.BlockSpec((tm, ml_w), lambda i: (i, dm["offs"]["ml_o"] // ml_w)),
                  pl.BlockSpec((1, ml_w), lambda i: (0, 0))],
        out_specs=blk,
        out_shape=jax.ShapeDtypeStruct((rows, ml_w), BF16),
        compiler_params=_params("parallel"),
        name="mlstm_out",
    )(hf, hb, p, gain.reshape(1, ml_w))


def _pack_halves(y):
    half = y.shape[1] // 2
    lo = lax.bitcast_convert_type(y[:, :half].astype(BF16).astype(F32), jnp.uint32) >> 16
    hi = lax.bitcast_convert_type(y[:, half:].astype(BF16).astype(F32), jnp.uint32) & jnp.uint32(0xFFFF0000)
    return lo | hi


def _unpack_halves(w):
    lo = lax.bitcast_convert_type(w << 16, F32).astype(BF16)
    hi = lax.bitcast_convert_type(w & jnp.uint32(0xFFFF0000), F32).astype(BF16)
    return lo, hi


def _outproj_body(att_ref, gm_ref, ml_ref, w1_ref, w2_ref, w3_ref, h_ref, h_ctx_ref, gate_ref, g_ref, b_ref,
                  shift_ref, scale_ref, hn_ref, act_ref, *, alpha, tn, sub, lat_tiles):
    j = pl.program_id(1)
    cols = pl.ds(pl.multiple_of(j * tn, tn), tn)
    mix = (jnp.dot(att_ref[...], w1_ref[...], preferred_element_type=F32)
           + jnp.dot(gm_ref[...], w2_ref[...], preferred_element_type=F32)
           + jnp.dot(ml_ref[...], w3_ref[...], preferred_element_type=F32))
    h = jnp.where(pl.program_id(0) < lat_tiles, h_ref[...], h_ctx_ref[...])
    hn_ref[:, cols] = alpha * h + gate_ref[:, cols] * mix

    @pl.when(j == pl.num_programs(1) - 1)
    def _():
        for r in range(hn_ref.shape[0] // sub):
            rs = slice(r * sub, (r + 1) * sub)
            hn = _norm_rows(hn_ref[rs, :]) * g_ref[...] + b_ref[...]
            hn_ref[rs, :] = hn
            act_ref[rs, :] = _pack_halves(_norm_rows(hn) * (1.0 + scale_ref[...]) + shift_ref[...])


def _stream_specs(block, lat_tiles, col=None):
    def lat_map(i, *rest):
        return (jnp.minimum(i, lat_tiles - 1), 0 if col is None else rest[col])

    def ctx_map(i, *rest):
        return (jnp.maximum(i - lat_tiles, 0), 0 if col is None else rest[col])

    return [pl.BlockSpec(block, lat_map), pl.BlockSpec(block, ctx_map)]


def _outproj(att, gm, ml, w_out, layer, h_parts, mod, ln_g, ln_b, rows, bsz, n, alpha):
    dm = _dims()
    d = dm["d"]
    tm = _tile(n, (512, 256))
    tn = _tile(d, (512, 256, 128))
    tps = n // tm
    att_w, gm_w, ml_w = dm["att_w"], dm["gm_w"], dm["ml_w"]
    vec = pl.BlockSpec((1, d), lambda i, j: (0, 0))
    return pl.pallas_call(
        functools.partial(_outproj_body, alpha=alpha, tn=tn, sub=min(tm, 128), lat_tiles=bsz * tps),
        grid=(rows // tm, d // tn),
        in_specs=[pl.BlockSpec((tm, att_w), lambda i, j: (i, 0)),
                  pl.BlockSpec((tm, gm_w), lambda i, j: (i, 0)),
                  pl.BlockSpec((tm, ml_w), lambda i, j: (i, 0)),
                  pl.BlockSpec((None, att_w, tn), lambda i, j: (layer, 0, j)),
                  pl.BlockSpec((None, gm_w, tn), lambda i, j: (layer, att_w // gm_w, j)),
                  pl.BlockSpec((None, ml_w, tn), lambda i, j: (layer, (att_w + gm_w) // ml_w, j)),
                  *_stream_specs((tm, tn), bsz * tps, col=0),
                  _mod_spec(2, tps, bsz), vec, vec, _mod_spec(3, tps, bsz), _mod_spec(4, tps, bsz)],
        out_specs=[pl.BlockSpec((tm, d), lambda i, j: (i, 0)), pl.BlockSpec((tm, d // 2), lambda i, j: (i, 0))],
        out_shape=[jax.ShapeDtypeStruct((rows, d), F32), jax.ShapeDtypeStruct((rows, d // 2), jnp.uint32)],
        compiler_params=_params("parallel", "arbitrary"),
        name="outproj_ln",
    )(att, gm, ml, w_out, w_out, w_out, *h_parts, mod, ln_g.reshape(1, d), ln_b.reshape(1, d), mod, mod)


def _tile_part_copies(w_refs, stage_ref, sems, layer, tile, part, slot, n_inner, parts):
    rows, cols = stage_ref.shape[2] // parts, stage_ref.shape[3]
    rs = pl.ds(pl.multiple_of(part * rows, rows), rows)
    cs = pl.ds(pl.multiple_of((tile % n_inner) * cols, cols), cols)
    return [pltpu.make_async_copy(w.at[layer, tile // n_inner, rs, cs], stage_ref.at[slot, m, rs, :], sems.at[slot])
            for m, w in enumerate(w_refs)]


def _stream_weight_tile(w_refs, stage_ref, wb_ref, sems, layer, n_inner):
    t, b = pl.program_id(0), pl.program_id(1)
    parts = pl.num_programs(1)
    args = (w_refs, stage_ref, sems, layer)

    @pl.when((t == 0) & (b == 0))
    def _():
        for part in range(parts):
            for copy in _tile_part_copies(*args, 0, part, 0, n_inner, parts):
                copy.start()

    @pl.when(t + 1 < pl.num_programs(0))
    def _():
        for copy in _tile_part_copies(*args, t + 1, b, (t + 1) % 2, n_inner, parts):
            copy.start()

    @pl.when(b == 0)
    def _():
        for part in range(parts):
            for copy in _tile_part_copies(*args, t, part, t % 2, n_inner, parts):
                copy.wait()
        wb_ref[...] = stage_ref[t % 2].astype(BF16)


def _expert_up_body(x_ref, wg_ref, wu_ref, o_ref, stage_ref, wb_ref, sems, *, layer, nf):
    _stream_weight_tile((wg_ref, wu_ref), stage_ref, wb_ref, sems, layer, nf)
    x = x_ref[...]
    g = jnp.dot(x, wb_ref[0], preferred_element_type=F32)
    u = jnp.dot(x, wb_ref[1], preferred_element_type=F32)
    o_ref[...] = (g * _sigmoid(g) * u).astype(BF16)


def _expert_up(xe, wg, wu, layer):
    bsz, ne, cap, d = xe.shape
    ff = wg.shape[3]
    tf = _tile(ff, (256, 128))
    nf = ff // tf
    return pl.pallas_call(
        functools.partial(_expert_up_body, layer=layer, nf=nf),
        grid=(ne * nf, bsz),
        in_specs=[pl.BlockSpec((None, None, cap, d), lambda t, b: (b, t // nf, 0, 0)),
                  pl.BlockSpec(memory_space=pl.ANY), pl.BlockSpec(memory_space=pl.ANY)],
        out_specs=pl.BlockSpec((None, None, cap, tf), lambda t, b: (b, t // nf, 0, t % nf)),
        out_shape=jax.ShapeDtypeStruct((bsz, ne, cap, ff), BF16),
        scratch_shapes=[pltpu.VMEM((2, 2, d, tf), F32), pltpu.VMEM((2, d, tf), BF16), pltpu.SemaphoreType.DMA((2,))],
        compiler_params=_params("arbitrary", "arbitrary"),
        name="expert_up",
    )(xe, wg, wu)


def _expert_down_body(x_ref, w_ref, gate_ref, o_ref, stage_ref, wb_ref, sems, *, layer, nj):
    _stream_weight_tile((w_ref,), stage_ref, wb_ref, sems, layer, nj)
    o_ref[...] = (jnp.dot(x_ref[...], wb_ref[0], preferred_element_type=F32) * gate_ref[...]).astype(BF16)


def _expert_down(hid, wd, gate, layer):
    bsz, ne, cap, ff = hid.shape
    d = wd.shape[3]
    tn = _tile(d, (2048, 1024, 512))
    nj = d // tn
    return pl.pallas_call(
        functools.partial(_expert_down_body, layer=layer, nj=nj),
        grid=(ne * nj, bsz),
        in_specs=[pl.BlockSpec((None, None, cap, ff), lambda t, b: (b, t // nj, 0, 0)),
                  pl.BlockSpec(memory_space=pl.ANY),
                  pl.BlockSpec((None, None, cap, 1), lambda t, b: (b, t // nj, 0, 0))],
        out_specs=pl.BlockSpec((None, None, cap, tn), lambda t, b: (b, t // nj, 0, t % nj)),
        out_shape=jax.ShapeDtypeStruct((bsz, ne, cap, d), BF16),
        scratch_shapes=[pltpu.VMEM((2, 1, ff, tn), F32), pltpu.VMEM((1, ff, tn), BF16), pltpu.SemaphoreType.DMA((2,))],
        compiler_params=_params("arbitrary", "arbitrary"),
        name="expert_down",
    )(hid, wd, gate)


def _router_body(a_ref, w_ref, o_ref):
    lo, hi = _unpack_halves(a_ref[...])
    half = lo.shape[1]
    logits = (jnp.dot(lo, w_ref[0:half, :], preferred_element_type=F32)
              + jnp.dot(hi, w_ref[half:, :], preferred_element_type=F32))
    o_ref[...] = logits.T


def _router(actp, w_r):
    rows, half = actp.shape
    tm = _tile(rows, (512, 256))
    return pl.pallas_call(
        _router_body,
        grid=(rows // tm,),
        in_specs=[pl.BlockSpec((tm, half), lambda i: (i, 0)), pl.BlockSpec((2 * half, HEAD_DIM), lambda i: (0, 0))],
        out_specs=pl.BlockSpec((HEAD_DIM, tm), lambda i: (0, i)),
        out_shape=jax.ShapeDtypeStruct((HEAD_DIM, rows), F32),
        compiler_params=_params("parallel"),
        name="router",
    )(actp, w_r)


def _prefix_excl(mask):
    lanes = 128
    x = jnp.where(mask, 1.0, 0.0)
    r = lax.broadcasted_iota(jnp.int32, (lanes, lanes), 0)
    c = lax.broadcasted_iota(jnp.int32, (lanes, lanes), 1)
    tri = jnp.where(r <= c, 1.0, 0.0).astype(BF16)
    carry = jnp.zeros((x.shape[0], 1), F32)
    out = []
    for t in range(x.shape[1] // lanes):
        xt = x[:, t * lanes:(t + 1) * lanes]
        inc = jnp.dot(xt.astype(BF16), tri, preferred_element_type=F32)
        out.append(inc - xt + carry)
        carry = carry + inc[:, lanes - 1:lanes]
    return jnp.concatenate(out, axis=1)


def _route_select_body(lg_ref, pos_ref, idx_ref, gate_ref, ts_ref, *, cap, chunk, tile):
    lg = lg_ref[...]
    ne, n = lg.shape
    ex = jnp.exp(lg - jnp.max(lg, axis=0, keepdims=True))
    aff = ex / jnp.sum(ex, axis=0, keepdims=True)
    bits = lax.bitcast_convert_type(aff, jnp.int32)

    thr = jnp.zeros((ne, 1), jnp.int32)
    for bit in range(30, -1, -1):
        cand = thr | (1 << bit)
        count = jnp.sum(jnp.where(bits >= cand, 1.0, 0.0), axis=1, keepdims=True)
        thr = jnp.where(count >= cap, cand, thr)
    above = bits > thr
    tied = bits == thr
    need = cap - jnp.sum(jnp.where(above, 1.0, 0.0), axis=1, keepdims=True)
    chosen = above | (tied & (_prefix_excl(tied) < need))
    before = _prefix_excl(chosen)
    pos = jnp.where(chosen, before, -1.0).astype(jnp.int32)
    pos_ref[...] = pos
    for t in range(n // tile):
        ts_ref[:, t:t + 1] = before[:, t * tile:t * tile + 1].astype(jnp.int32)

    tok = lax.broadcasted_iota(jnp.int32, (chunk, n), 1).astype(F32)
    slot = lax.broadcasted_iota(jnp.int32, (chunk, n), 0)
    for e in range(ne):
        for c0 in range(0, cap, chunk):
            hit = pos[e:e + 1, :] == slot + c0
            idx_ref[e, c0:c0 + chunk, :] = jnp.sum(jnp.where(hit, tok, 0.0), axis=1, keepdims=True).astype(jnp.int32)
            gate_ref[e, c0:c0 + chunk, :] = jnp.sum(jnp.where(hit, aff[e:e + 1, :], 0.0), axis=1, keepdims=True)


def _route_select(lt, bsz, n, col_block0):
    ne = N_EXPERTS
    cap = CAPACITY_FACTOR * n // ne
    chunk = min(cap, 128)
    nt = n // ROW_TILE
    return pl.pallas_call(
        functools.partial(_route_select_body, cap=cap, chunk=chunk, tile=ROW_TILE),
        grid=(bsz,),
        in_specs=[pl.BlockSpec((ne, n), lambda b: (0, col_block0 + b))],
        out_specs=[pl.BlockSpec((None, ne, n), lambda b: (b, 0, 0)),
                   pl.BlockSpec((None, ne, cap, 1), lambda b: (b, 0, 0, 0)),
                   pl.BlockSpec((None, ne, cap, 1), lambda b: (b, 0, 0, 0)),
                   pl.BlockSpec((None, ne, nt), lambda b: (b, 0, 0))],
        out_shape=[jax.ShapeDtypeStruct((bsz, ne, n), jnp.int32),
                   jax.ShapeDtypeStruct((bsz, ne, cap, 1), jnp.int32),
                   jax.ShapeDtypeStruct((bsz, ne, cap, 1), F32),
                   jax.ShapeDtypeStruct((bsz, ne, nt), jnp.int32)],
        compiler_params=_params("parallel"),
        name="route_select",
    )(lt)


def _gather_body(idx_ref, src_ref, o_ref, buf_ref, sem):
    b, e = pl.program_id(0), pl.program_id(1)
    cap = buf_ref.shape[0]

    def row_copy(s):
        return pltpu.make_async_copy(src_ref.at[pl.ds(idx_ref[b, e, s], 1)], buf_ref.at[pl.ds(s, 1)], sem)

    def start(s, carry):
        row_copy(s).start()
        return carry

    def wait(s, carry):
        row_copy(s).wait()
        return carry

    unroll = 8 if cap % 8 == 0 else 1
    lax.fori_loop(0, cap, start, 0, unroll=unroll)
    lax.fori_loop(0, cap, wait, 0, unroll=unroll)
    lo, hi = _unpack_halves(buf_ref[...])
    half = lo.shape[1]
    o_ref[:, 0:half] = lo
    o_ref[:, half:] = hi


def _gather(idx, actp):
    bsz, ne, cap = idx.shape
    half = actp.shape[1]
    return pl.pallas_call(
        _gather_body,
        grid_spec=pltpu.PrefetchScalarGridSpec(
            num_scalar_prefetch=1,
            grid=(bsz, ne),
            in_specs=[pl.BlockSpec(memory_space=pl.ANY)],
            out_specs=pl.BlockSpec((None, None, cap, 2 * half), lambda b, e, idx_ref: (b, e, 0, 0)),
            scratch_shapes=[pltpu.VMEM((cap, half), jnp.uint32), pltpu.SemaphoreType.DMA(())]),
        out_shape=jax.ShapeDtypeStruct((bsz, ne, cap, 2 * half), BF16),
        compiler_params=_params("arbitrary", "arbitrary"),
        name="moe_gather",
    )(idx, actp)


def _combine_body(ts_ref, pos_ref, ye_ref, h_ref, gate_ref, g_ref, b_ref, *rest, alpha, window, chunk, tiles,
                  with_next):
    if with_next:
        shift_ref, scale_ref, hn_ref, act_ref, stage_ref, slow_ref, sems, slow_sem = rest
    else:
        hn_ref, stage_ref, slow_ref, sems, slow_sem = rest
    ne, tt = pos_ref.shape
    cap_total = ye_ref.shape[2]
    i = pl.program_id(0)
    buf = i % 2

    def window_start(step, e):
        b = step // tiles
        s = jnp.minimum((ts_ref[b, e, step % tiles] // 16) * 16, cap_total - window)
        return b, pl.multiple_of(s, 16)

    def window_copy(step, e, slot):
        b, s = window_start(step, e)
        return pltpu.make_async_copy(ye_ref.at[b, e, pl.ds(s, window)],
                                     stage_ref.at[slot, pl.ds(e * window, window)], sems.at[slot])

    @pl.when(i == 0)
    def _():
        for e in range(ne):
            window_copy(i, e, buf).start()

    @pl.when(i + 1 < pl.num_programs(0))
    def _():
        for e in range(ne):
            window_copy(i + 1, e, 1 - buf).start()

    for e in range(ne):
        window_copy(i, e, buf).wait()

    pos = pos_ref[...]
    slot_iota = lax.broadcasted_iota(jnp.int32, (window, tt), 0)
    starts, blocks, late = [], [], None
    for e in range(ne):
        starts.append(window_start(i, e)[1])
        rel = pos[e:e + 1, :] - starts[e]
        blocks.append(jnp.where(rel == slot_iota, 1.0, 0.0).astype(BF16))
        late = (rel >= window) if late is None else (late | (rel >= window))
    onehot = jnp.concatenate(blocks, axis=0)
    hn_ref[...] = lax.dot_general(onehot, stage_ref[buf], (((0,), (0,)), ((), ())), preferred_element_type=F32)

    @pl.when(jnp.max(jnp.where(late, 1, 0)) > 0)
    def _():
        chunk_iota = lax.broadcasted_iota(jnp.int32, (chunk, tt), 0)
        for e in range(ne):
            beyond = pos[e:e + 1, :] - starts[e] >= window
            for c0 in range(0, cap_total, chunk):
                copy = pltpu.make_async_copy(ye_ref.at[i // tiles, e, pl.ds(c0, chunk)], slow_ref, slow_sem)
                copy.start()
                copy.wait()
                hit = jnp.where(beyond & (pos[e:e + 1, :] - c0 == chunk_iota), 1.0, 0.0).astype(BF16)
                hn_ref[...] += lax.dot_general(hit, slow_ref[...], (((0,), (0,)), ((), ())),
                                               preferred_element_type=F32)

    sub = min(tt, 128)
    for r in range(tt // sub):
        rs = slice(r * sub, (r + 1) * sub)
        hn = _norm_rows(alpha * h_ref[rs, :] + gate_ref[...] * hn_ref[rs, :]) * g_ref[...] + b_ref[...]
        hn_ref[rs, :] = hn
        if with_next:
            act_ref[rs, :] = (_norm_rows(hn) * (1.0 + scale_ref[...]) + shift_ref[...]).astype(BF16)


def _slow_chunk(cap_total):
    for c in range(min(cap_total, 272) // 16 * 16, 0, -16):
        if cap_total % c == 0:
            return c
    raise ValueError(cap_total)


def _combine_resid(ts, pos, ye, h, mod, next_mod, ln_g, ln_b, bsz, n, ctx_len, alpha):
    ne, cap_total, d = ye.shape[1:]
    tt = ROW_TILE
    nt, ntc = n // tt, ctx_len // tt
    tiles = nt + ntc
    with_next = next_mod is not None
    window = min(COMBINE_WINDOW, cap_total)

    def sample(i):
        return i // tiles

    def row_tile(i):
        b, t = i // tiles, i % tiles
        return jnp.where(t < nt, b * nt + t, bsz * nt + b * ntc + (t - nt))

    def mod_spec(which):
        return pl.BlockSpec((None, None, 1, d),
                            lambda i, ts_ref: (jnp.where(i % tiles < nt, sample(i), bsz), which, 0, 0))

    vec = pl.BlockSpec((1, d), lambda i, ts_ref: (0, 0))
    blk = pl.BlockSpec((tt, d), lambda i, ts_ref: (row_tile(i), 0))
    in_specs = [pl.BlockSpec((None, ne, tt), lambda i, ts_ref: (sample(i), 0, i % tiles)),
                pl.BlockSpec(memory_space=pl.ANY), blk, mod_spec(5), vec, vec]
    args = [pos, ye, h, mod, ln_g.reshape(1, d), ln_b.reshape(1, d)]
    rows = bsz * tiles * tt
    out_specs, out_shape = [blk], [jax.ShapeDtypeStruct((rows, d), F32)]
    if with_next:
        in_specs += [mod_spec(0), mod_spec(1)]
        args += [next_mod, next_mod]
        out_specs.append(blk)
        out_shape.append(jax.ShapeDtypeStruct((rows, d), BF16))
    return pl.pallas_call(
        functools.partial(_combine_body, alpha=alpha, window=window, chunk=_slow_chunk(cap_total), tiles=tiles,
                          with_next=with_next),
        grid_spec=pltpu.PrefetchScalarGridSpec(
            num_scalar_prefetch=1,
            grid=(bsz * tiles,),
            in_specs=in_specs,
            out_specs=out_specs,
            scratch_shapes=[pltpu.VMEM((2, ne * window, d), BF16), pltpu.VMEM((_slow_chunk(cap_total), d), BF16),
                            pltpu.SemaphoreType.DMA((2,)), pltpu.SemaphoreType.DMA(())]),
        out_shape=out_shape,
        compiler_params=_params("arbitrary"),
        name="moe_combine",
    )(ts, *args)


def kernel(x, c, ctx, c_ctx, w_mod, b_mod, w_in, q_gain, k_gain, gm_ln_g, gm_ln_b, w_spatial, b_spatial, b_gates,
           ml_gain, w_out, ln1_g, ln1_b, w_router, w_e_gate, w_e_up, w_e_down, ln2_g, ln2_b):
    dm = _dims()
    bsz, n, d = x.shape
    ctx_len = ctx.shape[1]
    depth = w_mod.shape[0]
    alpha = (2 * depth) ** 0.25
    lat_rows = bsz * n
    main_w = dm["main_w"]
    n_gates = 4 * MLSTM_HEADS
    cap_l = CAPACITY_FACTOR * n // N_EXPERTS
    sample = jnp.arange(bsz, dtype=jnp.int32)[:, None, None]

    cc = jnp.concatenate([c, c_ctx[None], jnp.zeros((MOD_ROWS - bsz - 1, d), F32)], axis=0)
    mods = _mod_vectors(cc, w_mod, b_mod).reshape(depth, MOD_ROWS, N_MOD, 1, d)
    cos, sin = _rope_tables(n, ROW_TILE)
    w_in_t = jnp.swapaxes(w_in, 1, 2)
    w_gates = jnp.swapaxes(lax.optimization_barrier(w_in_t[:, main_w:, :]), 1, 2)
    w_out_b = w_out.astype(BF16)

    h_parts = (x.reshape(lat_rows, d), ctx.reshape(bsz * ctx_len, d))
    act = _ln_mod(*h_parts, mods[0], bsz, n)
    for l in range(depth):
        last = l == depth - 1
        rows = lat_rows if last else lat_rows + bsz * ctx_len
        mod = mods[l]

        p = _in_proj(act, w_in_t, l, main_w)
        w_g = jnp.pad(w_gates[l].astype(BF16), ((0, 0), (0, HEAD_DIM - n_gates)))
        b_g = jnp.pad(b_gates[l], (0, HEAD_DIM - n_gates)).reshape(1, HEAD_DIM)
        gates = _matmul(act, w_g, b_g, "gate_proj")[:, :n_gates]
        gates_seq = jnp.concatenate([gates[lat_rows:].reshape(bsz, ctx_len, n_gates),
                                     gates[:lat_rows].reshape(bsz, n, n_gates)], axis=1)
        gates_rows = jnp.swapaxes(gates_seq, 1, 2)

        qb, kb, vb = _qkv_prep(p, cos, sin, q_gain[l], k_gain[l], bsz, n, ctx_len)
        att = _attention(qb, kb, vb, bsz, n, ctx_len, not last)
        gm = _gmlp(p, gm_ln_g[l], gm_ln_b[l], w_spatial[l], b_spatial[l])
        hf, hb = _mlstm(p, gates_rows, bsz, n, ctx_len)
        ml = _mlstm_out(hf, hb, p, ml_gain[l], rows)
        h, actp = _outproj(att, gm, ml, w_out_b, l, h_parts, mod, ln1_g[l], ln1_b[l], rows, bsz, n, alpha)

        w_r = jnp.pad(w_router[l].astype(BF16), ((0, 0), (0, HEAD_DIM - N_EXPERTS)))
        lt = _router(actp, w_r)
        pos, idx_l, gate, ts = _route_select(lt, bsz, n, 0)
        idx = idx_l[..., 0] + n * sample
        if not last:
            pos_c, idx_c, gate_c, ts_c = _route_select(lt, bsz, ctx_len, lat_rows // ctx_len)
            idx = jnp.concatenate([idx, idx_c[..., 0] + lat_rows + ctx_len * sample], axis=2)
            gate = jnp.concatenate([gate, gate_c], axis=2)
            pos = jnp.concatenate([pos, jnp.where(pos_c >= 0, pos_c + cap_l, -1)], axis=2)
            ts = jnp.concatenate([ts, ts_c + cap_l], axis=2)
        xe = _gather(idx, actp)
        hid = _expert_up(xe, w_e_gate, w_e_up, l)
        ye = _expert_down(hid, w_e_down, gate, l)
        if not last:
            h, act = _combine_resid(ts, pos, ye, h, mod, mods[l + 1], ln2_g[l], ln2_b[l], bsz, n, ctx_len, alpha)
            h_parts = (h, h)
        else:
            h, = _combine_resid(ts, pos, ye, h, mod, None, ln2_g[l], ln2_b[l], bsz, n, 0, alpha)
    return h.reshape(bsz, n, d)
```

```python
import functools

import jax
import jax.numpy as jnp
from jax import lax
from jax.experimental import pallas as pl
from jax.experimental.pallas import tpu as pltpu

D_MODEL = 4096
BATCH = 4
SEQ = 4096
DEPTH = 2
CTX_LEN = 256
GRID_W = 64
HEAD_DIM = 128
ATT_Q_HEADS = 16
ATT_KV_HEADS = 4
ROPE_THETA = 10000.0
GMLP_CHUNK = 128
GMLP_GROUPS = 8
MLSTM_HEADS = 4
MLSTM_CHUNK = 128
N_EXPERTS = 16
CAPACITY_FACTOR = 2
N_MOD = 6
EPS = 1e-6
LOG2_E = 1.4426950408889634

F32 = jnp.float32
BF16 = jnp.bfloat16

ROW_TILE = 256
MOD_ROWS = 8
COMBINE_WINDOW = 64
VMEM_LIMIT = 56 * 1024 * 1024


def _dims():
    d = D_MODEL
    att_w = ATT_Q_HEADS * HEAD_DIM
    kv_w = ATT_KV_HEADS * HEAD_DIM
    gm_w = d // 4
    ml_w = d // 4
    offs, start = {}, 0
    for name, width in (("att_q", att_w), ("att_k", kv_w), ("att_v", kv_w), ("gm_u", gm_w), ("gm_v", gm_w),
                        ("ml_q", ml_w), ("ml_k", ml_w), ("ml_v", ml_w), ("ml_o", ml_w),
                        ("ml_gates", 4 * MLSTM_HEADS)):
        offs[name] = start
        start += width
    return dict(d=d, att_w=att_w, kv_w=kv_w, gm_w=gm_w, ml_w=ml_w, offs=offs, proj_w=start,
                main_w=offs["ml_gates"], ml_dh=ml_w // MLSTM_HEADS, ff=d // 4)


def _tile(n, prefs):
    for t in prefs:
        if n % t == 0:
            return t
    return n


def _params(*sem):
    return pltpu.CompilerParams(dimension_semantics=sem, vmem_limit_bytes=VMEM_LIMIT)


def _sigmoid(x):
    return 1.0 / (1.0 + jnp.exp(-x))


def _norm_rows(x):
    mu = jnp.mean(x, axis=-1, keepdims=True)
    xc = x - mu
    var = jnp.mean(xc * xc, axis=-1, keepdims=True)
    return xc * lax.rsqrt(var + EPS)


def _gelu_tanh(x):
    return 0.5 * x * (1.0 + jnp.tanh(0.7978845608028654 * (x + 0.044715 * (x * x * x))))


def _mod_body(x_ref, w_ref, b_ref, o_ref):
    x = x_ref[...]
    xs = (x * _sigmoid(x)).astype(BF16)
    o_ref[...] = jnp.dot(xs, w_ref[...].astype(BF16), preferred_element_type=F32) + b_ref[...]


def _mod_vectors(cc, w_mod, b_mod):
    depth, d, n = w_mod.shape
    tn = _tile(n, (512, 256, 128))
    return pl.pallas_call(
        _mod_body,
        grid=(depth, n // tn),
        in_specs=[pl.BlockSpec((MOD_ROWS, d), lambda l, j: (0, 0)),
                  pl.BlockSpec((None, d, tn), lambda l, j: (l, 0, j)),
                  pl.BlockSpec((None, 1, tn), lambda l, j: (l, 0, j))],
        out_specs=pl.BlockSpec((None, MOD_ROWS, tn), lambda l, j: (l, 0, j)),
        out_shape=jax.ShapeDtypeStruct((depth, MOD_ROWS, n), F32),
        compiler_params=_params("parallel", "parallel"),
        name="mod_vectors",
    )(cc, w_mod, b_mod.reshape(depth, 1, n))


def _mod_spec(which, tiles_per_sample, bsz):
    d = D_MODEL
    return pl.BlockSpec((None, None, 1, d),
                        lambda i, *_: (jnp.minimum(i // tiles_per_sample, bsz), which, 0, 0))


def _ln_mod_body(h_ref, h_ctx_ref, shift_ref, scale_ref, o_ref, *, lat_tiles):
    h = jnp.where(pl.program_id(0) < lat_tiles, h_ref[...], h_ctx_ref[...])
    o_ref[...] = (_norm_rows(h) * (1.0 + scale_ref[...]) + shift_ref[...]).astype(BF16)


def _ln_mod(h_lat, h_ctx, mod, bsz, n):
    d = h_lat.shape[1]
    rows = h_lat.shape[0] + h_ctx.shape[0]
    tm = ROW_TILE
    tps = n // tm
    return pl.pallas_call(
        functools.partial(_ln_mod_body, lat_tiles=bsz * tps),
        grid=(rows // tm,),
        in_specs=[*_stream_specs((tm, d), bsz * tps), _mod_spec(0, tps, bsz), _mod_spec(1, tps, bsz)],
        out_specs=pl.BlockSpec((tm, d), lambda i: (i, 0)),
        out_shape=jax.ShapeDtypeStruct((rows, d), BF16),
        compiler_params=_params("parallel"),
        name="ln_mod",
    )(h_lat, h_ctx, mod, mod)


def _in_proj_body(a_ref, w_ref, o_ref, wb_ref):
    @pl.when(pl.program_id(1) == 0)
    def _():
        wb_ref[...] = w_ref[...].astype(BF16)

    o_ref[...] = lax.dot_general(a_ref[...], wb_ref[...], (((1,), (1,)), ((), ())), preferred_element_type=F32)


def _in_proj(a, w_in_t, layer, n_out):
    m, k = a.shape
    tm = _tile(m, (1024, 512, 256))
    tn = _tile(n_out, (512, 256, 128))
    return pl.pallas_call(
        _in_proj_body,
        grid=(n_out // tn, m // tm),
        in_specs=[pl.BlockSpec((tm, k), lambda j, i: (i, 0)),
                  pl.BlockSpec((None, tn, k), lambda j, i: (layer, j, 0))],
        out_specs=pl.BlockSpec((tm, tn), lambda j, i: (i, j)),
        out_shape=jax.ShapeDtypeStruct((m, n_out), F32),
        scratch_shapes=[pltpu.VMEM((tn, k), BF16)],
        compiler_params=_params("parallel", "arbitrary"),
        name="in_proj",
    )(a, w_in_t)


def _gate_proj_body(a_ref, w_ref, b_ref, o_ref):
    o_ref[...] = lax.dot_general(w_ref[...].astype(BF16), a_ref[...], (((1,), (1,)), ((), ())),
                                 preferred_element_type=F32) + b_ref[...]


def _gate_proj(a, w_in_t, layer, row0, n_gates, bias):
    m, k = a.shape
    tm = _tile(m, (1024, 512, 256))
    return pl.pallas_call(
        _gate_proj_body,
        grid=(m // tm,),
        in_specs=[pl.BlockSpec((tm, k), lambda i: (i, 0)),
                  pl.BlockSpec((None, n_gates, k), lambda i: (layer, row0 // n_gates, 0)),
                  pl.BlockSpec((n_gates, 1), lambda i: (0, 0))],
        out_specs=pl.BlockSpec((n_gates, tm), lambda i: (0, i)),
        out_shape=jax.ShapeDtypeStruct((n_gates, m), F32),
        compiler_params=_params("parallel"),
        name="gate_proj",
    )(a, w_in_t, bias.reshape(n_gates, 1))


def _rope_tables(n, tm):
    rows = n // GRID_W
    row = jnp.repeat(jnp.arange(rows), GRID_W).astype(F32)
    col = jnp.tile(jnp.arange(GRID_W), rows).astype(F32)
    n_freq = HEAD_DIM // 4
    inv_freq = ROPE_THETA ** (-jnp.arange(n_freq, dtype=F32) / n_freq)
    ang_r, ang_c = row[:, None] * inv_freq, col[:, None] * inv_freq
    cos = jnp.concatenate([jnp.cos(ang_r), jnp.cos(ang_r), jnp.cos(ang_c), jnp.cos(ang_c)], axis=1)
    sin = jnp.concatenate([-jnp.sin(ang_r), jnp.sin(ang_r), -jnp.sin(ang_c), jnp.sin(ang_c)], axis=1)
    cos = jnp.concatenate([cos, jnp.ones((tm, HEAD_DIM), F32)], axis=0)
    sin = jnp.concatenate([sin, jnp.zeros((tm, HEAD_DIM), F32)], axis=0)
    return cos, sin


def _qkv_prep_body(q_ref, k_ref, v_ref, cos_ref, sin_ref, qg_ref, kg_ref, qo_ref, ko_ref, vo_ref):
    cos, sin = cos_ref[...], sin_ref[...]
    lane = lax.broadcasted_iota(jnp.int32, cos.shape, 1)
    first_half = (lane % (HEAD_DIM // 2)) < (HEAD_DIM // 4)

    def norm_rope(x, gain):
        y = x * lax.rsqrt(jnp.mean(x * x, axis=-1, keepdims=True) + EPS) * gain
        partner = jnp.where(first_half, pltpu.roll(y, HEAD_DIM - HEAD_DIM // 4, 1), pltpu.roll(y, HEAD_DIM // 4, 1))
        return y * cos + partner * sin

    qg = qg_ref[...] * (HEAD_DIM ** -0.5 * LOG2_E)
    for h in range(ATT_Q_HEADS):
        sl = slice(h * HEAD_DIM, (h + 1) * HEAD_DIM)
        qo_ref[:, sl] = norm_rope(q_ref[:, sl], qg).astype(BF16)
    for h in range(ATT_KV_HEADS):
        sl = slice(h * HEAD_DIM, (h + 1) * HEAD_DIM)
        ko_ref[:, sl] = norm_rope(k_ref[:, sl], kg_ref[...]).astype(BF16)
        vo_ref[:, 2 * h * HEAD_DIM:(2 * h + 1) * HEAD_DIM] = v_ref[:, sl].astype(BF16)
        vo_ref[:, (2 * h + 1) * HEAD_DIM:(2 * h + 2) * HEAD_DIM] = jnp.where(lane == 0, 1.0, 0.0).astype(BF16)


def _qkv_prep(p, cos, sin, q_gain, k_gain, bsz, n, ctx_len):
    dm = _dims()
    rows = p.shape[0]
    tm = ROW_TILE
    tps, tpc = n // tm, ctx_len // tm
    nlat = bsz * tps
    att_w, kv_w = dm["att_w"], dm["kv_w"]
    ntot = n + ctx_len

    def table_map(i):
        return (jnp.where(i < nlat, i % tps, tps), 0)

    def kv_map(i):
        j = i - nlat
        return (jnp.where(i < nlat, i // tps, j // tpc), jnp.where(i < nlat, tpc + i % tps, j % tpc), 0)

    return pl.pallas_call(
        _qkv_prep_body,
        grid=(rows // tm,),
        in_specs=[pl.BlockSpec((tm, att_w), lambda i: (i, dm["offs"]["att_q"] // att_w)),
                  pl.BlockSpec((tm, kv_w), lambda i: (i, dm["offs"]["att_k"] // kv_w)),
                  pl.BlockSpec((tm, kv_w), lambda i: (i, dm["offs"]["att_v"] // kv_w)),
                  pl.BlockSpec((tm, HEAD_DIM), table_map),
                  pl.BlockSpec((tm, HEAD_DIM), table_map),
                  pl.BlockSpec((1, HEAD_DIM), lambda i: (0, 0)),
                  pl.BlockSpec((1, HEAD_DIM), lambda i: (0, 0))],
        out_specs=[pl.BlockSpec((tm, att_w), lambda i: (i, 0)),
                   pl.BlockSpec((None, tm, kv_w), kv_map),
                   pl.BlockSpec((None, tm, 2 * kv_w), kv_map)],
        out_shape=[jax.ShapeDtypeStruct((rows, att_w), BF16),
                   jax.ShapeDtypeStruct((bsz, ntot, kv_w), BF16),
                   jax.ShapeDtypeStruct((bsz, ntot, 2 * kv_w), BF16)],
        compiler_params=_params("parallel"),
        name="qkv_prep",
    )(p, p, p, cos, sin, q_gain.reshape(1, HEAD_DIM), k_gain.reshape(1, HEAD_DIM))


def _attn_heads(q_ref, k, v, o_ref):
    group = ATT_Q_HEADS // ATT_KV_HEADS

    def scores(h):
        return lax.dot_general(q_ref[:, h * HEAD_DIM:(h + 1) * HEAD_DIM], k, (((1,), (1,)), ((), ())),
                               preferred_element_type=F32)

    s = scores(0)
    for h in range(group):
        s_next = scores(h + 1) if h + 1 < group else None
        e = jnp.exp2(s - jnp.max(s, axis=-1, keepdims=True))
        o = jnp.dot(e.astype(BF16), v, preferred_element_type=F32)
        o_ref[:, h * HEAD_DIM:(h + 1) * HEAD_DIM] = (o[:, 0:HEAD_DIM] / o[:, HEAD_DIM:HEAD_DIM + 1]).astype(BF16)
        s = s_next


def _attn_body(q_ref, k_ref, v_ref, o_ref, *, lat_q_tiles, ctx_len):
    qi = pl.program_id(2)

    @pl.when(qi < lat_q_tiles)
    def _():
        _attn_heads(q_ref, k_ref[...], v_ref[...], o_ref)

    @pl.when(qi >= lat_q_tiles)
    def _():
        _attn_heads(q_ref, k_ref[0:ctx_len, :], v_ref[0:ctx_len, :], o_ref)


def _attention(qb, kb, vb, bsz, n, ctx_len, with_ctx):
    rows_all = qb.shape[0]
    tq = ROW_TILE
    tps, tpc = n // tq, ctx_len // tq
    nlat = bsz * tps
    group_w = (ATT_Q_HEADS // ATT_KV_HEADS) * HEAD_DIM
    ntot = n + ctx_len
    q_tiles = tps + (tpc if with_ctx else 0)
    rows = rows_all if with_ctx else bsz * n

    def q_map(b, g, qi):
        return (jnp.where(qi < tps, b * tps + qi, nlat + b * tpc + (qi - tps)), g)

    return pl.pallas_call(
        functools.partial(_attn_body, lat_q_tiles=tps, ctx_len=ctx_len),
        grid=(bsz, ATT_KV_HEADS, q_tiles),
        in_specs=[pl.BlockSpec((tq, group_w), q_map),
                  pl.BlockSpec((None, ntot, HEAD_DIM), lambda b, g, qi: (b, 0, g)),
                  pl.BlockSpec((None, ntot, 2 * HEAD_DIM), lambda b, g, qi: (b, 0, g))],
        out_specs=pl.BlockSpec((tq, group_w), q_map),
        out_shape=jax.ShapeDtypeStruct((rows, ATT_Q_HEADS * HEAD_DIM), BF16),
        compiler_params=_params("parallel", "parallel", "parallel"),
        name="attention",
    )(qb, kb, vb)


def _gmlp_body(u_ref, v_ref, g_ref, b_ref, ws_ref, bs_ref, o_ref):
    u = _gelu_tanh(u_ref[...])
    v = (_norm_rows(_gelu_tanh(v_ref[...])) * g_ref[...] + b_ref[...]).astype(BF16)
    gd = v.shape[1] // GMLP_GROUPS
    for c in range(v.shape[0] // GMLP_CHUNK):
        rs = slice(c * GMLP_CHUNK, (c + 1) * GMLP_CHUNK)
        for g in range(GMLP_GROUPS):
            cs = slice(g * gd, (g + 1) * gd)
            sv = jnp.dot(ws_ref[g].astype(BF16), v[rs, cs], preferred_element_type=F32) + bs_ref[:, g:g + 1]
            o_ref[rs, cs] = (u[rs, cs] * sv).astype(BF16)


def _gmlp(p, ln_g, ln_b, w_s, b_s):
    dm = _dims()
    rows = p.shape[0]
    tm = ROW_TILE
    gm_w = dm["gm_w"]
    return pl.pallas_call(
        _gmlp_body,
        grid=(rows // tm,),
        in_specs=[pl.BlockSpec((tm, gm_w), lambda i: (i, dm["offs"]["gm_u"] // gm_w)),
                  pl.BlockSpec((tm, gm_w), lambda i: (i, dm["offs"]["gm_v"] // gm_w)),
                  pl.BlockSpec((1, gm_w), lambda i: (0, 0)),
                  pl.BlockSpec((1, gm_w), lambda i: (0, 0)),
                  pl.BlockSpec((GMLP_GROUPS, GMLP_CHUNK, GMLP_CHUNK), lambda i: (0, 0, 0)),
                  pl.BlockSpec((GMLP_CHUNK, GMLP_GROUPS), lambda i: (0, 0))],
        out_specs=pl.BlockSpec((tm, gm_w), lambda i: (i, 0)),
        out_shape=jax.ShapeDtypeStruct((rows, gm_w), BF16),
        compiler_params=_params("parallel"),
        name="gmlp",
    )(p, p, ln_g.reshape(1, gm_w), ln_b.reshape(1, gm_w), w_s, b_s.T)


def _mlstm_chain_step(q_ref, k_ref, v_ref, g_ref, o_ref, c_ref, n_ref, m_ref, direction, head, cols, masks):
    seen, seen_t, eye = masks
    heads = MLSTM_HEADS

    def to_col(x_row):
        return jnp.sum(jnp.where(eye, x_row, 0.0), axis=1, keepdims=True)

    gi = 2 * direction * heads + head
    i_row = g_ref[gi:gi + 1, :]
    f_raw = g_ref[gi + heads:gi + heads + 1, :]
    lf_row = jnp.minimum(f_raw, 0.0) - jnp.log1p(jnp.exp(-jnp.abs(f_raw)))
    lf_col, i_col = to_col(lf_row), to_col(i_row)
    bcum_col = jnp.sum(jnp.where(seen, lf_row, 0.0), axis=1, keepdims=True)
    bcum_row = jnp.sum(jnp.where(seen_t, lf_col, 0.0), axis=0, keepdims=True)
    b_last = jnp.sum(lf_row, axis=1, keepdims=True)

    m_old = m_ref[direction, head]
    g_row = b_last - bcum_row + i_row
    g_col = b_last - bcum_col + i_col
    m_new = jnp.maximum(b_last + m_old, jnp.max(g_row, axis=1, keepdims=True))
    w_prev = jnp.exp(b_last + m_old - m_new)

    q = q_ref[:, cols]
    k = k_ref[:, cols] * ((cols.stop - cols.start) ** -0.5)
    vb = v_ref[:, cols].astype(BF16)
    qb, kb = q.astype(BF16), k.astype(BF16)
    c_old, n_old = c_ref[direction, head], n_ref[direction, head]

    a_col = bcum_col + m_old
    dlog = jnp.where(seen, bcum_col - bcum_row + i_row, -jnp.inf)
    mt = jnp.maximum(a_col, jnp.max(dlog, axis=1, keepdims=True))
    s = lax.dot_general(qb, kb, (((1,), (1,)), ((), ())), preferred_element_type=F32) * jnp.exp(dlog - mt)
    wa = jnp.exp(a_col - mt)
    num = (wa * jnp.dot(qb, c_old.astype(BF16), preferred_element_type=F32)
           + jnp.dot(s.astype(BF16), vb, preferred_element_type=F32))
    den = wa * jnp.sum(q * n_old, axis=1, keepdims=True) + jnp.sum(s, axis=1, keepdims=True)
    o_ref[:, cols] = num / jnp.maximum(jnp.abs(den), jnp.exp(-mt))

    kw = k * jnp.exp(g_col - m_new)
    c_ref[direction, head] = w_prev * c_old + lax.dot_general(kw.astype(BF16), vb, (((0,), (0,)), ((), ())),
                                                              preferred_element_type=F32)
    n_ref[direction, head] = w_prev * n_old + jnp.sum(kw, axis=0, keepdims=True)
    m_ref[direction, head] = m_new


def _mlstm_body(qf_ref, kf_ref, vf_ref, gf_ref, qb_ref, kb_ref, vb_ref, gb_ref, of_ref, ob_ref,
                c_ref, n_ref, m_ref):
    @pl.when(pl.program_id(1) == 0)
    def _():
        c_ref[...] = jnp.zeros_like(c_ref)
        n_ref[...] = jnp.zeros_like(n_ref)
        m_ref[...] = jnp.zeros_like(m_ref)

    chunk = qf_ref.shape[0]
    dh = qf_ref.shape[1] // MLSTM_HEADS
    row = lax.broadcasted_iota(jnp.int32, (chunk, chunk), 0)
    col = lax.broadcasted_iota(jnp.int32, (chunk, chunk), 1)
    eye = row == col
    fwd_masks = (col <= row, row <= col, eye)
    bwd_masks = (col >= row, row >= col, eye)
    for head in range(MLSTM_HEADS):
        cols = slice(head * dh, (head + 1) * dh)
        _mlstm_chain_step(qf_ref, kf_ref, vf_ref, gf_ref, of_ref, c_ref, n_ref, m_ref, 0, head, cols, fwd_masks)
        _mlstm_chain_step(qb_ref, kb_ref, vb_ref, gb_ref, ob_ref, c_ref, n_ref, m_ref, 1, head, cols, bwd_masks)


def _mlstm(p, gates_rows, bsz, n, ctx_len):
    dm = _dims()
    rows = p.shape[0]
    lc = MLSTM_CHUNK
    dh, ml_w = dm["ml_dh"], dm["ml_w"]
    heads = MLSTM_HEADS
    ncc, nlc = ctx_len // lc, n // lc
    steps = ncc + nlc
    lat_chunks = bsz * nlc

    def seq_chunk(dr, j):
        return j if dr == 0 else jnp.where(j < ncc, ncc - 1 - j, ncc + nlc - 1 - (j - ncc))

    def row_chunk(b, dr, j):
        c = seq_chunk(dr, j)
        return jnp.where(c < ncc, lat_chunks + b * ncc + c, b * nlc + (c - ncc))

    def proj_spec(name, dr):
        return pl.BlockSpec((lc, ml_w), lambda b, j: (row_chunk(b, dr, j), dm["offs"][name] // ml_w))

    def dir_specs(dr):
        return [proj_spec("ml_q", dr), proj_spec("ml_k", dr), proj_spec("ml_v", dr),
                pl.BlockSpec((None, 4 * heads, lc), lambda b, j: (b, 0, seq_chunk(dr, j)))]

    return pl.pallas_call(
        _mlstm_body,
        grid=(bsz, steps),
        in_specs=dir_specs(0) + dir_specs(1),
        out_specs=[pl.BlockSpec((lc, ml_w), lambda b, j: (row_chunk(b, 0, j), 0)),
                   pl.BlockSpec((lc, ml_w), lambda b, j: (row_chunk(b, 1, j), 0))],
        out_shape=[jax.ShapeDtypeStruct((rows, ml_w), F32), jax.ShapeDtypeStruct((rows, ml_w), F32)],
        scratch_shapes=[pltpu.VMEM((2, heads, dh, dh), F32), pltpu.VMEM((2, heads, 1, dh), F32),
                        pltpu.VMEM((2, heads, 1, 1), F32)],
        compiler_params=_params("parallel", "arbitrary"),
        name="mlstm_scan",
    )(p, p, p, gates_rows, p, p, p, gates_rows)


def _mlstm_out_body(hf_ref, hb_ref, o_ref, gain_ref, out_ref):
    dh = out_ref.shape[1] // MLSTM_HEADS
    for h in range(MLSTM_HEADS):
        sl = slice(h * dh, (h + 1) * dh)
        hs = hf_ref[:, sl] + hb_ref[:, sl]
        hn = hs * lax.rsqrt(jnp.mean(hs * hs, axis=-1, keepdims=True) + EPS) * gain_ref[:, sl]
        out_ref[:, sl] = (_sigmoid(o_ref[:, sl]) * hn).astype(BF16)


def _mlstm_out(hf, hb, p, gain, rows):
    dm = _dims()
    tm = ROW_TILE
    ml_w = dm["ml_w"]
    blk = pl.BlockSpec((tm, ml_w), lambda i: (i, 0))
    return pl.pallas_call(
        _mlstm_out_body,
        grid=(rows // tm,),
        in_specs=[blk, blk,
                  pl.BlockSpec((tm, ml_w), lambda i: (i, dm["offs"]["ml_o"] // ml_w)),
                  pl.BlockSpec((1, ml_w), lambda i: (0, 0))],
        out_specs=blk,
        out_shape=jax.ShapeDtypeStruct((rows, ml_w), BF16),
        compiler_params=_params("parallel"),
        name="mlstm_out",
    )(hf, hb, p, gain.reshape(1, ml_w))


def _pack_halves(y):
    half = y.shape[1] // 2
    lo = lax.bitcast_convert_type(y[:, :half].astype(BF16).astype(F32), jnp.uint32) >> 16
    hi = lax.bitcast_convert_type(y[:, half:].astype(BF16).astype(F32), jnp.uint32) & jnp.uint32(0xFFFF0000)
    return lo | hi


def _unpack_halves(w):
    lo = lax.bitcast_convert_type(w << 16, F32).astype(BF16)
    hi = lax.bitcast_convert_type(w & jnp.uint32(0xFFFF0000), F32).astype(BF16)
    return lo, hi


def _outproj_body(att_ref, gm_ref, ml_ref, w1_ref, w2_ref, w3_ref, h_ref, h_ctx_ref, gate_ref, g_ref, b_ref,
                  shift_ref, scale_ref, hn_ref, act_ref, *, alpha, tn, sub, lat_tiles):
    j = pl.program_id(1)
    cols = pl.ds(pl.multiple_of(j * tn, tn), tn)
    mix = (jnp.dot(att_ref[...], w1_ref[...], preferred_element_type=F32)
           + jnp.dot(gm_ref[...], w2_ref[...], preferred_element_type=F32)
           + jnp.dot(ml_ref[...], w3_ref[...], preferred_element_type=F32))
    h = jnp.where(pl.program_id(0) < lat_tiles, h_ref[...], h_ctx_ref[...])
    hn_ref[:, cols] = alpha * h + gate_ref[:, cols] * mix

    @pl.when(j == pl.num_programs(1) - 1)
    def _():
        for r in range(hn_ref.shape[0] // sub):
            rs = slice(r * sub, (r + 1) * sub)
            hn = _norm_rows(hn_ref[rs, :]) * g_ref[...] + b_ref[...]
            hn_ref[rs, :] = hn
            act_ref[rs, :] = _pack_halves(_norm_rows(hn) * (1.0 + scale_ref[...]) + shift_ref[...])


def _stream_specs(block, lat_tiles, col=None):
    def lat_map(i, *rest):
        return (jnp.minimum(i, lat_tiles - 1), 0 if col is None else rest[col])

    def ctx_map(i, *rest):
        return (jnp.maximum(i - lat_tiles, 0), 0 if col is None else rest[col])

    return [pl.BlockSpec(block, lat_map), pl.BlockSpec(block, ctx_map)]


def _outproj(att, gm, ml, w_out, layer, h_parts, mod, ln_g, ln_b, rows, bsz, n, alpha):
    dm = _dims()
    d = dm["d"]
    tm = _tile(n, (512, 256))
    tn = _tile(d, (512, 256, 128))
    tps = n // tm
    att_w, gm_w, ml_w = dm["att_w"], dm["gm_w"], dm["ml_w"]
    vec = pl.BlockSpec((1, d), lambda i, j: (0, 0))
    return pl.pallas_call(
        functools.partial(_outproj_body, alpha=alpha, tn=tn, sub=min(tm, 128), lat_tiles=bsz * tps),
        grid=(rows // tm, d // tn),
        in_specs=[pl.BlockSpec((tm, att_w), lambda i, j: (i, 0)),
                  pl.BlockSpec((tm, gm_w), lambda i, j: (i, 0)),
                  pl.BlockSpec((tm, ml_w), lambda i, j: (i, 0)),
                  pl.BlockSpec((None, att_w, tn), lambda i, j: (layer, 0, j)),
                  pl.BlockSpec((None, gm_w, tn), lambda i, j: (layer, att_w // gm_w, j)),
                  pl.BlockSpec((None, ml_w, tn), lambda i, j: (layer, (att_w + gm_w) // ml_w, j)),
                  *_stream_specs((tm, tn), bsz * tps, col=0),
                  _mod_spec(2, tps, bsz), vec, vec, _mod_spec(3, tps, bsz), _mod_spec(4, tps, bsz)],
        out_specs=[pl.BlockSpec((tm, d), lambda i, j: (i, 0)), pl.BlockSpec((tm, d // 2), lambda i, j: (i, 0))],
        out_shape=[jax.ShapeDtypeStruct((rows, d), F32), jax.ShapeDtypeStruct((rows, d // 2), jnp.uint32)],
        compiler_params=_params("parallel", "arbitrary"),
        name="outproj_ln",
    )(att, gm, ml, w_out, w_out, w_out, *h_parts, mod, ln_g.reshape(1, d), ln_b.reshape(1, d), mod, mod)


def _tile_part_copies(w_refs, stage_ref, sems, layer, tile, part, slot, n_inner, parts):
    rows, cols = stage_ref.shape[2] // parts, stage_ref.shape[3]
    rs = pl.ds(pl.multiple_of(part * rows, rows), rows)
    cs = pl.ds(pl.multiple_of((tile % n_inner) * cols, cols), cols)
    return [pltpu.make_async_copy(w.at[layer, tile // n_inner, rs, cs], stage_ref.at[slot, m, rs, :], sems.at[slot])
            for m, w in enumerate(w_refs)]


def _stream_weight_tile(w_refs, stage_ref, wb_ref, sems, layer, n_inner):
    t, b = pl.program_id(0), pl.program_id(1)
    parts = pl.num_programs(1)
    args = (w_refs, stage_ref, sems, layer)

    @pl.when((t == 0) & (b == 0))
    def _():
        for part in range(parts):
            for copy in _tile_part_copies(*args, 0, part, 0, n_inner, parts):
                copy.start()

    @pl.when(t + 1 < pl.num_programs(0))
    def _():
        for copy in _tile_part_copies(*args, t + 1, b, (t + 1) % 2, n_inner, parts):
            copy.start()

    @pl.when(b == 0)
    def _():
        for part in range(parts):
            for copy in _tile_part_copies(*args, t, part, t % 2, n_inner, parts):
                copy.wait()
        wb_ref[...] = stage_ref[t % 2].astype(BF16)


def _expert_up_body(x_ref, wg_ref, wu_ref, o_ref, stage_ref, wb_ref, sems, *, layer, nf):
    _stream_weight_tile((wg_ref, wu_ref), stage_ref, wb_ref, sems, layer, nf)
    group, cap, d = x_ref.shape
    x = x_ref[...].reshape(group * cap, d)
    g = jnp.dot(x, wb_ref[0], preferred_element_type=F32)
    u = jnp.dot(x, wb_ref[1], preferred_element_type=F32)
    o_ref[...] = (g * _sigmoid(g) * u).astype(BF16).reshape(o_ref.shape)


def _sample_group(bsz):
    return 2 if bsz % 2 == 0 else 1


def _expert_up(xe, wg, wu, layer):
    bsz, ne, cap, d = xe.shape
    ff = wg.shape[3]
    tf = _tile(ff, (256, 128))
    nf = ff // tf
    group = _sample_group(bsz)
    return pl.pallas_call(
        functools.partial(_expert_up_body, layer=layer, nf=nf),
        grid=(ne * nf, bsz // group),
        in_specs=[pl.BlockSpec((group, None, cap, d), lambda t, b: (b, t // nf, 0, 0)),
                  pl.BlockSpec(memory_space=pl.ANY), pl.BlockSpec(memory_space=pl.ANY)],
        out_specs=pl.BlockSpec((group, None, cap, tf), lambda t, b: (b, t // nf, 0, t % nf)),
        out_shape=jax.ShapeDtypeStruct((bsz, ne, cap, ff), BF16),
        scratch_shapes=[pltpu.VMEM((2, 2, d, tf), F32), pltpu.VMEM((2, d, tf), BF16), pltpu.SemaphoreType.DMA((2,))],
        compiler_params=_params("arbitrary", "arbitrary"),
        name="expert_up",
    )(xe, wg, wu)


def _expert_down_body(x_ref, w_ref, gate_ref, o_ref, stage_ref, wb_ref, sems, *, layer, nj):
    _stream_weight_tile((w_ref,), stage_ref, wb_ref, sems, layer, nj)
    group, cap, ff = x_ref.shape
    y = jnp.dot(x_ref[...].reshape(group * cap, ff), wb_ref[0], preferred_element_type=F32)
    o_ref[...] = (y * gate_ref[...].reshape(group * cap, 1)).astype(BF16).reshape(o_ref.shape)


def _expert_down(hid, wd, gate, layer):
    bsz, ne, cap, ff = hid.shape
    d = wd.shape[3]
    tn = _tile(d, (2048, 1024, 512))
    nj = d // tn
    group = _sample_group(bsz)
    return pl.pallas_call(
        functools.partial(_expert_down_body, layer=layer, nj=nj),
        grid=(ne * nj, bsz // group),
        in_specs=[pl.BlockSpec((group, None, cap, ff), lambda t, b: (b, t // nj, 0, 0)),
                  pl.BlockSpec(memory_space=pl.ANY),
                  pl.BlockSpec((group, None, cap, 1), lambda t, b: (b, t // nj, 0, 0))],
        out_specs=pl.BlockSpec((group, None, cap, tn), lambda t, b: (b, t // nj, 0, t % nj)),
        out_shape=jax.ShapeDtypeStruct((bsz, ne, cap, d), BF16),
        scratch_shapes=[pltpu.VMEM((2, 1, ff, tn), F32), pltpu.VMEM((1, ff, tn), BF16), pltpu.SemaphoreType.DMA((2,))],
        compiler_params=_params("arbitrary", "arbitrary"),
        name="expert_down",
    )(hid, wd, gate)


def _router_body(a_ref, w_ref, o_ref):
    lo, hi = _unpack_halves(a_ref[...])
    half = lo.shape[1]
    logits = (jnp.dot(lo, w_ref[0:half, :], preferred_element_type=F32)
              + jnp.dot(hi, w_ref[half:, :], preferred_element_type=F32))
    o_ref[...] = logits.T


def _router(actp, w_r):
    rows, half = actp.shape
    tm = _tile(rows, (512, 256))
    return pl.pallas_call(
        _router_body,
        grid=(rows // tm,),
        in_specs=[pl.BlockSpec((tm, half), lambda i: (i, 0)), pl.BlockSpec((2 * half, HEAD_DIM), lambda i: (0, 0))],
        out_specs=pl.BlockSpec((HEAD_DIM, tm), lambda i: (0, i)),
        out_shape=jax.ShapeDtypeStruct((HEAD_DIM, rows), F32),
        compiler_params=_params("parallel"),
        name="router",
    )(actp, w_r)


def _prefix_excl(mask):
    lanes = 128
    x = jnp.where(mask, 1.0, 0.0)
    r = lax.broadcasted_iota(jnp.int32, (lanes, lanes), 0)
    c = lax.broadcasted_iota(jnp.int32, (lanes, lanes), 1)
    tri = jnp.where(r <= c, 1.0, 0.0).astype(BF16)
    carry = jnp.zeros((x.shape[0], 1), F32)
    out = []
    for t in range(x.shape[1] // lanes):
        xt = x[:, t * lanes:(t + 1) * lanes]
        inc = jnp.dot(xt.astype(BF16), tri, preferred_element_type=F32)
        out.append(inc - xt + carry)
        carry = carry + inc[:, lanes - 1:lanes]
    return jnp.concatenate(out, axis=1)


def _route_select_body(lg_ref, pos_ref, idx_ref, gate_ref, ts_ref, *, cap, chunk, tile):
    lg = lg_ref[...]
    ne, n = lg.shape
    ex = jnp.exp(lg - jnp.max(lg, axis=0, keepdims=True))
    aff = ex / jnp.sum(ex, axis=0, keepdims=True)
    bits = lax.bitcast_convert_type(aff, jnp.int32)

    thr = jnp.zeros((ne, 1), jnp.int32)
    for bit in range(30, -1, -1):
        cand = thr | (1 << bit)
        count = jnp.sum(jnp.where(bits >= cand, 1.0, 0.0), axis=1, keepdims=True)
        thr = jnp.where(count >= cap, cand, thr)
    above = bits > thr
    tied = bits == thr
    need = cap - jnp.sum(jnp.where(above, 1.0, 0.0), axis=1, keepdims=True)
    chosen = above | (tied & (_prefix_excl(tied) < need))
    before = _prefix_excl(chosen)
    pos = jnp.where(chosen, before, -1.0).astype(jnp.int32)
    pos_ref[...] = pos
    for t in range(n // tile):
        ts_ref[:, t:t + 1] = before[:, t * tile:t * tile + 1].astype(jnp.int32)

    tok = lax.broadcasted_iota(jnp.int32, (chunk, n), 1).astype(F32)
    slot = lax.broadcasted_iota(jnp.int32, (chunk, n), 0)
    for e in range(ne):
        for c0 in range(0, cap, chunk):
            hit = pos[e:e + 1, :] == slot + c0
            idx_ref[e, c0:c0 + chunk, :] = jnp.sum(jnp.where(hit, tok, 0.0), axis=1, keepdims=True).astype(jnp.int32)
            gate_ref[e, c0:c0 + chunk, :] = jnp.sum(jnp.where(hit, aff[e:e + 1, :], 0.0), axis=1, keepdims=True)


def _route_select(lt, bsz, n, col_block0):
    ne = N_EXPERTS
    cap = CAPACITY_FACTOR * n // ne
    chunk = min(cap, 128)
    nt = n // ROW_TILE
    return pl.pallas_call(
        functools.partial(_route_select_body, cap=cap, chunk=chunk, tile=ROW_TILE),
        grid=(bsz,),
        in_specs=[pl.BlockSpec((ne, n), lambda b: (0, col_block0 + b))],
        out_specs=[pl.BlockSpec((None, ne, n), lambda b: (b, 0, 0)),
                   pl.BlockSpec((None, ne, cap, 1), lambda b: (b, 0, 0, 0)),
                   pl.BlockSpec((None, ne, cap, 1), lambda b: (b, 0, 0, 0)),
                   pl.BlockSpec((None, ne, nt), lambda b: (b, 0, 0))],
        out_shape=[jax.ShapeDtypeStruct((bsz, ne, n), jnp.int32),
                   jax.ShapeDtypeStruct((bsz, ne, cap, 1), jnp.int32),
                   jax.ShapeDtypeStruct((bsz, ne, cap, 1), F32),
                   jax.ShapeDtypeStruct((bsz, ne, nt), jnp.int32)],
        compiler_params=_params("parallel"),
        name="route_select",
    )(lt)


def _gather_body(idx_ref, src_ref, o_ref, buf_ref, sem):
    b, e = pl.program_id(0), pl.program_id(1)
    cap = buf_ref.shape[0]

    def row_copy(s):
        return pltpu.make_async_copy(src_ref.at[pl.ds(idx_ref[b, e, s], 1)], buf_ref.at[pl.ds(s, 1)], sem)

    def start(s, carry):
        row_copy(s).start()
        return carry

    def wait(s, carry):
        row_copy(s).wait()
        return carry

    unroll = 8 if cap % 8 == 0 else 1
    lax.fori_loop(0, cap, start, 0, unroll=unroll)
    lax.fori_loop(0, cap, wait, 0, unroll=unroll)
    lo, hi = _unpack_halves(buf_ref[...])
    half = lo.shape[1]
    o_ref[:, 0:half] = lo
    o_ref[:, half:] = hi


def _gather(idx, actp):
    bsz, ne, cap = idx.shape
    half = actp.shape[1]
    return pl.pallas_call(
        _gather_body,
        grid_spec=pltpu.PrefetchScalarGridSpec(
            num_scalar_prefetch=1,
            grid=(bsz, ne),
            in_specs=[pl.BlockSpec(memory_space=pl.ANY)],
            out_specs=pl.BlockSpec((None, None, cap, 2 * half), lambda b, e, idx_ref: (b, e, 0, 0)),
            scratch_shapes=[pltpu.VMEM((cap, half), jnp.uint32), pltpu.SemaphoreType.DMA(())]),
        out_shape=jax.ShapeDtypeStruct((bsz, ne, cap, 2 * half), BF16),
        compiler_params=_params("arbitrary", "arbitrary"),
        name="moe_gather",
    )(idx, actp)


def _combine_body(ts_ref, pos_ref, ye_ref, h_ref, gate_ref, g_ref, b_ref, *rest, alpha, window, chunk, tiles,
                  with_next):
    if with_next:
        shift_ref, scale_ref, hn_ref, act_ref, stage_ref, slow_ref, sems, slow_sem = rest
    else:
        hn_ref, stage_ref, slow_ref, sems, slow_sem = rest
    ne, tt = pos_ref.shape
    cap_total = ye_ref.shape[2]
    i = pl.program_id(0)
    buf = i % 2

    def window_start(step, e):
        b = step // tiles
        s = jnp.minimum((ts_ref[b, e, step % tiles] // 16) * 16, cap_total - window)
        return b, pl.multiple_of(s, 16)

    def window_copy(step, e, slot):
        b, s = window_start(step, e)
        return pltpu.make_async_copy(ye_ref.at[b, e, pl.ds(s, window)],
                                     stage_ref.at[slot, pl.ds(e * window, window)], sems.at[slot])

    @pl.when(i == 0)
    def _():
        for e in range(ne):
            window_copy(i, e, buf).start()

    @pl.when(i + 1 < pl.num_programs(0))
    def _():
        for e in range(ne):
            window_copy(i + 1, e, 1 - buf).start()

    for e in range(ne):
        window_copy(i, e, buf).wait()

    pos = pos_ref[...]
    slot_iota = lax.broadcasted_iota(jnp.int32, (window, tt), 0)
    starts, blocks, late = [], [], None
    for e in range(ne):
        starts.append(window_start(i, e)[1])
        rel = pos[e:e + 1, :] - starts[e]
        blocks.append(jnp.where(rel == slot_iota, 1.0, 0.0).astype(BF16))
        late = (rel >= window) if late is None else (late | (rel >= window))
    onehot = jnp.concatenate(blocks, axis=0)
    hn_ref[...] = lax.dot_general(onehot, stage_ref[buf], (((0,), (0,)), ((), ())), preferred_element_type=F32)

    @pl.when(jnp.max(jnp.where(late, 1, 0)) > 0)
    def _():
        chunk_iota = lax.broadcasted_iota(jnp.int32, (chunk, tt), 0)
        for e in range(ne):
            beyond = pos[e:e + 1, :] - starts[e] >= window
            for c0 in range(0, cap_total, chunk):
                copy = pltpu.make_async_copy(ye_ref.at[i // tiles, e, pl.ds(c0, chunk)], slow_ref, slow_sem)
                copy.start()
                copy.wait()
                hit = jnp.where(beyond & (pos[e:e + 1, :] - c0 == chunk_iota), 1.0, 0.0).astype(BF16)
                hn_ref[...] += lax.dot_general(hit, slow_ref[...], (((0,), (0,)), ((), ())),
                                               preferred_element_type=F32)

    sub = min(tt, 128)
    for r in range(tt // sub):
        rs = slice(r * sub, (r + 1) * sub)
        hn = _norm_rows(alpha * h_ref[rs, :] + gate_ref[...] * hn_ref[rs, :]) * g_ref[...] + b_ref[...]
        hn_ref[rs, :] = hn
        if with_next:
            act_ref[rs, :] = (_norm_rows(hn) * (1.0 + scale_ref[...]) + shift_ref[...]).astype(BF16)


def _slow_chunk(cap_total):
    for c in range(min(cap_total, 272) // 16 * 16, 0, -16):
        if cap_total % c == 0:
            return c
    raise ValueError(cap_total)


def _combine_resid(ts, pos, ye, h, mod, next_mod, ln_g, ln_b, bsz, n, ctx_len, alpha):
    ne, cap_total, d = ye.shape[1:]
    tt = ROW_TILE
    nt, ntc = n // tt, ctx_len // tt
    tiles = nt + ntc
    with_next = next_mod is not None
    window = min(COMBINE_WINDOW, cap_total)

    def sample(i):
        return i // tiles

    def row_tile(i):
        b, t = i // tiles, i % tiles
        return jnp.where(t < nt, b * nt + t, bsz * nt + b * ntc + (t - nt))

    def mod_spec(which):
        return pl.BlockSpec((None, None, 1, d),
                            lambda i, ts_ref: (jnp.where(i % tiles < nt, sample(i), bsz), which, 0, 0))

    vec = pl.BlockSpec((1, d), lambda i, ts_ref: (0, 0))
    blk = pl.BlockSpec((tt, d), lambda i, ts_ref: (row_tile(i), 0))
    in_specs = [pl.BlockSpec((None, ne, tt), lambda i, ts_ref: (sample(i), 0, i % tiles)),
                pl.BlockSpec(memory_space=pl.ANY), blk, mod_spec(5), vec, vec]
    args = [pos, ye, h, mod, ln_g.reshape(1, d), ln_b.reshape(1, d)]
    rows = bsz * tiles * tt
    out_specs, out_shape = [blk], [jax.ShapeDtypeStruct((rows, d), F32)]
    if with_next:
        in_specs += [mod_spec(0), mod_spec(1)]
        args += [next_mod, next_mod]
        out_specs.append(blk)
        out_shape.append(jax.ShapeDtypeStruct((rows, d), BF16))
    return pl.pallas_call(
        functools.partial(_combine_body, alpha=alpha, window=window, chunk=_slow_chunk(cap_total), tiles=tiles,
                          with_next=with_next),
        grid_spec=pltpu.PrefetchScalarGridSpec(
            num_scalar_prefetch=1,
            grid=(bsz * tiles,),
            in_specs=in_specs,
            out_specs=out_specs,
            scratch_shapes=[pltpu.VMEM((2, ne * window, d), BF16), pltpu.VMEM((_slow_chunk(cap_total), d), BF16),
                            pltpu.SemaphoreType.DMA((2,)), pltpu.SemaphoreType.DMA(())]),
        out_shape=out_shape,
        compiler_params=_params("arbitrary"),
        name="moe_combine",
    )(ts, *args)


def kernel(x, c, ctx, c_ctx, w_mod, b_mod, w_in, q_gain, k_gain, gm_ln_g, gm_ln_b, w_spatial, b_spatial, b_gates,
           ml_gain, w_out, ln1_g, ln1_b, w_router, w_e_gate, w_e_up, w_e_down, ln2_g, ln2_b):
    dm = _dims()
    bsz, n, d = x.shape
    ctx_len = ctx.shape[1]
    depth = w_mod.shape[0]
    alpha = (2 * depth) ** 0.25
    lat_rows = bsz * n
    main_w = dm["main_w"]
    n_gates = 4 * MLSTM_HEADS
    cap_l = CAPACITY_FACTOR * n // N_EXPERTS
    sample = jnp.arange(bsz, dtype=jnp.int32)[:, None, None]

    cc = jnp.concatenate([c, c_ctx[None], jnp.zeros((MOD_ROWS - bsz - 1, d), F32)], axis=0)
    mods = _mod_vectors(cc, w_mod, b_mod).reshape(depth, MOD_ROWS, N_MOD, 1, d)
    cos, sin = _rope_tables(n, ROW_TILE)
    w_in_t = jnp.swapaxes(w_in, 1, 2)
    w_out_b = w_out.astype(BF16)

    h_parts = (x.reshape(lat_rows, d), ctx.reshape(bsz * ctx_len, d))
    act = _ln_mod(*h_parts, mods[0], bsz, n)
    for l in range(depth):
        last = l == depth - 1
        rows = lat_rows if last else lat_rows + bsz * ctx_len
        mod = mods[l]

        p = _in_proj(act, w_in_t, l, main_w)
        gates = _gate_proj(act, w_in_t, l, main_w, n_gates, b_gates[l])
        gates_rows = jnp.swapaxes(jnp.concatenate([gates[:, lat_rows:].reshape(n_gates, bsz, ctx_len),
                                                   gates[:, :lat_rows].reshape(n_gates, bsz, n)], axis=2), 0, 1)

        qb, kb, vb = _qkv_prep(p, cos, sin, q_gain[l], k_gain[l], bsz, n, ctx_len)
        att = _attention(qb, kb, vb, bsz, n, ctx_len, not last)
        gm = _gmlp(p, gm_ln_g[l], gm_ln_b[l], w_spatial[l], b_spatial[l])
        hf, hb = _mlstm(p, gates_rows, bsz, n, ctx_len)
        ml = _mlstm_out(hf, hb, p, ml_gain[l], rows)
        h, actp = _outproj(att, gm, ml, w_out_b, l, h_parts, mod, ln1_g[l], ln1_b[l], rows, bsz, n, alpha)

        w_r = jnp.pad(w_router[l].astype(BF16), ((0, 0), (0, HEAD_DIM - N_EXPERTS)))
        lt = _router(actp, w_r)
        pos, idx_l, gate, ts = _route_select(lt, bsz, n, 0)
        idx = idx_l[..., 0] + n * sample
        if not last:
            pos_c, idx_c, gate_c, ts_c = _route_select(lt, bsz, ctx_len, lat_rows // ctx_len)
            idx = jnp.concatenate([idx, idx_c[..., 0] + lat_rows + ctx_len * sample], axis=2)
            gate = jnp.concatenate([gate, gate_c], axis=2)
            pos = jnp.concatenate([pos, jnp.where(pos_c >= 0, pos_c + cap_l, -1)], axis=2)
            ts = jnp.concatenate([ts, ts_c + cap_l], axis=2)
        xe = _gather(idx, actp)
        hid = _expert_up(xe, w_e_gate, w_e_up, l)
        ye = _expert_down(hid, w_e_down, gate, l)
        if not last:
            h, act = _combine_resid(ts, pos, ye, h, mod, mods[l + 1], ln2_g[l], ln2_b[l], bsz, n, ctx_len, alpha)
            h_parts = (h, h)
        else:
            h, = _combine_resid(ts, pos, ye, h, mod, None, ln2_g[l], ln2_b[l], bsz, n, 0, alpha)
    return h.reshape(bsz, n, d)
```

```python
import functools

import jax
import jax.numpy as jnp
from jax import lax
from jax.experimental import pallas as pl
from jax.experimental.pallas import tpu as pltpu

D_MODEL = 4096
BATCH = 4
SEQ = 4096
DEPTH = 2
CTX_LEN = 256
GRID_W = 64
HEAD_DIM = 128
ATT_Q_HEADS = 16
ATT_KV_HEADS = 4
ROPE_THETA = 10000.0
GMLP_CHUNK = 128
GMLP_GROUPS = 8
MLSTM_HEADS = 4
MLSTM_CHUNK = 128
N_EXPERTS = 16
CAPACITY_FACTOR = 2
N_MOD = 6
EPS = 1e-6
LOG2_E = 1.4426950408889634

F32 = jnp.float32
BF16 = jnp.bfloat16

ROW_TILE = 256
MOD_ROWS = 8
COMBINE_WINDOW = 64
VMEM_LIMIT = 56 * 1024 * 1024


def _dims():
    d = D_MODEL
    att_w = ATT_Q_HEADS * HEAD_DIM
    kv_w = ATT_KV_HEADS * HEAD_DIM
    gm_w = d // 4
    ml_w = d // 4
    offs, start = {}, 0
    for name, width in (("att_q", att_w), ("att_k", kv_w), ("att_v", kv_w), ("gm_u", gm_w), ("gm_v", gm_w),
                        ("ml_q", ml_w), ("ml_k", ml_w), ("ml_v", ml_w), ("ml_o", ml_w),
                        ("ml_gates", 4 * MLSTM_HEADS)):
        offs[name] = start
        start += width
    return dict(d=d, att_w=att_w, kv_w=kv_w, gm_w=gm_w, ml_w=ml_w, offs=offs, proj_w=start,
                main_w=offs["ml_gates"], ml_dh=ml_w // MLSTM_HEADS, ff=d // 4)


def _tile(n, prefs):
    for t in prefs:
        if n % t == 0:
            return t
    return n


def _params(*sem):
    return pltpu.CompilerParams(dimension_semantics=sem, vmem_limit_bytes=VMEM_LIMIT)


def _sigmoid(x):
    return 1.0 / (1.0 + jnp.exp(-x))


def _norm_rows(x):
    mu = jnp.mean(x, axis=-1, keepdims=True)
    xc = x - mu
    var = jnp.mean(xc * xc, axis=-1, keepdims=True)
    return xc * lax.rsqrt(var + EPS)


def _gelu_tanh(x):
    return 0.5 * x * (1.0 + jnp.tanh(0.7978845608028654 * (x + 0.044715 * (x * x * x))))


def _mod_body(x_ref, w_ref, b_ref, o_ref):
    x = x_ref[...]
    xs = (x * _sigmoid(x)).astype(BF16)
    o_ref[...] = jnp.dot(xs, w_ref[...].astype(BF16), preferred_element_type=F32) + b_ref[...]


def _mod_vectors(cc, w_mod, b_mod):
    depth, d, n = w_mod.shape
    tn = _tile(n, (512, 256, 128))
    return pl.pallas_call(
        _mod_body,
        grid=(depth, n // tn),
        in_specs=[pl.BlockSpec((MOD_ROWS, d), lambda l, j: (0, 0)),
                  pl.BlockSpec((None, d, tn), lambda l, j: (l, 0, j)),
                  pl.BlockSpec((None, 1, tn), lambda l, j: (l, 0, j))],
        out_specs=pl.BlockSpec((None, MOD_ROWS, tn), lambda l, j: (l, 0, j)),
        out_shape=jax.ShapeDtypeStruct((depth, MOD_ROWS, n), F32),
        compiler_params=_params("parallel", "parallel"),
        name="mod_vectors",
    )(cc, w_mod, b_mod.reshape(depth, 1, n))


def _mod_spec(which, tiles_per_sample, bsz):
    d = D_MODEL
    return pl.BlockSpec((None, None, 1, d),
                        lambda i, *_: (jnp.minimum(i // tiles_per_sample, bsz), which, 0, 0))


def _ln_mod_body(h_ref, h_ctx_ref, shift_ref, scale_ref, o_ref, *, lat_tiles):
    h = jnp.where(pl.program_id(0) < lat_tiles, h_ref[...], h_ctx_ref[...])
    o_ref[...] = (_norm_rows(h) * (1.0 + scale_ref[...]) + shift_ref[...]).astype(BF16)


def _ln_mod(h_lat, h_ctx, mod, bsz, n):
    d = h_lat.shape[1]
    rows = h_lat.shape[0] + h_ctx.shape[0]
    tm = ROW_TILE
    tps = n // tm
    return pl.pallas_call(
        functools.partial(_ln_mod_body, lat_tiles=bsz * tps),
        grid=(rows // tm,),
        in_specs=[*_stream_specs((tm, d), bsz * tps), _mod_spec(0, tps, bsz), _mod_spec(1, tps, bsz)],
        out_specs=pl.BlockSpec((tm, d), lambda i: (i, 0)),
        out_shape=jax.ShapeDtypeStruct((rows, d), BF16),
        compiler_params=_params("parallel"),
        name="ln_mod",
    )(h_lat, h_ctx, mod, mod)


def _in_proj_body(a_ref, w_ref, o_ref, wb_ref):
    @pl.when(pl.program_id(1) == 0)
    def _():
        wb_ref[...] = w_ref[...].astype(BF16)

    o_ref[...] = lax.dot_general(a_ref[...], wb_ref[...], (((1,), (1,)), ((), ())), preferred_element_type=F32)


def _in_proj(a, w_in_t, layer, n_out):
    m, k = a.shape
    tm = _tile(m, (1024, 512, 256))
    tn = _tile(n_out, (512, 256, 128))
    return pl.pallas_call(
        _in_proj_body,
        grid=(n_out // tn, m // tm),
        in_specs=[pl.BlockSpec((tm, k), lambda j, i: (i, 0)),
                  pl.BlockSpec((None, tn, k), lambda j, i: (layer, j, 0))],
        out_specs=pl.BlockSpec((tm, tn), lambda j, i: (i, j)),
        out_shape=jax.ShapeDtypeStruct((m, n_out), F32),
        scratch_shapes=[pltpu.VMEM((tn, k), BF16)],
        compiler_params=_params("parallel", "arbitrary"),
        name="in_proj",
    )(a, w_in_t)


def _gate_proj_body(a_ref, w_ref, b_ref, o_ref):
    o_ref[...] = lax.dot_general(w_ref[...].astype(BF16), a_ref[...], (((1,), (1,)), ((), ())),
                                 preferred_element_type=F32) + b_ref[...]


def _gate_proj(a, w_in_t, layer, row0, n_gates, bias):
    m, k = a.shape
    tm = _tile(m, (1024, 512, 256))
    return pl.pallas_call(
        _gate_proj_body,
        grid=(m // tm,),
        in_specs=[pl.BlockSpec((tm, k), lambda i: (i, 0)),
                  pl.BlockSpec((None, n_gates, k), lambda i: (layer, row0 // n_gates, 0)),
                  pl.BlockSpec((n_gates, 1), lambda i: (0, 0))],
        out_specs=pl.BlockSpec((n_gates, tm), lambda i: (0, i)),
        out_shape=jax.ShapeDtypeStruct((n_gates, m), F32),
        compiler_params=_params("parallel"),
        name="gate_proj",
    )(a, w_in_t, bias.reshape(n_gates, 1))


def _rope_tables(n, tm):
    rows = n // GRID_W
    row = jnp.repeat(jnp.arange(rows), GRID_W).astype(F32)
    col = jnp.tile(jnp.arange(GRID_W), rows).astype(F32)
    n_freq = HEAD_DIM // 4
    inv_freq = ROPE_THETA ** (-jnp.arange(n_freq, dtype=F32) / n_freq)
    ang_r, ang_c = row[:, None] * inv_freq, col[:, None] * inv_freq
    cos = jnp.concatenate([jnp.cos(ang_r), jnp.cos(ang_r), jnp.cos(ang_c), jnp.cos(ang_c)], axis=1)
    sin = jnp.concatenate([-jnp.sin(ang_r), jnp.sin(ang_r), -jnp.sin(ang_c), jnp.sin(ang_c)], axis=1)
    cos = jnp.concatenate([cos, jnp.ones((tm, HEAD_DIM), F32)], axis=0)
    sin = jnp.concatenate([sin, jnp.zeros((tm, HEAD_DIM), F32)], axis=0)
    return cos, sin


def _qkv_prep_body(q_ref, k_ref, v_ref, cos_ref, sin_ref, qg_ref, kg_ref, qo_ref, ko_ref, vo_ref):
    cos, sin = cos_ref[...], sin_ref[...]
    lane = lax.broadcasted_iota(jnp.int32, cos.shape, 1)
    first_half = (lane % (HEAD_DIM // 2)) < (HEAD_DIM // 4)
    partner_lane = jnp.where(first_half, lane + HEAD_DIM // 4, lane - HEAD_DIM // 4)

    def norm_rope(x, gain):
        y = x * lax.rsqrt(jnp.mean(x * x, axis=-1, keepdims=True) + EPS) * gain
        partner = jnp.take_along_axis(y, partner_lane, axis=1)
        return y * cos + partner * sin

    qg = qg_ref[...] * (HEAD_DIM ** -0.5 * LOG2_E)
    for h in range(ATT_Q_HEADS):
        sl = slice(h * HEAD_DIM, (h + 1) * HEAD_DIM)
        qo_ref[:, sl] = norm_rope(q_ref[:, sl], qg).astype(BF16)
    for h in range(ATT_KV_HEADS):
        sl = slice(h * HEAD_DIM, (h + 1) * HEAD_DIM)
        ko_ref[:, sl] = norm_rope(k_ref[:, sl], kg_ref[...]).astype(BF16)
        vo_ref[:, 2 * h * HEAD_DIM:(2 * h + 1) * HEAD_DIM] = v_ref[:, sl].astype(BF16)
        vo_ref[:, (2 * h + 1) * HEAD_DIM:(2 * h + 2) * HEAD_DIM] = jnp.where(lane == 0, 1.0, 0.0).astype(BF16)


def _qkv_prep(p, cos, sin, q_gain, k_gain, bsz, n, ctx_len):
    dm = _dims()
    rows = p.shape[0]
    tm = ROW_TILE
    tps, tpc = n // tm, ctx_len // tm
    nlat = bsz * tps
    att_w, kv_w = dm["att_w"], dm["kv_w"]
    ntot = n + ctx_len

    def table_map(i):
        return (jnp.where(i < nlat, i % tps, tps), 0)

    def kv_map(i):
        j = i - nlat
        return (jnp.where(i < nlat, i // tps, j // tpc), jnp.where(i < nlat, tpc + i % tps, j % tpc), 0)

    return pl.pallas_call(
        _qkv_prep_body,
        grid=(rows // tm,),
        in_specs=[pl.BlockSpec((tm, att_w), lambda i: (i, dm["offs"]["att_q"] // att_w)),
                  pl.BlockSpec((tm, kv_w), lambda i: (i, dm["offs"]["att_k"] // kv_w)),
                  pl.BlockSpec((tm, kv_w), lambda i: (i, dm["offs"]["att_v"] // kv_w)),
                  pl.BlockSpec((tm, HEAD_DIM), table_map),
                  pl.BlockSpec((tm, HEAD_DIM), table_map),
                  pl.BlockSpec((1, HEAD_DIM), lambda i: (0, 0)),
                  pl.BlockSpec((1, HEAD_DIM), lambda i: (0, 0))],
        out_specs=[pl.BlockSpec((tm, att_w), lambda i: (i, 0)),
                   pl.BlockSpec((None, tm, kv_w), kv_map),
                   pl.BlockSpec((None, tm, 2 * kv_w), kv_map)],
        out_shape=[jax.ShapeDtypeStruct((rows, att_w), BF16),
                   jax.ShapeDtypeStruct((bsz, ntot, kv_w), BF16),
                   jax.ShapeDtypeStruct((bsz, ntot, 2 * kv_w), BF16)],
        compiler_params=_params("parallel"),
        name="qkv_prep",
    )(p, p, p, cos, sin, q_gain.reshape(1, HEAD_DIM), k_gain.reshape(1, HEAD_DIM))


def _attn_heads(q_ref, k, v, o_ref):
    group = ATT_Q_HEADS // ATT_KV_HEADS

    def scores(h):
        return lax.dot_general(q_ref[:, h * HEAD_DIM:(h + 1) * HEAD_DIM], k, (((1,), (1,)), ((), ())),
                               preferred_element_type=F32)

    s = scores(0)
    for h in range(group):
        s_next = scores(h + 1) if h + 1 < group else None
        e = jnp.exp2(s - jnp.max(s, axis=-1, keepdims=True))
        o = jnp.dot(e.astype(BF16), v, preferred_element_type=F32)
        o_ref[:, h * HEAD_DIM:(h + 1) * HEAD_DIM] = (o[:, 0:HEAD_DIM] / o[:, HEAD_DIM:HEAD_DIM + 1]).astype(BF16)
        s = s_next


def _attn_body(q_ref, k_ref, v_ref, o_ref, *, lat_q_tiles, ctx_len):
    qi = pl.program_id(2)

    @pl.when(qi < lat_q_tiles)
    def _():
        _attn_heads(q_ref, k_ref[...], v_ref[...], o_ref)

    @pl.when(qi >= lat_q_tiles)
    def _():
        _attn_heads(q_ref, k_ref[0:ctx_len, :], v_ref[0:ctx_len, :], o_ref)


def _attention(qb, kb, vb, bsz, n, ctx_len, with_ctx):
    rows_all = qb.shape[0]
    tq = ROW_TILE
    tps, tpc = n // tq, ctx_len // tq
    nlat = bsz * tps
    group_w = (ATT_Q_HEADS // ATT_KV_HEADS) * HEAD_DIM
    ntot = n + ctx_len
    q_tiles = tps + (tpc if with_ctx else 0)
    rows = rows_all if with_ctx else bsz * n

    def q_map(b, g, qi):
        return (jnp.where(qi < tps, b * tps + qi, nlat + b * tpc + (qi - tps)), g)

    return pl.pallas_call(
        functools.partial(_attn_body, lat_q_tiles=tps, ctx_len=ctx_len),
        grid=(bsz, ATT_KV_HEADS, q_tiles),
        in_specs=[pl.BlockSpec((tq, group_w), q_map),
                  pl.BlockSpec((None, ntot, HEAD_DIM), lambda b, g, qi: (b, 0, g)),
                  pl.BlockSpec((None, ntot, 2 * HEAD_DIM), lambda b, g, qi: (b, 0, g))],
        out_specs=pl.BlockSpec((tq, group_w), q_map),
        out_shape=jax.ShapeDtypeStruct((rows, ATT_Q_HEADS * HEAD_DIM), BF16),
        compiler_params=_params("parallel", "parallel", "parallel"),
        name="attention",
    )(qb, kb, vb)


def _gmlp_body(u_ref, v_ref, g_ref, b_ref, ws_ref, bs_ref, o_ref):
    u = _gelu_tanh(u_ref[...])
    v = (_norm_rows(_gelu_tanh(v_ref[...])) * g_ref[...] + b_ref[...]).astype(BF16)
    gd = v.shape[1] // GMLP_GROUPS
    for c in range(v.shape[0] // GMLP_CHUNK):
        rs = slice(c * GMLP_CHUNK, (c + 1) * GMLP_CHUNK)
        for g in range(GMLP_GROUPS):
            cs = slice(g * gd, (g + 1) * gd)
            sv = jnp.dot(ws_ref[g].astype(BF16), v[rs, cs], preferred_element_type=F32) + bs_ref[:, g:g + 1]
            o_ref[rs, cs] = (u[rs, cs] * sv).astype(BF16)


def _gmlp(p, ln_g, ln_b, w_s, b_s):
    dm = _dims()
    rows = p.shape[0]
    tm = ROW_TILE
    gm_w = dm["gm_w"]
    return pl.pallas_call(
        _gmlp_body,
        grid=(rows // tm,),
        in_specs=[pl.BlockSpec((tm, gm_w), lambda i: (i, dm["offs"]["gm_u"] // gm_w)),
                  pl.BlockSpec((tm, gm_w), lambda i: (i, dm["offs"]["gm_v"] // gm_w)),
                  pl.BlockSpec((1, gm_w), lambda i: (0, 0)),
                  pl.BlockSpec((1, gm_w), lambda i: (0, 0)),
                  pl.BlockSpec((GMLP_GROUPS, GMLP_CHUNK, GMLP_CHUNK), lambda i: (0, 0, 0)),
                  pl.BlockSpec((GMLP_CHUNK, GMLP_GROUPS), lambda i: (0, 0))],
        out_specs=pl.BlockSpec((tm, gm_w), lambda i: (i, 0)),
        out_shape=jax.ShapeDtypeStruct((rows, gm_w), BF16),
        compiler_params=_params("parallel"),
        name="gmlp",
    )(p, p, ln_g.reshape(1, gm_w), ln_b.reshape(1, gm_w), w_s, b_s.T)


def _mlstm_chain_step(q_ref, k_ref, v_ref, g_ref, o_ref, c_ref, n_ref, m_ref, direction, head, cols, masks):
    seen, seen_t, eye = masks
    heads = MLSTM_HEADS

    def to_col(x_row):
        return jnp.sum(jnp.where(eye, x_row, 0.0), axis=1, keepdims=True)

    gi = 2 * direction * heads + head
    i_row = g_ref[gi:gi + 1, :]
    f_raw = g_ref[gi + heads:gi + heads + 1, :]
    lf_row = jnp.minimum(f_raw, 0.0) - jnp.log1p(jnp.exp(-jnp.abs(f_raw)))
    lf_col, i_col = to_col(lf_row), to_col(i_row)
    bcum_col = jnp.sum(jnp.where(seen, lf_row, 0.0), axis=1, keepdims=True)
    bcum_row = jnp.sum(jnp.where(seen_t, lf_col, 0.0), axis=0, keepdims=True)
    b_last = jnp.sum(lf_row, axis=1, keepdims=True)

    m_old = m_ref[direction, head]
    g_row = b_last - bcum_row + i_row
    g_col = b_last - bcum_col + i_col
    m_new = jnp.maximum(b_last + m_old, jnp.max(g_row, axis=1, keepdims=True))
    w_prev = jnp.exp(b_last + m_old - m_new)

    q = q_ref[:, cols]
    k = k_ref[:, cols] * ((cols.stop - cols.start) ** -0.5)
    vb = v_ref[:, cols].astype(BF16)
    qb, kb = q.astype(BF16), k.astype(BF16)
    c_old, n_old = c_ref[direction, head], n_ref[direction, head]

    a_col = bcum_col + m_old
    dlog = jnp.where(seen, bcum_col - bcum_row + i_row, -jnp.inf)
    mt = jnp.maximum(a_col, jnp.max(dlog, axis=1, keepdims=True))
    s = lax.dot_general(qb, kb, (((1,), (1,)), ((), ())), preferred_element_type=F32) * jnp.exp(dlog - mt)
    wa = jnp.exp(a_col - mt)
    num = (wa * jnp.dot(qb, c_old.astype(BF16), preferred_element_type=F32)
           + jnp.dot(s.astype(BF16), vb, preferred_element_type=F32))
    den = wa * jnp.sum(q * n_old, axis=1, keepdims=True) + jnp.sum(s, axis=1, keepdims=True)
    o_ref[:, cols] = num / jnp.maximum(jnp.abs(den), jnp.exp(-mt))

    kw = k * jnp.exp(g_col - m_new)
    c_ref[direction, head] = w_prev * c_old + lax.dot_general(kw.astype(BF16), vb, (((0,), (0,)), ((), ())),
                                                              preferred_element_type=F32)
    n_ref[direction, head] = w_prev * n_old + jnp.sum(kw, axis=0, keepdims=True)
    m_ref[direction, head] = m_new


def _mlstm_body(qf_ref, kf_ref, vf_ref, gf_ref, qb_ref, kb_ref, vb_ref, gb_ref, of_ref, ob_ref,
                c_ref, n_ref, m_ref):
    @pl.when(pl.program_id(1) == 0)
    def _():
        c_ref[...] = jnp.zeros_like(c_ref)
        n_ref[...] = jnp.zeros_like(n_ref)
        m_ref[...] = jnp.zeros_like(m_ref)

    chunk = qf_ref.shape[0]
    dh = qf_ref.shape[1] // MLSTM_HEADS
    row = lax.broadcasted_iota(jnp.int32, (chunk, chunk), 0)
    col = lax.broadcasted_iota(jnp.int32, (chunk, chunk), 1)
    eye = row == col
    fwd_masks = (col <= row, row <= col, eye)
    bwd_masks = (col >= row, row >= col, eye)
    for head in range(MLSTM_HEADS):
        cols = slice(head * dh, (head + 1) * dh)
        _mlstm_chain_step(qf_ref, kf_ref, vf_ref, gf_ref, of_ref, c_ref, n_ref, m_ref, 0, head, cols, fwd_masks)
        _mlstm_chain_step(qb_ref, kb_ref, vb_ref, gb_ref, ob_ref, c_ref, n_ref, m_ref, 1, head, cols, bwd_masks)


def _mlstm(p, gates_rows, bsz, n, ctx_len):
    dm = _dims()
    rows = p.shape[0]
    lc = MLSTM_CHUNK
    dh, ml_w = dm["ml_dh"], dm["ml_w"]
    heads = MLSTM_HEADS
    ncc, nlc = ctx_len // lc, n // lc
    steps = ncc + nlc
    lat_chunks = bsz * nlc

    def seq_chunk(dr, j):
        return j if dr == 0 else jnp.where(j < ncc, ncc - 1 - j, ncc + nlc - 1 - (j - ncc))

    def row_chunk(b, dr, j):
        c = seq_chunk(dr, j)
        return jnp.where(c < ncc, lat_chunks + b * ncc + c, b * nlc + (c - ncc))

    def proj_spec(name, dr):
        return pl.BlockSpec((lc, ml_w), lambda b, j: (row_chunk(b, dr, j), dm["offs"][name] // ml_w))

    def dir_specs(dr):
        return [proj_spec("ml_q", dr), proj_spec("ml_k", dr), proj_spec("ml_v", dr),
                pl.BlockSpec((None, 4 * heads, lc), lambda b, j: (b, 0, seq_chunk(dr, j)))]

    return pl.pallas_call(
        _mlstm_body,
        grid=(bsz, steps),
        in_specs=dir_specs(0) + dir_specs(1),
        out_specs=[pl.BlockSpec((lc, ml_w), lambda b, j: (row_chunk(b, 0, j), 0)),
                   pl.BlockSpec((lc, ml_w), lambda b, j: (row_chunk(b, 1, j), 0))],
        out_shape=[jax.ShapeDtypeStruct((rows, ml_w), F32), jax.ShapeDtypeStruct((rows, ml_w), F32)],
        scratch_shapes=[pltpu.VMEM((2, heads, dh, dh), F32), pltpu.VMEM((2, heads, 1, dh), F32),
                        pltpu.VMEM((2, heads, 1, 1), F32)],
        compiler_params=_params("parallel", "arbitrary"),
        name="mlstm_scan",
    )(p, p, p, gates_rows, p, p, p, gates_rows)


def _mlstm_out_body(hf_ref, hb_ref, o_ref, gain_ref, out_ref):
    dh = out_ref.shape[1] // MLSTM_HEADS
    for h in range(MLSTM_HEADS):
        sl = slice(h * dh, (h + 1) * dh)
        hs = hf_ref[:, sl] + hb_ref[:, sl]
        hn = hs * lax.rsqrt(jnp.mean(hs * hs, axis=-1, keepdims=True) + EPS) * gain_ref[:, sl]
        out_ref[:, sl] = (_sigmoid(o_ref[:, sl]) * hn).astype(BF16)


def _mlstm_out(hf, hb, p, gain, rows):
    dm = _dims()
    tm = ROW_TILE
    ml_w = dm["ml_w"]
    blk = pl.BlockSpec((tm, ml_w), lambda i: (i, 0))
    return pl.pallas_call(
        _mlstm_out_body,
        grid=(rows // tm,),
        in_specs=[blk, blk,
                  pl.BlockSpec((tm, ml_w), lambda i: (i, dm["offs"]["ml_o"] // ml_w)),
                  pl.BlockSpec((1, ml_w), lambda i: (0, 0))],
        out_specs=blk,
        out_shape=jax.ShapeDtypeStruct((rows, ml_w), BF16),
        compiler_params=_params("parallel"),
        name="mlstm_out",
    )(hf, hb, p, gain.reshape(1, ml_w))


def _pack_halves(y):
    half = y.shape[1] // 2
    lo = lax.bitcast_convert_type(y[:, :half].astype(BF16).astype(F32), jnp.uint32) >> 16
    hi = lax.bitcast_convert_type(y[:, half:].astype(BF16).astype(F32), jnp.uint32) & jnp.uint32(0xFFFF0000)
    return lo | hi


def _unpack_halves(w):
    lo = lax.bitcast_convert_type(w << 16, F32).astype(BF16)
    hi = lax.bitcast_convert_type(w & jnp.uint32(0xFFFF0000), F32).astype(BF16)
    return lo, hi


def _outproj_body(att_ref, gm_ref, ml_ref, w1_ref, w2_ref, w3_ref, h_ref, h_ctx_ref, gate_ref, y_ref, *,
                  alpha, lat_tiles):
    mix = (jnp.dot(att_ref[...], w1_ref[...], preferred_element_type=F32)
           + jnp.dot(gm_ref[...], w2_ref[...], preferred_element_type=F32)
           + jnp.dot(ml_ref[...], w3_ref[...], preferred_element_type=F32))
    h = jnp.where(pl.program_id(0) < lat_tiles, h_ref[...], h_ctx_ref[...])
    y_ref[...] = alpha * h + gate_ref[...] * mix


def _stream_specs(block, lat_tiles, col=None):
    def lat_map(i, *rest):
        return (jnp.minimum(i, lat_tiles - 1), 0 if col is None else rest[col])

    def ctx_map(i, *rest):
        return (jnp.maximum(i - lat_tiles, 0), 0 if col is None else rest[col])

    return [pl.BlockSpec(block, lat_map), pl.BlockSpec(block, ctx_map)]


def _outproj(att, gm, ml, w_out, layer, h_parts, mod, rows, bsz, n, alpha):
    dm = _dims()
    d = dm["d"]
    tm = _tile(n, (1024, 512, 256))
    tn = _tile(d, (512, 256, 128))
    tps = n // tm
    att_w, gm_w, ml_w = dm["att_w"], dm["gm_w"], dm["ml_w"]
    return pl.pallas_call(
        functools.partial(_outproj_body, alpha=alpha, lat_tiles=bsz * tps),
        grid=(rows // tm, d // tn),
        in_specs=[pl.BlockSpec((tm, att_w), lambda i, j: (i, 0)),
                  pl.BlockSpec((tm, gm_w), lambda i, j: (i, 0)),
                  pl.BlockSpec((tm, ml_w), lambda i, j: (i, 0)),
                  pl.BlockSpec((None, att_w, tn), lambda i, j: (layer, 0, j)),
                  pl.BlockSpec((None, gm_w, tn), lambda i, j: (layer, att_w // gm_w, j)),
                  pl.BlockSpec((None, ml_w, tn), lambda i, j: (layer, (att_w + gm_w) // ml_w, j)),
                  *_stream_specs((tm, tn), bsz * tps, col=0),
                  pl.BlockSpec((None, None, 1, tn), lambda i, j: (jnp.minimum(i // tps, bsz), 2, 0, j))],
        out_specs=pl.BlockSpec((tm, tn), lambda i, j: (i, j)),
        out_shape=jax.ShapeDtypeStruct((rows, d), F32),
        compiler_params=_params("parallel", "parallel"),
        name="outproj",
    )(att, gm, ml, w_out, w_out, w_out, *h_parts, mod)


def _post_ln_body(y_ref, g_ref, b_ref, shift_ref, scale_ref, wr_ref, hn_ref, act_ref, lt_ref):
    hn = _norm_rows(y_ref[...]) * g_ref[...] + b_ref[...]
    hn_ref[...] = hn
    act = _norm_rows(hn) * (1.0 + scale_ref[...]) + shift_ref[...]
    act_ref[...] = _pack_halves(act)
    lt_ref[...] = lax.dot_general(wr_ref[...], act.astype(BF16), (((1,), (1,)), ((), ())),
                                  preferred_element_type=F32)


def _post_ln(y, mod, ln_g, ln_b, w_r, bsz, n):
    rows, d = y.shape
    tm = ROW_TILE
    tps = n // tm
    vec = pl.BlockSpec((1, d), lambda i: (0, 0))
    return pl.pallas_call(
        _post_ln_body,
        grid=(rows // tm,),
        in_specs=[pl.BlockSpec((tm, d), lambda i: (i, 0)), vec, vec, _mod_spec(3, tps, bsz), _mod_spec(4, tps, bsz),
                  pl.BlockSpec((HEAD_DIM, d), lambda i: (0, 0))],
        out_specs=[pl.BlockSpec((tm, d), lambda i: (i, 0)), pl.BlockSpec((tm, d // 2), lambda i: (i, 0)),
                   pl.BlockSpec((HEAD_DIM, tm), lambda i: (0, i))],
        out_shape=[jax.ShapeDtypeStruct((rows, d), F32), jax.ShapeDtypeStruct((rows, d // 2), jnp.uint32),
                   jax.ShapeDtypeStruct((HEAD_DIM, rows), F32)],
        compiler_params=_params("parallel"),
        name="post_ln",
    )(y, ln_g.reshape(1, d), ln_b.reshape(1, d), mod, mod, w_r)


def _tile_part_copies(w_refs, stage_ref, sems, layer, tile, part, slot, n_inner, parts):
    rows, cols = stage_ref.shape[2] // parts, stage_ref.shape[3]
    rs = pl.ds(pl.multiple_of(part * rows, rows), rows)
    cs = pl.ds(pl.multiple_of((tile % n_inner) * cols, cols), cols)
    return [pltpu.make_async_copy(w.at[layer, tile // n_inner, rs, cs], stage_ref.at[slot, m, rs, :], sems.at[slot])
            for m, w in enumerate(w_refs)]


def _stream_weight_tile(w_refs, stage_ref, wb_ref, sems, layer, n_inner):
    t, b = pl.program_id(0), pl.program_id(1)
    parts = pl.num_programs(1)
    args = (w_refs, stage_ref, sems, layer)

    @pl.when((t == 0) & (b == 0))
    def _():
        for part in range(parts):
            for copy in _tile_part_copies(*args, 0, part, 0, n_inner, parts):
                copy.start()

    @pl.when(t + 1 < pl.num_programs(0))
    def _():
        for copy in _tile_part_copies(*args, t + 1, b, (t + 1) % 2, n_inner, parts):
            copy.start()

    @pl.when(b == 0)
    def _():
        for part in range(parts):
            for copy in _tile_part_copies(*args, t, part, t % 2, n_inner, parts):
                copy.wait()
        wb_ref[...] = stage_ref[t % 2].astype(BF16)


def _expert_up_body(x_ref, wg_ref, wu_ref, o_ref, stage_ref, wb_ref, sems, *, layer, nf):
    _stream_weight_tile((wg_ref, wu_ref), stage_ref, wb_ref, sems, layer, nf)
    group, cap, d = x_ref.shape
    x = x_ref[...].reshape(group * cap, d)
    g = jnp.dot(x, wb_ref[0], preferred_element_type=F32)
    u = jnp.dot(x, wb_ref[1], preferred_element_type=F32)
    o_ref[...] = (g * _sigmoid(g) * u).astype(BF16).reshape(o_ref.shape)


def _sample_group(bsz):
    return 2 if bsz % 2 == 0 else 1


def _expert_up(xe, wg, wu, layer):
    bsz, ne, cap, d = xe.shape
    ff = wg.shape[3]
    tf = _tile(ff, (256, 128))
    nf = ff // tf
    group = _sample_group(bsz)
    return pl.pallas_call(
        functools.partial(_expert_up_body, layer=layer, nf=nf),
        grid=(ne * nf, bsz // group),
        in_specs=[pl.BlockSpec((group, None, cap, d), lambda t, b: (b, t // nf, 0, 0)),
                  pl.BlockSpec(memory_space=pl.ANY), pl.BlockSpec(memory_space=pl.ANY)],
        out_specs=pl.BlockSpec((group, None, cap, tf), lambda t, b: (b, t // nf, 0, t % nf)),
        out_shape=jax.ShapeDtypeStruct((bsz, ne, cap, ff), BF16),
        scratch_shapes=[pltpu.VMEM((2, 2, d, tf), F32), pltpu.VMEM((2, d, tf), BF16), pltpu.SemaphoreType.DMA((2,))],
        compiler_params=_params("arbitrary", "arbitrary"),
        name="expert_up",
    )(xe, wg, wu)


def _expert_down_body(x_ref, w_ref, gate_ref, o_ref, stage_ref, wb_ref, sems, *, layer, nj):
    _stream_weight_tile((w_ref,), stage_ref, wb_ref, sems, layer, nj)
    group, cap, ff = x_ref.shape
    y = jnp.dot(x_ref[...].reshape(group * cap, ff), wb_ref[0], preferred_element_type=F32)
    o_ref[...] = (y * gate_ref[...].reshape(group * cap, 1)).astype(BF16).reshape(o_ref.shape)


def _expert_down(hid, wd, gate, layer):
    bsz, ne, cap, ff = hid.shape
    d = wd.shape[3]
    tn = _tile(d, (2048, 1024, 512))
    nj = d // tn
    group = _sample_group(bsz)
    return pl.pallas_call(
        functools.partial(_expert_down_body, layer=layer, nj=nj),
        grid=(ne * nj, bsz // group),
        in_specs=[pl.BlockSpec((group, None, cap, ff), lambda t, b: (b, t // nj, 0, 0)),
                  pl.BlockSpec(memory_space=pl.ANY),
                  pl.BlockSpec((group, None, cap, 1), lambda t, b: (b, t // nj, 0, 0))],
        out_specs=pl.BlockSpec((group, None, cap, tn), lambda t, b: (b, t // nj, 0, t % nj)),
        out_shape=jax.ShapeDtypeStruct((bsz, ne, cap, d), BF16),
        scratch_shapes=[pltpu.VMEM((2, 1, ff, tn), F32), pltpu.VMEM((1, ff, tn), BF16), pltpu.SemaphoreType.DMA((2,))],
        compiler_params=_params("arbitrary", "arbitrary"),
        name="expert_down",
    )(hid, wd, gate)


def _prefix_excl(mask):
    lanes = 128
    x = jnp.where(mask, 1.0, 0.0)
    r = lax.broadcasted_iota(jnp.int32, (lanes, lanes), 0)
    c = lax.broadcasted_iota(jnp.int32, (lanes, lanes), 1)
    tri = jnp.where(r <= c, 1.0, 0.0).astype(BF16)
    carry = jnp.zeros((x.shape[0], 1), F32)
    out = []
    for t in range(x.shape[1] // lanes):
        xt = x[:, t * lanes:(t + 1) * lanes]
        inc = jnp.dot(xt.astype(BF16), tri, preferred_element_type=F32)
        out.append(inc - xt + carry)
        carry = carry + inc[:, lanes - 1:lanes]
    return jnp.concatenate(out, axis=1)


def _route_select_body(lg_ref, pos_ref, idx_ref, gate_ref, ts_ref, *, cap, chunk, tile):
    lg = lg_ref[...]
    ne, n = lg.shape
    ex = jnp.exp(lg - jnp.max(lg, axis=0, keepdims=True))
    aff = ex / jnp.sum(ex, axis=0, keepdims=True)
    bits = lax.bitcast_convert_type(aff, jnp.int32)

    thr = jnp.zeros((ne, 1), jnp.int32)
    for bit in range(30, -1, -1):
        cand = thr | (1 << bit)
        count = jnp.sum(jnp.where(bits >= cand, 1.0, 0.0), axis=1, keepdims=True)
        thr = jnp.where(count >= cap, cand, thr)
    above = bits > thr
    tied = bits == thr
    need = cap - jnp.sum(jnp.where(above, 1.0, 0.0), axis=1, keepdims=True)
    chosen = above | (tied & (_prefix_excl(tied) < need))
    before = _prefix_excl(chosen)
    pos = jnp.where(chosen, before, -1.0).astype(jnp.int32)
    pos_ref[...] = pos
    for t in range(n // tile):
        ts_ref[:, t:t + 1] = before[:, t * tile:t * tile + 1].astype(jnp.int32)

    tok = lax.broadcasted_iota(jnp.int32, (chunk, n), 1).astype(F32)
    slot = lax.broadcasted_iota(jnp.int32, (chunk, n), 0)
    for e in range(ne):
        for c0 in range(0, cap, chunk):
            hit = pos[e:e + 1, :] == slot + c0
            idx_ref[e, c0:c0 + chunk, :] = jnp.sum(jnp.where(hit, tok, 0.0), axis=1, keepdims=True).astype(jnp.int32)
            gate_ref[e, c0:c0 + chunk, :] = jnp.sum(jnp.where(hit, aff[e:e + 1, :], 0.0), axis=1, keepdims=True)


def _route_select(lt, bsz, n, col_block0):
    ne = N_EXPERTS
    cap = CAPACITY_FACTOR * n // ne
    chunk = min(cap, 128)
    nt = n // ROW_TILE
    return pl.pallas_call(
        functools.partial(_route_select_body, cap=cap, chunk=chunk, tile=ROW_TILE),
        grid=(bsz,),
        in_specs=[pl.BlockSpec((ne, n), lambda b: (0, col_block0 + b))],
        out_specs=[pl.BlockSpec((None, ne, n), lambda b: (b, 0, 0)),
                   pl.BlockSpec((None, ne, cap, 1), lambda b: (b, 0, 0, 0)),
                   pl.BlockSpec((None, ne, cap, 1), lambda b: (b, 0, 0, 0)),
                   pl.BlockSpec((None, ne, nt), lambda b: (b, 0, 0))],
        out_shape=[jax.ShapeDtypeStruct((bsz, ne, n), jnp.int32),
                   jax.ShapeDtypeStruct((bsz, ne, cap, 1), jnp.int32),
                   jax.ShapeDtypeStruct((bsz, ne, cap, 1), F32),
                   jax.ShapeDtypeStruct((bsz, ne, nt), jnp.int32)],
        compiler_params=_params("parallel"),
        name="route_select",
    )(lt)


def _gather_body(idx_ref, src_ref, o_ref, buf_ref, sem):
    b, e = pl.program_id(0), pl.program_id(1)
    cap = buf_ref.shape[0]

    def row_copy(s):
        return pltpu.make_async_copy(src_ref.at[pl.ds(idx_ref[b, e, s], 1)], buf_ref.at[pl.ds(s, 1)], sem)

    def start(s, carry):
        row_copy(s).start()
        return carry

    def wait(s, carry):
        row_copy(s).wait()
        return carry

    unroll = 8 if cap % 8 == 0 else 1
    lax.fori_loop(0, cap, start, 0, unroll=unroll)
    lax.fori_loop(0, cap, wait, 0, unroll=unroll)
    lo, hi = _unpack_halves(buf_ref[...])
    half = lo.shape[1]
    o_ref[:, 0:half] = lo
    o_ref[:, half:] = hi


def _gather(idx, actp):
    bsz, ne, cap = idx.shape
    half = actp.shape[1]
    return pl.pallas_call(
        _gather_body,
        grid_spec=pltpu.PrefetchScalarGridSpec(
            num_scalar_prefetch=1,
            grid=(bsz, ne),
            in_specs=[pl.BlockSpec(memory_space=pl.ANY)],
            out_specs=pl.BlockSpec((None, None, cap, 2 * half), lambda b, e, idx_ref: (b, e, 0, 0)),
            scratch_shapes=[pltpu.VMEM((cap, half), jnp.uint32), pltpu.SemaphoreType.DMA(())]),
        out_shape=jax.ShapeDtypeStruct((bsz, ne, cap, 2 * half), BF16),
        compiler_params=_params("arbitrary", "arbitrary"),
        name="moe_gather",
    )(idx, actp)


def _combine_body(ts_ref, pos_ref, ye_ref, h_ref, gate_ref, g_ref, b_ref, *rest, alpha, window, chunk, tiles,
                  with_next):
    if with_next:
        shift_ref, scale_ref, hn_ref, act_ref, stage_ref, slow_ref, sems, slow_sem = rest
    else:
        hn_ref, stage_ref, slow_ref, sems, slow_sem = rest
    ne, tt = pos_ref.shape
    cap_total = ye_ref.shape[2]
    i = pl.program_id(0)
    buf = i % 2

    def window_start(step, e):
        b = step // tiles
        s = jnp.minimum((ts_ref[b, e, step % tiles] // 16) * 16, cap_total - window)
        return b, pl.multiple_of(s, 16)

    def window_copy(step, e, slot):
        b, s = window_start(step, e)
        return pltpu.make_async_copy(ye_ref.at[b, e, pl.ds(s, window)],
                                     stage_ref.at[slot, pl.ds(e * window, window)], sems.at[slot])

    @pl.when(i == 0)
    def _():
        for e in range(ne):
            window_copy(i, e, buf).start()

    @pl.when(i + 1 < pl.num_programs(0))
    def _():
        for e in range(ne):
            window_copy(i + 1, e, 1 - buf).start()

    for e in range(ne):
        window_copy(i, e, buf).wait()

    pos = pos_ref[...]
    slot_iota = lax.broadcasted_iota(jnp.int32, (window, tt), 0)
    starts, blocks, late = [], [], None
    for e in range(ne):
        starts.append(window_start(i, e)[1])
        rel = pos[e:e + 1, :] - starts[e]
        blocks.append(jnp.where(rel == slot_iota, 1.0, 0.0).astype(BF16))
        late = (rel >= window) if late is None else (late | (rel >= window))
    onehot = jnp.concatenate(blocks, axis=0)
    hn_ref[...] = lax.dot_general(onehot, stage_ref[buf], (((0,), (0,)), ((), ())), preferred_element_type=F32)

    @pl.when(jnp.max(jnp.where(late, 1, 0)) > 0)
    def _():
        chunk_iota = lax.broadcasted_iota(jnp.int32, (chunk, tt), 0)
        for e in range(ne):
            beyond = pos[e:e + 1, :] - starts[e] >= window
            for c0 in range(0, cap_total, chunk):
                copy = pltpu.make_async_copy(ye_ref.at[i // tiles, e, pl.ds(c0, chunk)], slow_ref, slow_sem)
                copy.start()
                copy.wait()
                hit = jnp.where(beyond & (pos[e:e + 1, :] - c0 == chunk_iota), 1.0, 0.0).astype(BF16)
                hn_ref[...] += lax.dot_general(hit, slow_ref[...], (((0,), (0,)), ((), ())),
                                               preferred_element_type=F32)

    sub = min(tt, 128)
    for r in range(tt // sub):
        rs = slice(r * sub, (r + 1) * sub)
        hn = _norm_rows(alpha * h_ref[rs, :] + gate_ref[...] * hn_ref[rs, :]) * g_ref[...] + b_ref[...]
        hn_ref[rs, :] = hn
        if with_next:
            act_ref[rs, :] = (_norm_rows(hn) * (1.0 + scale_ref[...]) + shift_ref[...]).astype(BF16)


def _slow_chunk(cap_total):
    for c in range(min(cap_total, 272) // 16 * 16, 0, -16):
        if cap_total % c == 0:
            return c
    raise ValueError(cap_total)


def _combine_resid(ts, pos, ye, h, mod, next_mod, ln_g, ln_b, bsz, n, ctx_len, alpha):
    ne, cap_total, d = ye.shape[1:]
    tt = ROW_TILE
    nt, ntc = n // tt, ctx_len // tt
    tiles = nt + ntc
    with_next = next_mod is not None
    window = min(COMBINE_WINDOW, cap_total)

    def sample(i):
        return i // tiles

    def row_tile(i):
        b, t = i // tiles, i % tiles
        return jnp.where(t < nt, b * nt + t, bsz * nt + b * ntc + (t - nt))

    def mod_spec(which):
        return pl.BlockSpec((None, None, 1, d),
                            lambda i, ts_ref: (jnp.where(i % tiles < nt, sample(i), bsz), which, 0, 0))

    vec = pl.BlockSpec((1, d), lambda i, ts_ref: (0, 0))
    blk = pl.BlockSpec((tt, d), lambda i, ts_ref: (row_tile(i), 0))
    in_specs = [pl.BlockSpec((None, ne, tt), lambda i, ts_ref: (sample(i), 0, i % tiles)),
                pl.BlockSpec(memory_space=pl.ANY), blk, mod_spec(5), vec, vec]
    args = [pos, ye, h, mod, ln_g.reshape(1, d), ln_b.reshape(1, d)]
    rows = bsz * tiles * tt
    out_specs, out_shape = [blk], [jax.ShapeDtypeStruct((rows, d), F32)]
    if with_next:
        in_specs += [mod_spec(0), mod_spec(1)]
        args += [next_mod, next_mod]
        out_specs.append(blk)
        out_shape.append(jax.ShapeDtypeStruct((rows, d), BF16))
    return pl.pallas_call(
        functools.partial(_combine_body, alpha=alpha, window=window, chunk=_slow_chunk(cap_total), tiles=tiles,
                          with_next=with_next),
        grid_spec=pltpu.PrefetchScalarGridSpec(
            num_scalar_prefetch=1,
            grid=(bsz * tiles,),
            in_specs=in_specs,
            out_specs=out_specs,
            scratch_shapes=[pltpu.VMEM((2, ne * window, d), BF16), pltpu.VMEM((_slow_chunk(cap_total), d), BF16),
                            pltpu.SemaphoreType.DMA((2,)), pltpu.SemaphoreType.DMA(())]),
        out_shape=out_shape,
        compiler_params=_params("arbitrary"),
        name="moe_combine",
    )(ts, *args)


def kernel(x, c, ctx, c_ctx, w_mod, b_mod, w_in, q_gain, k_gain, gm_ln_g, gm_ln_b, w_spatial, b_spatial, b_gates,
           ml_gain, w_out, ln1_g, ln1_b, w_router, w_e_gate, w_e_up, w_e_down, ln2_g, ln2_b):
    dm = _dims()
    bsz, n, d = x.shape
    ctx_len = ctx.shape[1]
    depth = w_mod.shape[0]
    alpha = (2 * depth) ** 0.25
    lat_rows = bsz * n
    main_w = dm["main_w"]
    n_gates = 4 * MLSTM_HEADS
    cap_l = CAPACITY_FACTOR * n // N_EXPERTS
    sample = jnp.arange(bsz, dtype=jnp.int32)[:, None, None]

    cc = jnp.concatenate([c, c_ctx[None], jnp.zeros((MOD_ROWS - bsz - 1, d), F32)], axis=0)
    mods = _mod_vectors(cc, w_mod, b_mod).reshape(depth, MOD_ROWS, N_MOD, 1, d)
    cos, sin = _rope_tables(n, ROW_TILE)
    w_in_t = jnp.swapaxes(w_in, 1, 2)
    w_out_b = w_out.astype(BF16)

    h_parts = (x.reshape(lat_rows, d), ctx.reshape(bsz * ctx_len, d))
    act = _ln_mod(*h_parts, mods[0], bsz, n)
    for l in range(depth):
        last = l == depth - 1
        rows = lat_rows if last else lat_rows + bsz * ctx_len
        mod = mods[l]

        p = _in_proj(act, w_in_t, l, main_w)
        gates = _gate_proj(act, w_in_t, l, main_w, n_gates, b_gates[l])
        gates_rows = jnp.swapaxes(jnp.concatenate([gates[:, lat_rows:].reshape(n_gates, bsz, ctx_len),
                                                   gates[:, :lat_rows].reshape(n_gates, bsz, n)], axis=2), 0, 1)

        qb, kb, vb = _qkv_prep(p, cos, sin, q_gain[l], k_gain[l], bsz, n, ctx_len)
        att = _attention(qb, kb, vb, bsz, n, ctx_len, not last)
        gm = _gmlp(p, gm_ln_g[l], gm_ln_b[l], w_spatial[l], b_spatial[l])
        hf, hb = _mlstm(p, gates_rows, bsz, n, ctx_len)
        ml = _mlstm_out(hf, hb, p, ml_gain[l], rows)
        y = _outproj(att, gm, ml, w_out_b, l, h_parts, mod, rows, bsz, n, alpha)
        w_r = jnp.pad(w_router[l].T.astype(BF16), ((0, HEAD_DIM - N_EXPERTS), (0, 0)))
        h, actp, lt = _post_ln(y, mod, ln1_g[l], ln1_b[l], w_r, bsz, n)

        pos, idx_l, gate, ts = _route_select(lt, bsz, n, 0)
        idx = idx_l[..., 0] + n * sample
        if not last:
            pos_c, idx_c, gate_c, ts_c = _route_select(lt, bsz, ctx_len, lat_rows // ctx_len)
            idx = jnp.concatenate([idx, idx_c[..., 0] + lat_rows + ctx_len * sample], axis=2)
            gate = jnp.concatenate([gate, gate_c], axis=2)
            pos = jnp.concatenate([pos, jnp.where(pos_c >= 0, pos_c + cap_l, -1)], axis=2)
            ts = jnp.concatenate([ts, ts_c + cap_l], axis=2)
        xe = _gather(idx, actp)
        hid = _expert_up(xe, w_e_gate, w_e_up, l)
        ye = _expert_down(hid, w_e_down, gate, l)
        if not last:
            h, act = _combine_resid(ts, pos, ye, h, mod, mods[l + 1], ln2_g[l], ln2_b[l], bsz, n, ctx_len, alpha)
            h_parts = (h, h)
        else:
            h, = _combine_resid(ts, pos, ye, h, mod, None, ln2_g[l], ln2_b[l], bsz, n, 0, alpha)
    return h.reshape(bsz, n, d)
```

```python
import functools

import jax
import jax.numpy as jnp
from jax import lax
from jax.experimental import pallas as pl
from jax.experimental.pallas import tpu as pltpu

D_MODEL = 4096
BATCH = 4
SEQ = 4096
DEPTH = 2
CTX_LEN = 256
GRID_W = 64
HEAD_DIM = 128
ATT_Q_HEADS = 16
ATT_KV_HEADS = 4
ROPE_THETA = 10000.0
GMLP_CHUNK = 128
GMLP_GROUPS = 8
MLSTM_HEADS = 4
MLSTM_CHUNK = 128
N_EXPERTS = 16
CAPACITY_FACTOR = 2
N_MOD = 6
EPS = 1e-6
LOG2_E = 1.4426950408889634

F32 = jnp.float32
BF16 = jnp.bfloat16

ROW_TILE = 256
MOD_ROWS = 8
COMBINE_WINDOW = 64
VMEM_LIMIT = 56 * 1024 * 1024


def _dims():
    d = D_MODEL
    att_w = ATT_Q_HEADS * HEAD_DIM
    kv_w = ATT_KV_HEADS * HEAD_DIM
    gm_w = d // 4
    ml_w = d // 4
    offs, start = {}, 0
    for name, width in (("att_q", att_w), ("att_k", kv_w), ("att_v", kv_w), ("gm_u", gm_w), ("gm_v", gm_w),
                        ("ml_q", ml_w), ("ml_k", ml_w), ("ml_v", ml_w), ("ml_o", ml_w),
                        ("ml_gates", 4 * MLSTM_HEADS)):
        offs[name] = start
        start += width
    return dict(d=d, att_w=att_w, kv_w=kv_w, gm_w=gm_w, ml_w=ml_w, offs=offs, proj_w=start,
                main_w=offs["ml_gates"], ml_dh=ml_w // MLSTM_HEADS, ff=d // 4)


def _tile(n, prefs):
    for t in prefs:
        if n % t == 0:
            return t
    return n


def _params(*sem):
    return pltpu.CompilerParams(dimension_semantics=sem, vmem_limit_bytes=VMEM_LIMIT)


def _sigmoid(x):
    return 1.0 / (1.0 + jnp.exp(-x))


def _norm_rows(x):
    mu = jnp.mean(x, axis=-1, keepdims=True)
    xc = x - mu
    var = jnp.mean(xc * xc, axis=-1, keepdims=True)
    return xc * lax.rsqrt(var + EPS)


def _gelu_tanh(x):
    return 0.5 * x * (1.0 + jnp.tanh(0.7978845608028654 * (x + 0.044715 * (x * x * x))))


def _mod_body(x_ref, w_ref, b_ref, o_ref):
    x = x_ref[...]
    xs = (x * _sigmoid(x)).astype(BF16)
    o_ref[...] = jnp.dot(xs, w_ref[...].astype(BF16), preferred_element_type=F32) + b_ref[...]


def _mod_vectors(cc, w_mod, b_mod):
    depth, d, n = w_mod.shape
    tn = _tile(n, (512, 256, 128))
    return pl.pallas_call(
        _mod_body,
        grid=(depth, n // tn),
        in_specs=[pl.BlockSpec((MOD_ROWS, d), lambda l, j: (0, 0)),
                  pl.BlockSpec((None, d, tn), lambda l, j: (l, 0, j)),
                  pl.BlockSpec((None, 1, tn), lambda l, j: (l, 0, j))],
        out_specs=pl.BlockSpec((None, MOD_ROWS, tn), lambda l, j: (l, 0, j)),
        out_shape=jax.ShapeDtypeStruct((depth, MOD_ROWS, n), F32),
        compiler_params=_params("parallel", "parallel"),
        name="mod_vectors",
    )(cc, w_mod, b_mod.reshape(depth, 1, n))


def _mod_spec(which, tiles_per_sample, bsz):
    d = D_MODEL
    return pl.BlockSpec((None, None, 1, d),
                        lambda i, *_: (jnp.minimum(i // tiles_per_sample, bsz), which, 0, 0))


def _ln_mod_body(h_ref, h_ctx_ref, shift_ref, scale_ref, o_ref, *, lat_tiles):
    h = jnp.where(pl.program_id(0) < lat_tiles, h_ref[...], h_ctx_ref[...])
    o_ref[...] = (_norm_rows(h) * (1.0 + scale_ref[...]) + shift_ref[...]).astype(BF16)


def _ln_mod(h_lat, h_ctx, mod, bsz, n):
    d = h_lat.shape[1]
    rows = h_lat.shape[0] + h_ctx.shape[0]
    tm = ROW_TILE
    tps = n // tm
    return pl.pallas_call(
        functools.partial(_ln_mod_body, lat_tiles=bsz * tps),
        grid=(rows // tm,),
        in_specs=[*_stream_specs((tm, d), bsz * tps), _mod_spec(0, tps, bsz), _mod_spec(1, tps, bsz)],
        out_specs=pl.BlockSpec((tm, d), lambda i: (i, 0)),
        out_shape=jax.ShapeDtypeStruct((rows, d), BF16),
        compiler_params=_params("parallel"),
        name="ln_mod",
    )(h_lat, h_ctx, mod, mod)


def _in_proj_body(a_ref, w_ref, o_ref):
    o_ref[...] = lax.dot_general(a_ref[...], w_ref[...].astype(BF16), (((1,), (1,)), ((), ())),
                                 preferred_element_type=F32)


def _in_proj(a, w_in_t, layer, n_out):
    m, k = a.shape
    tm = _tile(m, (1024, 512, 256))
    tn = _tile(n_out, (512, 256, 128))
    return pl.pallas_call(
        _in_proj_body,
        grid=(n_out // tn, m // tm),
        in_specs=[pl.BlockSpec((tm, k), lambda j, i: (i, 0)),
                  pl.BlockSpec((None, tn, k), lambda j, i: (layer, j, 0))],
        out_specs=pl.BlockSpec((tm, tn), lambda j, i: (i, j)),
        out_shape=jax.ShapeDtypeStruct((m, n_out), F32),
        compiler_params=_params("parallel", "parallel"),
        name="in_proj",
    )(a, w_in_t)


def _gate_proj_body(a_ref, w_ref, b_ref, o_ref):
    o_ref[...] = lax.dot_general(w_ref[...].astype(BF16), a_ref[...], (((1,), (1,)), ((), ())),
                                 preferred_element_type=F32) + b_ref[...]


def _gate_proj(a, w_in_t, layer, row0, n_gates, bias):
    m, k = a.shape
    tm = _tile(m, (1024, 512, 256))
    return pl.pallas_call(
        _gate_proj_body,
        grid=(m // tm,),
        in_specs=[pl.BlockSpec((tm, k), lambda i: (i, 0)),
                  pl.BlockSpec((None, n_gates, k), lambda i: (layer, row0 // n_gates, 0)),
                  pl.BlockSpec((n_gates, 1), lambda i: (0, 0))],
        out_specs=pl.BlockSpec((n_gates, tm), lambda i: (0, i)),
        out_shape=jax.ShapeDtypeStruct((n_gates, m), F32),
        compiler_params=_params("parallel"),
        name="gate_proj",
    )(a, w_in_t, bias.reshape(n_gates, 1))


def _rope_tables(n, tm):
    rows = n // GRID_W
    row = jnp.repeat(jnp.arange(rows), GRID_W).astype(F32)
    col = jnp.tile(jnp.arange(GRID_W), rows).astype(F32)
    n_freq = HEAD_DIM // 4
    inv_freq = ROPE_THETA ** (-jnp.arange(n_freq, dtype=F32) / n_freq)
    ang_r, ang_c = row[:, None] * inv_freq, col[:, None] * inv_freq
    cos = jnp.concatenate([jnp.cos(ang_r), jnp.cos(ang_r), jnp.cos(ang_c), jnp.cos(ang_c)], axis=1)
    sin = jnp.concatenate([-jnp.sin(ang_r), jnp.sin(ang_r), -jnp.sin(ang_c), jnp.sin(ang_c)], axis=1)
    cos = jnp.concatenate([cos, jnp.ones((tm, HEAD_DIM), F32)], axis=0)
    sin = jnp.concatenate([sin, jnp.zeros((tm, HEAD_DIM), F32)], axis=0)
    return cos, sin


def _qkv_prep_body(q_ref, k_ref, v_ref, cos_ref, sin_ref, qg_ref, kg_ref, qo_ref, ko_ref, vo_ref):
    cos, sin = cos_ref[...], sin_ref[...]
    lane = lax.broadcasted_iota(jnp.int32, cos.shape, 1)
    first_half = (lane % (HEAD_DIM // 2)) < (HEAD_DIM // 4)
    partner_lane = jnp.where(first_half, lane + HEAD_DIM // 4, lane - HEAD_DIM // 4)

    def norm_rope(x, gain):
        y = x * lax.rsqrt(jnp.mean(x * x, axis=-1, keepdims=True) + EPS) * gain
        partner = jnp.take_along_axis(y, partner_lane, axis=1)
        return y * cos + partner * sin

    qg = qg_ref[...] * (HEAD_DIM ** -0.5 * LOG2_E)
    for h in range(ATT_Q_HEADS):
        sl = slice(h * HEAD_DIM, (h + 1) * HEAD_DIM)
        qo_ref[:, sl] = norm_rope(q_ref[:, sl], qg).astype(BF16)
    for h in range(ATT_KV_HEADS):
        sl = slice(h * HEAD_DIM, (h + 1) * HEAD_DIM)
        ko_ref[:, sl] = norm_rope(k_ref[:, sl], kg_ref[...]).astype(BF16)
        vo_ref[:, 2 * h * HEAD_DIM:(2 * h + 1) * HEAD_DIM] = v_ref[:, sl].astype(BF16)
        vo_ref[:, (2 * h + 1) * HEAD_DIM:(2 * h + 2) * HEAD_DIM] = jnp.where(lane == 0, 1.0, 0.0).astype(BF16)


def _qkv_prep(p, cos, sin, q_gain, k_gain, bsz, n, ctx_len):
    dm = _dims()
    rows = p.shape[0]
    tm = ROW_TILE
    tps, tpc = n // tm, ctx_len // tm
    nlat = bsz * tps
    att_w, kv_w = dm["att_w"], dm["kv_w"]
    ntot = n + ctx_len

    def table_map(i):
        return (jnp.where(i < nlat, i % tps, tps), 0)

    def kv_map(i):
        j = i - nlat
        return (jnp.where(i < nlat, i // tps, j // tpc), jnp.where(i < nlat, tpc + i % tps, j % tpc), 0)

    return pl.pallas_call(
        _qkv_prep_body,
        grid=(rows // tm,),
        in_specs=[pl.BlockSpec((tm, att_w), lambda i: (i, dm["offs"]["att_q"] // att_w)),
                  pl.BlockSpec((tm, kv_w), lambda i: (i, dm["offs"]["att_k"] // kv_w)),
                  pl.BlockSpec((tm, kv_w), lambda i: (i, dm["offs"]["att_v"] // kv_w)),
                  pl.BlockSpec((tm, HEAD_DIM), table_map),
                  pl.BlockSpec((tm, HEAD_DIM), table_map),
                  pl.BlockSpec((1, HEAD_DIM), lambda i: (0, 0)),
                  pl.BlockSpec((1, HEAD_DIM), lambda i: (0, 0))],
        out_specs=[pl.BlockSpec((tm, att_w), lambda i: (i, 0)),
                   pl.BlockSpec((None, tm, kv_w), kv_map),
                   pl.BlockSpec((None, tm, 2 * kv_w), kv_map)],
        out_shape=[jax.ShapeDtypeStruct((rows, att_w), BF16),
                   jax.ShapeDtypeStruct((bsz, ntot, kv_w), BF16),
                   jax.ShapeDtypeStruct((bsz, ntot, 2 * kv_w), BF16)],
        compiler_params=_params("parallel"),
        name="qkv_prep",
    )(p, p, p, cos, sin, q_gain.reshape(1, HEAD_DIM), k_gain.reshape(1, HEAD_DIM))


def _attn_heads(q_ref, k, v, o_ref):
    group = ATT_Q_HEADS // ATT_KV_HEADS

    def scores(h):
        return lax.dot_general(q_ref[:, h * HEAD_DIM:(h + 1) * HEAD_DIM], k, (((1,), (1,)), ((), ())),
                               preferred_element_type=F32)

    s = scores(0)
    for h in range(group):
        s_next = scores(h + 1) if h + 1 < group else None
        e = jnp.exp2(s - jnp.max(s, axis=-1, keepdims=True))
        o = jnp.dot(e.astype(BF16), v, preferred_element_type=F32)
        o_ref[:, h * HEAD_DIM:(h + 1) * HEAD_DIM] = (o[:, 0:HEAD_DIM] / o[:, HEAD_DIM:HEAD_DIM + 1]).astype(BF16)
        s = s_next


def _attn_body(q_ref, k_ref, v_ref, o_ref, *, lat_q_tiles, ctx_len):
    qi = pl.program_id(2)

    @pl.when(qi < lat_q_tiles)
    def _():
        _attn_heads(q_ref, k_ref[...], v_ref[...], o_ref)

    @pl.when(qi >= lat_q_tiles)
    def _():
        _attn_heads(q_ref, k_ref[0:ctx_len, :], v_ref[0:ctx_len, :], o_ref)


def _attention(qb, kb, vb, bsz, n, ctx_len, with_ctx):
    rows_all = qb.shape[0]
    tq = ROW_TILE
    tps, tpc = n // tq, ctx_len // tq
    nlat = bsz * tps
    group_w = (ATT_Q_HEADS // ATT_KV_HEADS) * HEAD_DIM
    ntot = n + ctx_len
    q_tiles = tps + (tpc if with_ctx else 0)
    rows = rows_all if with_ctx else bsz * n

    def q_map(b, g, qi):
        return (jnp.where(qi < tps, b * tps + qi, nlat + b * tpc + (qi - tps)), g)

    return pl.pallas_call(
        functools.partial(_attn_body, lat_q_tiles=tps, ctx_len=ctx_len),
        grid=(bsz, ATT_KV_HEADS, q_tiles),
        in_specs=[pl.BlockSpec((tq, group_w), q_map),
                  pl.BlockSpec((None, ntot, HEAD_DIM), lambda b, g, qi: (b, 0, g)),
                  pl.BlockSpec((None, ntot, 2 * HEAD_DIM), lambda b, g, qi: (b, 0, g))],
        out_specs=pl.BlockSpec((tq, group_w), q_map),
        out_shape=jax.ShapeDtypeStruct((rows, ATT_Q_HEADS * HEAD_DIM), BF16),
        compiler_params=_params("parallel", "parallel", "parallel"),
        name="attention",
    )(qb, kb, vb)


def _gmlp_body(u_ref, v_ref, g_ref, b_ref, ws_ref, bs_ref, o_ref):
    u = _gelu_tanh(u_ref[...])
    v = (_norm_rows(_gelu_tanh(v_ref[...])) * g_ref[...] + b_ref[...]).astype(BF16)
    gd = v.shape[1] // GMLP_GROUPS
    for c in range(v.shape[0] // GMLP_CHUNK):
        rs = slice(c * GMLP_CHUNK, (c + 1) * GMLP_CHUNK)
        for g in range(GMLP_GROUPS):
            cs = slice(g * gd, (g + 1) * gd)
            sv = jnp.dot(ws_ref[g].astype(BF16), v[rs, cs], preferred_element_type=F32) + bs_ref[:, g:g + 1]
            o_ref[rs, cs] = (u[rs, cs] * sv).astype(BF16)


def _gmlp(p, ln_g, ln_b, w_s, b_s):
    dm = _dims()
    rows = p.shape[0]
    tm = ROW_TILE
    gm_w = dm["gm_w"]
    return pl.pallas_call(
        _gmlp_body,
        grid=(rows // tm,),
        in_specs=[pl.BlockSpec((tm, gm_w), lambda i: (i, dm["offs"]["gm_u"] // gm_w)),
                  pl.BlockSpec((tm, gm_w), lambda i: (i, dm["offs"]["gm_v"] // gm_w)),
                  pl.BlockSpec((1, gm_w), lambda i: (0, 0)),
                  pl.BlockSpec((1, gm_w), lambda i: (0, 0)),
                  pl.BlockSpec((GMLP_GROUPS, GMLP_CHUNK, GMLP_CHUNK), lambda i: (0, 0, 0)),
                  pl.BlockSpec((GMLP_CHUNK, GMLP_GROUPS), lambda i: (0, 0))],
        out_specs=pl.BlockSpec((tm, gm_w), lambda i: (i, 0)),
        out_shape=jax.ShapeDtypeStruct((rows, gm_w), BF16),
        compiler_params=_params("parallel"),
        name="gmlp",
    )(p, p, ln_g.reshape(1, gm_w), ln_b.reshape(1, gm_w), w_s, b_s.T)


def _mlstm_chain_step(q_ref, k_ref, v_ref, g_ref, o_ref, c_ref, n_ref, m_ref, direction, head, cols, masks):
    seen, seen_t, eye = masks
    heads = MLSTM_HEADS

    def to_col(x_row):
        return jnp.sum(jnp.where(eye, x_row, 0.0), axis=1, keepdims=True)

    gi = 2 * direction * heads + head
    i_row = g_ref[gi:gi + 1, :]
    f_raw = g_ref[gi + heads:gi + heads + 1, :]
    lf_row = jnp.minimum(f_raw, 0.0) - jnp.log1p(jnp.exp(-jnp.abs(f_raw)))
    lf_col, i_col = to_col(lf_row), to_col(i_row)
    bcum_col = jnp.sum(jnp.where(seen, lf_row, 0.0), axis=1, keepdims=True)
    bcum_row = jnp.sum(jnp.where(seen_t, lf_col, 0.0), axis=0, keepdims=True)
    b_last = jnp.sum(lf_row, axis=1, keepdims=True)

    m_old = m_ref[direction, head]
    g_row = b_last - bcum_row + i_row
    g_col = b_last - bcum_col + i_col
    m_new = jnp.maximum(b_last + m_old, jnp.max(g_row, axis=1, keepdims=True))
    w_prev = jnp.exp(b_last + m_old - m_new)

    q = q_ref[:, cols]
    k = k_ref[:, cols] * ((cols.stop - cols.start) ** -0.5)
    vb = v_ref[:, cols].astype(BF16)
    qb, kb = q.astype(BF16), k.astype(BF16)
    c_old, n_old = c_ref[direction, head], n_ref[direction, head]

    a_col = bcum_col + m_old
    dlog = jnp.where(seen, bcum_col - bcum_row + i_row, -jnp.inf)
    mt = jnp.maximum(a_col, jnp.max(dlog, axis=1, keepdims=True))
    s = lax.dot_general(qb, kb, (((1,), (1,)), ((), ())), preferred_element_type=F32) * jnp.exp(dlog - mt)
    wa = jnp.exp(a_col - mt)
    num = (wa * jnp.dot(qb, c_old.astype(BF16), preferred_element_type=F32)
           + jnp.dot(s.astype(BF16), vb, preferred_element_type=F32))
    den = wa * jnp.sum(q * n_old, axis=1, keepdims=True) + jnp.sum(s, axis=1, keepdims=True)
    o_ref[:, cols] = num / jnp.maximum(jnp.abs(den), jnp.exp(-mt))

    kw = k * jnp.exp(g_col - m_new)
    c_ref[direction, head] = w_prev * c_old + lax.dot_general(kw.astype(BF16), vb, (((0,), (0,)), ((), ())),
                                                              preferred_element_type=F32)
    n_ref[direction, head] = w_prev * n_old + jnp.sum(kw, axis=0, keepdims=True)
    m_ref[direction, head] = m_new


def _mlstm_body(qf_ref, kf_ref, vf_ref, gf_ref, qb_ref, kb_ref, vb_ref, gb_ref, of_ref, ob_ref,
                c_ref, n_ref, m_ref):
    @pl.when(pl.program_id(1) == 0)
    def _():
        c_ref[...] = jnp.zeros_like(c_ref)
        n_ref[...] = jnp.zeros_like(n_ref)
        m_ref[...] = jnp.zeros_like(m_ref)

    chunk = qf_ref.shape[0]
    dh = qf_ref.shape[1] // MLSTM_HEADS
    row = lax.broadcasted_iota(jnp.int32, (chunk, chunk), 0)
    col = lax.broadcasted_iota(jnp.int32, (chunk, chunk), 1)
    eye = row == col
    fwd_masks = (col <= row, row <= col, eye)
    bwd_masks = (col >= row, row >= col, eye)
    for head in range(MLSTM_HEADS):
        cols = slice(head * dh, (head + 1) * dh)
        _mlstm_chain_step(qf_ref, kf_ref, vf_ref, gf_ref, of_ref, c_ref, n_ref, m_ref, 0, head, cols, fwd_masks)
        _mlstm_chain_step(qb_ref, kb_ref, vb_ref, gb_ref, ob_ref, c_ref, n_ref, m_ref, 1, head, cols, bwd_masks)


def _mlstm(p, gates_rows, bsz, n, ctx_len):
    dm = _dims()
    rows = p.shape[0]
    lc = MLSTM_CHUNK
    dh, ml_w = dm["ml_dh"], dm["ml_w"]
    heads = MLSTM_HEADS
    ncc, nlc = ctx_len // lc, n // lc
    steps = ncc + nlc
    lat_chunks = bsz * nlc

    def seq_chunk(dr, j):
        return j if dr == 0 else jnp.where(j < ncc, ncc - 1 - j, ncc + nlc - 1 - (j - ncc))

    def row_chunk(b, dr, j):
        c = seq_chunk(dr, j)
        return jnp.where(c < ncc, lat_chunks + b * ncc + c, b * nlc + (c - ncc))

    def proj_spec(name, dr):
        return pl.BlockSpec((lc, ml_w), lambda b, j: (row_chunk(b, dr, j), dm["offs"][name] // ml_w))

    def dir_specs(dr):
        return [proj_spec("ml_q", dr), proj_spec("ml_k", dr), proj_spec("ml_v", dr),
                pl.BlockSpec((None, 4 * heads, lc), lambda b, j: (b, 0, seq_chunk(dr, j)))]

    return pl.pallas_call(
        _mlstm_body,
        grid=(bsz, steps),
        in_specs=dir_specs(0) + dir_specs(1),
        out_specs=[pl.BlockSpec((lc, ml_w), lambda b, j: (row_chunk(b, 0, j), 0)),
                   pl.BlockSpec((lc, ml_w), lambda b, j: (row_chunk(b, 1, j), 0))],
        out_shape=[jax.ShapeDtypeStruct((rows, ml_w), F32), jax.ShapeDtypeStruct((rows, ml_w), F32)],
        scratch_shapes=[pltpu.VMEM((2, heads, dh, dh), F32), pltpu.VMEM((2, heads, 1, dh), F32),
                        pltpu.VMEM((2, heads, 1, 1), F32)],
        compiler_params=_params("parallel", "arbitrary"),
        name="mlstm_scan",
    )(p, p, p, gates_rows, p, p, p, gates_rows)


def _mlstm_out_body(hf_ref, hb_ref, o_ref, gain_ref, out_ref):
    dh = out_ref.shape[1] // MLSTM_HEADS
    for h in range(MLSTM_HEADS):
        sl = slice(h * dh, (h + 1) * dh)
        hs = hf_ref[:, sl] + hb_ref[:, sl]
        hn = hs * lax.rsqrt(jnp.mean(hs * hs, axis=-1, keepdims=True) + EPS) * gain_ref[:, sl]
        out_ref[:, sl] = (_sigmoid(o_ref[:, sl]) * hn).astype(BF16)


def _mlstm_out(hf, hb, p, gain, rows):
    dm = _dims()
    tm = ROW_TILE
    ml_w = dm["ml_w"]
    blk = pl.BlockSpec((tm, ml_w), lambda i: (i, 0))
    return pl.pallas_call(
        _mlstm_out_body,
        grid=(rows // tm,),
        in_specs=[blk, blk,
                  pl.BlockSpec((tm, ml_w), lambda i: (i, dm["offs"]["ml_o"] // ml_w)),
                  pl.BlockSpec((1, ml_w), lambda i: (0, 0))],
        out_specs=blk,
        out_shape=jax.ShapeDtypeStruct((rows, ml_w), BF16),
        compiler_params=_params("parallel"),
        name="mlstm_out",
    )(hf, hb, p, gain.reshape(1, ml_w))


def _pack_halves(y):
    half = y.shape[1] // 2
    lo = lax.bitcast_convert_type(y[:, :half].astype(BF16).astype(F32), jnp.uint32) >> 16
    hi = lax.bitcast_convert_type(y[:, half:].astype(BF16).astype(F32), jnp.uint32) & jnp.uint32(0xFFFF0000)
    return lo | hi


def _unpack_halves(w):
    lo = lax.bitcast_convert_type(w << 16, F32).astype(BF16)
    hi = lax.bitcast_convert_type(w & jnp.uint32(0xFFFF0000), F32).astype(BF16)
    return lo, hi


def _outproj_body(att_ref, gm_ref, ml_ref, w1_ref, w2_ref, w3_ref, h_ref, h_ctx_ref, gate_ref, y_ref, *,
                  alpha, lat_tiles):
    mix = (jnp.dot(att_ref[...], w1_ref[...], preferred_element_type=F32)
           + jnp.dot(gm_ref[...], w2_ref[...], preferred_element_type=F32)
           + jnp.dot(ml_ref[...], w3_ref[...], preferred_element_type=F32))
    h = jnp.where(pl.program_id(0) < lat_tiles, h_ref[...], h_ctx_ref[...])
    y_ref[...] = alpha * h + gate_ref[...] * mix


def _stream_specs(block, lat_tiles, col=None):
    def lat_map(i, *rest):
        return (jnp.minimum(i, lat_tiles - 1), 0 if col is None else rest[col])

    def ctx_map(i, *rest):
        return (jnp.maximum(i - lat_tiles, 0), 0 if col is None else rest[col])

    return [pl.BlockSpec(block, lat_map), pl.BlockSpec(block, ctx_map)]


def _outproj(att, gm, ml, w_out, layer, h_parts, mod, rows, bsz, n, alpha):
    dm = _dims()
    d = dm["d"]
    tm = _tile(n, (1024, 512, 256))
    tn = _tile(d, (512, 256, 128))
    tps = n // tm
    att_w, gm_w, ml_w = dm["att_w"], dm["gm_w"], dm["ml_w"]
    return pl.pallas_call(
        functools.partial(_outproj_body, alpha=alpha, lat_tiles=bsz * tps),
        grid=(rows // tm, d // tn),
        in_specs=[pl.BlockSpec((tm, att_w), lambda i, j: (i, 0)),
                  pl.BlockSpec((tm, gm_w), lambda i, j: (i, 0)),
                  pl.BlockSpec((tm, ml_w), lambda i, j: (i, 0)),
                  pl.BlockSpec((None, att_w, tn), lambda i, j: (layer, 0, j)),
                  pl.BlockSpec((None, gm_w, tn), lambda i, j: (layer, att_w // gm_w, j)),
                  pl.BlockSpec((None, ml_w, tn), lambda i, j: (layer, (att_w + gm_w) // ml_w, j)),
                  *_stream_specs((tm, tn), bsz * tps, col=0),
                  pl.BlockSpec((None, None, 1, tn), lambda i, j: (jnp.minimum(i // tps, bsz), 2, 0, j))],
        out_specs=pl.BlockSpec((tm, tn), lambda i, j: (i, j)),
        out_shape=jax.ShapeDtypeStruct((rows, d), F32),
        compiler_params=_params("parallel", "parallel"),
        name="outproj",
    )(att, gm, ml, w_out, w_out, w_out, *h_parts, mod)


def _post_ln_body(y_ref, g_ref, b_ref, shift_ref, scale_ref, wr_ref, hn_ref, act_ref, lt_ref):
    hn = _norm_rows(y_ref[...]) * g_ref[...] + b_ref[...]
    hn_ref[...] = hn
    act = _norm_rows(hn) * (1.0 + scale_ref[...]) + shift_ref[...]
    act_ref[...] = _pack_halves(act)
    lt_ref[...] = lax.dot_general(wr_ref[...], act.astype(BF16), (((1,), (1,)), ((), ())),
                                  preferred_element_type=F32)


def _post_ln(y, mod, ln_g, ln_b, w_r, bsz, n):
    rows, d = y.shape
    tm = ROW_TILE
    tps = n // tm
    vec = pl.BlockSpec((1, d), lambda i: (0, 0))
    return pl.pallas_call(
        _post_ln_body,
        grid=(rows // tm,),
        in_specs=[pl.BlockSpec((tm, d), lambda i: (i, 0)), vec, vec, _mod_spec(3, tps, bsz), _mod_spec(4, tps, bsz),
                  pl.BlockSpec((HEAD_DIM, d), lambda i: (0, 0))],
        out_specs=[pl.BlockSpec((tm, d), lambda i: (i, 0)), pl.BlockSpec((tm, d // 2), lambda i: (i, 0)),
                   pl.BlockSpec((HEAD_DIM, tm), lambda i: (0, i))],
        out_shape=[jax.ShapeDtypeStruct((rows, d), F32), jax.ShapeDtypeStruct((rows, d // 2), jnp.uint32),
                   jax.ShapeDtypeStruct((HEAD_DIM, rows), F32)],
        compiler_params=_params("parallel"),
        name="post_ln",
    )(y, ln_g.reshape(1, d), ln_b.reshape(1, d), mod, mod, w_r)


def _tile_part_copies(w_refs, stage_ref, sems, layer, tile, part, slot, n_inner, parts):
    rows, cols = stage_ref.shape[2] // parts, stage_ref.shape[3]
    rs = pl.ds(pl.multiple_of(part * rows, rows), rows)
    cs = pl.ds(pl.multiple_of((tile % n_inner) * cols, cols), cols)
    return [pltpu.make_async_copy(w.at[layer, tile // n_inner, rs, cs], stage_ref.at[slot, m, rs, :], sems.at[slot])
            for m, w in enumerate(w_refs)]


def _stream_weight_tile(w_refs, stage_ref, sems, layer, n_inner):
    t, b = pl.program_id(0), pl.program_id(1)
    parts = pl.num_programs(1)
    args = (w_refs, stage_ref, sems, layer)

    @pl.when((t == 0) & (b == 0))
    def _():
        for part in range(parts):
            for copy in _tile_part_copies(*args, 0, part, 0, n_inner, parts):
                copy.start()

    @pl.when(t + 1 < pl.num_programs(0))
    def _():
        for copy in _tile_part_copies(*args, t + 1, b, (t + 1) % 2, n_inner, parts):
            copy.start()

    @pl.when(b == 0)
    def _():
        for part in range(parts):
            for copy in _tile_part_copies(*args, t, part, t % 2, n_inner, parts):
                copy.wait()

    return t % 2


def _expert_up_body(x_ref, wg_ref, wu_ref, o_ref, stage_ref, sems, *, layer, nf):
    slot = _stream_weight_tile((wg_ref, wu_ref), stage_ref, sems, layer, nf)
    group, cap, d = x_ref.shape
    x = x_ref[...].reshape(group * cap, d)
    g = jnp.dot(x, stage_ref[slot, 0].astype(BF16), preferred_element_type=F32)
    u = jnp.dot(x, stage_ref[slot, 1].astype(BF16), preferred_element_type=F32)
    o_ref[...] = (g * _sigmoid(g) * u).astype(BF16).reshape(o_ref.shape)


def _sample_group(bsz):
    return 2 if bsz % 2 == 0 else 1


def _expert_up(xe, wg, wu, layer):
    bsz, ne, cap, d = xe.shape
    ff = wg.shape[3]
    tf = _tile(ff, (256, 128))
    nf = ff // tf
    group = _sample_group(bsz)
    return pl.pallas_call(
        functools.partial(_expert_up_body, layer=layer, nf=nf),
        grid=(ne * nf, bsz // group),
        in_specs=[pl.BlockSpec((group, None, cap, d), lambda t, b: (b, t // nf, 0, 0)),
                  pl.BlockSpec(memory_space=pl.ANY), pl.BlockSpec(memory_space=pl.ANY)],
        out_specs=pl.BlockSpec((group, None, cap, tf), lambda t, b: (b, t // nf, 0, t % nf)),
        out_shape=jax.ShapeDtypeStruct((bsz, ne, cap, ff), BF16),
        scratch_shapes=[pltpu.VMEM((2, 2, d, tf), F32), pltpu.SemaphoreType.DMA((2,))],
        compiler_params=_params("arbitrary", "arbitrary"),
        name="expert_up",
    )(xe, wg, wu)


def _expert_down_body(x_ref, w_ref, gate_ref, o_ref, stage_ref, sems, *, layer, nj):
    slot = _stream_weight_tile((w_ref,), stage_ref, sems, layer, nj)
    group, cap, ff = x_ref.shape
    y = jnp.dot(x_ref[...].reshape(group * cap, ff), stage_ref[slot, 0].astype(BF16), preferred_element_type=F32)
    o_ref[...] = (y * gate_ref[...].reshape(group * cap, 1)).astype(BF16).reshape(o_ref.shape)


def _expert_down(hid, wd, gate, layer):
    bsz, ne, cap, ff = hid.shape
    d = wd.shape[3]
    tn = _tile(d, (2048, 1024, 512))
    nj = d // tn
    group = _sample_group(bsz)
    return pl.pallas_call(
        functools.partial(_expert_down_body, layer=layer, nj=nj),
        grid=(ne * nj, bsz // group),
        in_specs=[pl.BlockSpec((group, None, cap, ff), lambda t, b: (b, t // nj, 0, 0)),
                  pl.BlockSpec(memory_space=pl.ANY),
                  pl.BlockSpec((group, None, cap, 1), lambda t, b: (b, t // nj, 0, 0))],
        out_specs=pl.BlockSpec((group, None, cap, tn), lambda t, b: (b, t // nj, 0, t % nj)),
        out_shape=jax.ShapeDtypeStruct((bsz, ne, cap, d), BF16),
        scratch_shapes=[pltpu.VMEM((2, 1, ff, tn), F32), pltpu.SemaphoreType.DMA((2,))],
        compiler_params=_params("arbitrary", "arbitrary"),
        name="expert_down",
    )(hid, wd, gate)


def _prefix_excl(mask):
    lanes = 128
    x = jnp.where(mask, 1.0, 0.0)
    r = lax.broadcasted_iota(jnp.int32, (lanes, lanes), 0)
    c = lax.broadcasted_iota(jnp.int32, (lanes, lanes), 1)
    tri = jnp.where(r <= c, 1.0, 0.0).astype(BF16)
    carry = jnp.zeros((x.shape[0], 1), F32)
    out = []
    for t in range(x.shape[1] // lanes):
        xt = x[:, t * lanes:(t + 1) * lanes]
        inc = jnp.dot(xt.astype(BF16), tri, preferred_element_type=F32)
        out.append(inc - xt + carry)
        carry = carry + inc[:, lanes - 1:lanes]
    return jnp.concatenate(out, axis=1)


def _route_select_body(lg_ref, pos_ref, idx_ref, gate_ref, ts_ref, *, cap, chunk, tile):
    lg = lg_ref[...]
    ne, n = lg.shape
    ex = jnp.exp(lg - jnp.max(lg, axis=0, keepdims=True))
    aff = ex / jnp.sum(ex, axis=0, keepdims=True)
    bits = lax.bitcast_convert_type(aff, jnp.int32)

    thr = jnp.zeros((ne, 1), jnp.int32)
    for bit in range(30, -1, -1):
        cand = thr | (1 << bit)
        count = jnp.sum(jnp.where(bits >= cand, 1.0, 0.0), axis=1, keepdims=True)
        thr = jnp.where(count >= cap, cand, thr)
    above = bits > thr
    tied = bits == thr
    need = cap - jnp.sum(jnp.where(above, 1.0, 0.0), axis=1, keepdims=True)
    chosen = above | (tied & (_prefix_excl(tied) < need))
    before = _prefix_excl(chosen)
    pos = jnp.where(chosen, before, -1.0).astype(jnp.int32)
    pos_ref[...] = pos
    for t in range(n // tile):
        ts_ref[:, t:t + 1] = before[:, t * tile:t * tile + 1].astype(jnp.int32)

    tok = lax.broadcasted_iota(jnp.int32, (chunk, n), 1).astype(F32)
    slot = lax.broadcasted_iota(jnp.int32, (chunk, n), 0)
    for e in range(ne):
        for c0 in range(0, cap, chunk):
            hit = pos[e:e + 1, :] == slot + c0
            idx_ref[e, c0:c0 + chunk, :] = jnp.sum(jnp.where(hit, tok, 0.0), axis=1, keepdims=True).astype(jnp.int32)
            gate_ref[e, c0:c0 + chunk, :] = jnp.sum(jnp.where(hit, aff[e:e + 1, :], 0.0), axis=1, keepdims=True)


def _route_select(lt, bsz, n, col_block0):
    ne = N_EXPERTS
    cap = CAPACITY_FACTOR * n // ne
    chunk = min(cap, 128)
    nt = n // ROW_TILE
    return pl.pallas_call(
        functools.partial(_route_select_body, cap=cap, chunk=chunk, tile=ROW_TILE),
        grid=(bsz,),
        in_specs=[pl.BlockSpec((ne, n), lambda b: (0, col_block0 + b))],
        out_specs=[pl.BlockSpec((None, ne, n), lambda b: (b, 0, 0)),
                   pl.BlockSpec((None, ne, cap, 1), lambda b: (b, 0, 0, 0)),
                   pl.BlockSpec((None, ne, cap, 1), lambda b: (b, 0, 0, 0)),
                   pl.BlockSpec((None, ne, nt), lambda b: (b, 0, 0))],
        out_shape=[jax.ShapeDtypeStruct((bsz, ne, n), jnp.int32),
                   jax.ShapeDtypeStruct((bsz, ne, cap, 1), jnp.int32),
                   jax.ShapeDtypeStruct((bsz, ne, cap, 1), F32),
                   jax.ShapeDtypeStruct((bsz, ne, nt), jnp.int32)],
        compiler_params=_params("parallel"),
        name="route_select",
    )(lt)


def _gather_body(idx_ref, src_ref, o_ref, buf_ref, sem):
    b, e = pl.program_id(0), pl.program_id(1)
    cap = buf_ref.shape[0]

    def row_copy(s):
        return pltpu.make_async_copy(src_ref.at[pl.ds(idx_ref[b, e, s], 1)], buf_ref.at[pl.ds(s, 1)], sem)

    def start(s, carry):
        row_copy(s).start()
        return carry

    def wait(s, carry):
        row_copy(s).wait()
        return carry

    unroll = 8 if cap % 8 == 0 else 1
    lax.fori_loop(0, cap, start, 0, unroll=unroll)
    lax.fori_loop(0, cap, wait, 0, unroll=unroll)
    lo, hi = _unpack_halves(buf_ref[...])
    half = lo.shape[1]
    o_ref[:, 0:half] = lo
    o_ref[:, half:] = hi


def _gather(idx, actp):
    bsz, ne, cap = idx.shape
    half = actp.shape[1]
    return pl.pallas_call(
        _gather_body,
        grid_spec=pltpu.PrefetchScalarGridSpec(
            num_scalar_prefetch=1,
            grid=(bsz, ne),
            in_specs=[pl.BlockSpec(memory_space=pl.ANY)],
            out_specs=pl.BlockSpec((None, None, cap, 2 * half), lambda b, e, idx_ref: (b, e, 0, 0)),
            scratch_shapes=[pltpu.VMEM((cap, half), jnp.uint32), pltpu.SemaphoreType.DMA(())]),
        out_shape=jax.ShapeDtypeStruct((bsz, ne, cap, 2 * half), BF16),
        compiler_params=_params("arbitrary", "arbitrary"),
        name="moe_gather",
    )(idx, actp)


def _combine_body(ts_ref, pos_ref, ye_ref, h_ref, gate_ref, g_ref, b_ref, *rest, alpha, window, chunk, tiles,
                  with_next):
    if with_next:
        shift_ref, scale_ref, hn_ref, act_ref, stage_ref, slow_ref, sems, slow_sem = rest
    else:
        hn_ref, stage_ref, slow_ref, sems, slow_sem = rest
    ne, tt = pos_ref.shape
    cap_total = ye_ref.shape[2]
    i = pl.program_id(0)
    buf = i % 2

    def window_start(step, e):
        b = step // tiles
        s = jnp.minimum((ts_ref[b, e, step % tiles] // 16) * 16, cap_total - window)
        return b, pl.multiple_of(s, 16)

    def window_copy(step, e, slot):
        b, s = window_start(step, e)
        return pltpu.make_async_copy(ye_ref.at[b, e, pl.ds(s, window)],
                                     stage_ref.at[slot, pl.ds(e * window, window)], sems.at[slot])

    @pl.when(i == 0)
    def _():
        for e in range(ne):
            window_copy(i, e, buf).start()

    @pl.when(i + 1 < pl.num_programs(0))
    def _():
        for e in range(ne):
            window_copy(i + 1, e, 1 - buf).start()

    for e in range(ne):
        window_copy(i, e, buf).wait()

    pos = pos_ref[...]
    slot_iota = lax.broadcasted_iota(jnp.int32, (window, tt), 0)
    starts, blocks, late = [], [], None
    for e in range(ne):
        starts.append(window_start(i, e)[1])
        rel = pos[e:e + 1, :] - starts[e]
        blocks.append(jnp.where(rel == slot_iota, 1.0, 0.0).astype(BF16))
        late = (rel >= window) if late is None else (late | (rel >= window))
    onehot = jnp.concatenate(blocks, axis=0)
    hn_ref[...] = lax.dot_general(onehot, stage_ref[buf], (((0,), (0,)), ((), ())), preferred_element_type=F32)

    @pl.when(jnp.max(jnp.where(late, 1, 0)) > 0)
    def _():
        chunk_iota = lax.broadcasted_iota(jnp.int32, (chunk, tt), 0)
        for e in range(ne):
            beyond = pos[e:e + 1, :] - starts[e] >= window
            for c0 in range(0, cap_total, chunk):
                copy = pltpu.make_async_copy(ye_ref.at[i // tiles, e, pl.ds(c0, chunk)], slow_ref, slow_sem)
                copy.start()
                copy.wait()
                hit = jnp.where(beyond & (pos[e:e + 1, :] - c0 == chunk_iota), 1.0, 0.0).astype(BF16)
                hn_ref[...] += lax.dot_general(hit, slow_ref[...], (((0,), (0,)), ((), ())),
                                               preferred_element_type=F32)

    sub = min(tt, 128)
    for r in range(tt // sub):
        rs = slice(r * sub, (r + 1) * sub)
        hn = _norm_rows(alpha * h_ref[rs, :] + gate_ref[...] * hn_ref[rs, :]) * g_ref[...] + b_ref[...]
        hn_ref[rs, :] = hn
        if with_next:
            act_ref[rs, :] = (_norm_rows(hn) * (1.0 + scale_ref[...]) + shift_ref[...]).astype(BF16)


def _slow_chunk(cap_total):
    for c in range(min(cap_total, 272) // 16 * 16, 0, -16):
        if cap_total % c == 0:
            return c
    raise ValueError(cap_total)


def _combine_resid(ts, pos, ye, h, mod, next_mod, ln_g, ln_b, bsz, n, ctx_len, alpha):
    ne, cap_total, d = ye.shape[1:]
    tt = ROW_TILE
    nt, ntc = n // tt, ctx_len // tt
    tiles = nt + ntc
    with_next = next_mod is not None
    window = min(COMBINE_WINDOW, cap_total)

    def sample(i):
        return i // tiles

    def row_tile(i):
        b, t = i // tiles, i % tiles
        return jnp.where(t < nt, b * nt + t, bsz * nt + b * ntc + (t - nt))

    def mod_spec(which):
        return pl.BlockSpec((None, None, 1, d),
                            lambda i, ts_ref: (jnp.where(i % tiles < nt, sample(i), bsz), which, 0, 0))

    vec = pl.BlockSpec((1, d), lambda i, ts_ref: (0, 0))
    blk = pl.BlockSpec((tt, d), lambda i, ts_ref: (row_tile(i), 0))
    in_specs = [pl.BlockSpec((None, ne, tt), lambda i, ts_ref: (sample(i), 0, i % tiles)),
                pl.BlockSpec(memory_space=pl.ANY), blk, mod_spec(5), vec, vec]
    args = [pos, ye, h, mod, ln_g.reshape(1, d), ln_b.reshape(1, d)]
    rows = bsz * tiles * tt
    out_specs, out_shape = [blk], [jax.ShapeDtypeStruct((rows, d), F32)]
    if with_next:
        in_specs += [mod_spec(0), mod_spec(1)]
        args += [next_mod, next_mod]
        out_specs.append(blk)
        out_shape.append(jax.ShapeDtypeStruct((rows, d), BF16))
    return pl.pallas_call(
        functools.partial(_combine_body, alpha=alpha, window=window, chunk=_slow_chunk(cap_total), tiles=tiles,
                          with_next=with_next),
        grid_spec=pltpu.PrefetchScalarGridSpec(
            num_scalar_prefetch=1,
            grid=(bsz * tiles,),
            in_specs=in_specs,
            out_specs=out_specs,
            scratch_shapes=[pltpu.VMEM((2, ne * window, d), BF16), pltpu.VMEM((_slow_chunk(cap_total), d), BF16),
                            pltpu.SemaphoreType.DMA((2,)), pltpu.SemaphoreType.DMA(())]),
        out_shape=out_shape,
        compiler_params=_params("arbitrary"),
        name="moe_combine",
    )(ts, *args)


def kernel(x, c, ctx, c_ctx, w_mod, b_mod, w_in, q_gain, k_gain, gm_ln_g, gm_ln_b, w_spatial, b_spatial, b_gates,
           ml_gain, w_out, ln1_g, ln1_b, w_router, w_e_gate, w_e_up, w_e_down, ln2_g, ln2_b):
    dm = _dims()
    bsz, n, d = x.shape
    ctx_len = ctx.shape[1]
    depth = w_mod.shape[0]
    alpha = (2 * depth) ** 0.25
    lat_rows = bsz * n
    main_w = dm["main_w"]
    n_gates = 4 * MLSTM_HEADS
    cap_l = CAPACITY_FACTOR * n // N_EXPERTS
    sample = jnp.arange(bsz, dtype=jnp.int32)[:, None, None]

    cc = jnp.concatenate([c, c_ctx[None], jnp.zeros((MOD_ROWS - bsz - 1, d), F32)], axis=0)
    mods = _mod_vectors(cc, w_mod, b_mod).reshape(depth, MOD_ROWS, N_MOD, 1, d)
    cos, sin = _rope_tables(n, ROW_TILE)
    w_in_t = jnp.swapaxes(w_in, 1, 2)
    w_out_b = w_out.astype(BF16)

    h_parts = (x.reshape(lat_rows, d), ctx.reshape(bsz * ctx_len, d))
    act = _ln_mod(*h_parts, mods[0], bsz, n)
    for l in range(depth):
        last = l == depth - 1
        rows = lat_rows if last else lat_rows + bsz * ctx_len
        mod = mods[l]

        p = _in_proj(act, w_in_t, l, main_w)
        gates = _gate_proj(act, w_in_t, l, main_w, n_gates, b_gates[l])
        gates_rows = jnp.swapaxes(jnp.concatenate([gates[:, lat_rows:].reshape(n_gates, bsz, ctx_len),
                                                   gates[:, :lat_rows].reshape(n_gates, bsz, n)], axis=2), 0, 1)

        qb, kb, vb = _qkv_prep(p, cos, sin, q_gain[l], k_gain[l], bsz, n, ctx_len)
        att = _attention(qb, kb, vb, bsz, n, ctx_len, not last)
        gm = _gmlp(p, gm_ln_g[l], gm_ln_b[l], w_spatial[l], b_spatial[l])
        hf, hb = _mlstm(p, gates_rows, bsz, n, ctx_len)
        ml = _mlstm_out(hf, hb, p, ml_gain[l], rows)
        y = _outproj(att, gm, ml, w_out_b, l, h_parts, mod, rows, bsz, n, alpha)
        w_r = jnp.pad(w_router[l].T.astype(BF16), ((0, HEAD_DIM - N_EXPERTS), (0, 0)))
        h, actp, lt = _post_ln(y, mod, ln1_g[l], ln1_b[l], w_r, bsz, n)

        pos, idx_l, gate, ts = _route_select(lt, bsz, n, 0)
        idx = idx_l[..., 0] + n * sample
        if not last:
            pos_c, idx_c, gate_c, ts_c = _route_select(lt, bsz, ctx_len, lat_rows // ctx_len)
            idx = jnp.concatenate([idx, idx_c[..., 0] + lat_rows + ctx_len * sample], axis=2)
            gate = jnp.concatenate([gate, gate_c], axis=2)
            pos = jnp.concatenate([pos, jnp.where(pos_c >= 0, pos_c + cap_l, -1)], axis=2)
            ts = jnp.concatenate([ts, ts_c + cap_l], axis=2)
        xe = _gather(idx, actp)
        hid = _expert_up(xe, w_e_gate, w_e_up, l)
        ye = _expert_down(hid, w_e_down, gate, l)
        if not last:
            h, act = _combine_resid(ts, pos, ye, h, mod, mods[l + 1], ln2_g[l], ln2_b[l], bsz, n, ctx_len, alpha)
            h_parts = (h, h)
        else:
            h, = _combine_resid(ts, pos, ye, h, mod, None, ln2_g[l], ln2_b[l], bsz, n, 0, alpha)
    return h.reshape(bsz, n, d)
```

```python
import functools

import jax
import jax.numpy as jnp
from jax import lax
from jax.experimental import pallas as pl
from jax.experimental.pallas import tpu as pltpu

D_MODEL = 4096
BATCH = 4
SEQ = 4096
DEPTH = 2
CTX_LEN = 256
GRID_W = 64
HEAD_DIM = 128
ATT_Q_HEADS = 16
ATT_KV_HEADS = 4
ROPE_THETA = 10000.0
GMLP_CHUNK = 128
GMLP_GROUPS = 8
MLSTM_HEADS = 4
MLSTM_CHUNK = 128
N_EXPERTS = 16
CAPACITY_FACTOR = 2
N_MOD = 6
EPS = 1e-6
LOG2_E = 1.4426950408889634

F32 = jnp.float32
BF16 = jnp.bfloat16

ROW_TILE = 256
MOD_ROWS = 8
COMBINE_WINDOW = 64
VMEM_LIMIT = 56 * 1024 * 1024


def _dims():
    d = D_MODEL
    att_w = ATT_Q_HEADS * HEAD_DIM
    kv_w = ATT_KV_HEADS * HEAD_DIM
    gm_w = d // 4
    ml_w = d // 4
    offs, start = {}, 0
    for name, width in (("att_q", att_w), ("att_k", kv_w), ("att_v", kv_w), ("gm_u", gm_w), ("gm_v", gm_w),
                        ("ml_q", ml_w), ("ml_k", ml_w), ("ml_v", ml_w), ("ml_o", ml_w),
                        ("ml_gates", 4 * MLSTM_HEADS)):
        offs[name] = start
        start += width
    return dict(d=d, att_w=att_w, kv_w=kv_w, gm_w=gm_w, ml_w=ml_w, offs=offs, proj_w=start,
                main_w=offs["ml_gates"], ml_dh=ml_w // MLSTM_HEADS, ff=d // 4)


def _tile(n, prefs):
    for t in prefs:
        if n % t == 0:
            return t
    return n


def _params(*sem):
    return pltpu.CompilerParams(dimension_semantics=sem, vmem_limit_bytes=VMEM_LIMIT)


def _sigmoid(x):
    return 1.0 / (1.0 + jnp.exp(-x))


def _norm_rows(x):
    mu = jnp.mean(x, axis=-1, keepdims=True)
    xc = x - mu
    var = jnp.mean(xc * xc, axis=-1, keepdims=True)
    return xc * lax.rsqrt(var + EPS)


def _gelu_tanh(x):
    return 0.5 * x * (1.0 + jnp.tanh(0.7978845608028654 * (x + 0.044715 * (x * x * x))))


def _mod_body(x_ref, w_ref, b_ref, o_ref):
    x = x_ref[...]
    xs = (x * _sigmoid(x)).astype(BF16)
    o_ref[...] = jnp.dot(xs, w_ref[...].astype(BF16), preferred_element_type=F32) + b_ref[...]


def _mod_vectors(cc, w_mod, b_mod):
    depth, d, n = w_mod.shape
    tn = _tile(n, (512, 256, 128))
    return pl.pallas_call(
        _mod_body,
        grid=(depth, n // tn),
        in_specs=[pl.BlockSpec((MOD_ROWS, d), lambda l, j: (0, 0)),
                  pl.BlockSpec((None, d, tn), lambda l, j: (l, 0, j)),
                  pl.BlockSpec((None, 1, tn), lambda l, j: (l, 0, j))],
        out_specs=pl.BlockSpec((None, MOD_ROWS, tn), lambda l, j: (l, 0, j)),
        out_shape=jax.ShapeDtypeStruct((depth, MOD_ROWS, n), F32),
        compiler_params=_params("parallel", "parallel"),
        name="mod_vectors",
    )(cc, w_mod, b_mod.reshape(depth, 1, n))


def _mod_spec(which, tiles_per_sample, bsz):
    d = D_MODEL
    return pl.BlockSpec((None, None, 1, d),
                        lambda i, *_: (jnp.minimum(i // tiles_per_sample, bsz), which, 0, 0))


def _ln_mod_body(h_ref, h_ctx_ref, shift_ref, scale_ref, o_ref, *, lat_tiles):
    h = jnp.where(pl.program_id(0) < lat_tiles, h_ref[...], h_ctx_ref[...])
    o_ref[...] = (_norm_rows(h) * (1.0 + scale_ref[...]) + shift_ref[...]).astype(BF16)


def _ln_mod(h_lat, h_ctx, mod, bsz, n):
    d = h_lat.shape[1]
    rows = h_lat.shape[0] + h_ctx.shape[0]
    tm = ROW_TILE
    tps = n // tm
    return pl.pallas_call(
        functools.partial(_ln_mod_body, lat_tiles=bsz * tps),
        grid=(rows // tm,),
        in_specs=[*_stream_specs((tm, d), bsz * tps), _mod_spec(0, tps, bsz), _mod_spec(1, tps, bsz)],
        out_specs=pl.BlockSpec((tm, d), lambda i: (i, 0)),
        out_shape=jax.ShapeDtypeStruct((rows, d), BF16),
        compiler_params=_params("parallel"),
        name="ln_mod",
    )(h_lat, h_ctx, mod, mod)


def _in_proj_body(a_ref, w_ref, o_ref):
    o_ref[...] = lax.dot_general(a_ref[...], w_ref[...].astype(BF16), (((1,), (1,)), ((), ())),
                                 preferred_element_type=F32)


def _in_proj(a, w_in_t, layer, n_out):
    m, k = a.shape
    tm = _tile(m, (1024, 512, 256))
    tn = _tile(n_out, (1024, 512, 256, 128))
    return pl.pallas_call(
        _in_proj_body,
        grid=(n_out // tn, m // tm),
        in_specs=[pl.BlockSpec((tm, k), lambda j, i: (i, 0)),
                  pl.BlockSpec((None, tn, k), lambda j, i: (layer, j, 0), pipeline_mode=pl.Buffered(1))],
        out_specs=pl.BlockSpec((tm, tn), lambda j, i: (i, j)),
        out_shape=jax.ShapeDtypeStruct((m, n_out), F32),
        compiler_params=_params("parallel", "parallel"),
        name="in_proj",
    )(a, w_in_t)


def _gate_proj_body(a_ref, w_ref, b_ref, o_ref):
    o_ref[...] = lax.dot_general(w_ref[...].astype(BF16), a_ref[...], (((1,), (1,)), ((), ())),
                                 preferred_element_type=F32) + b_ref[...]


def _gate_proj(a, w_in_t, layer, row0, n_gates, bias):
    m, k = a.shape
    tm = _tile(m, (1024, 512, 256))
    return pl.pallas_call(
        _gate_proj_body,
        grid=(m // tm,),
        in_specs=[pl.BlockSpec((tm, k), lambda i: (i, 0)),
                  pl.BlockSpec((None, n_gates, k), lambda i: (layer, row0 // n_gates, 0)),
                  pl.BlockSpec((n_gates, 1), lambda i: (0, 0))],
        out_specs=pl.BlockSpec((n_gates, tm), lambda i: (0, i)),
        out_shape=jax.ShapeDtypeStruct((n_gates, m), F32),
        compiler_params=_params("parallel"),
        name="gate_proj",
    )(a, w_in_t, bias.reshape(n_gates, 1))


def _rope_tables(n, tm):
    rows = n // GRID_W
    row = jnp.repeat(jnp.arange(rows), GRID_W).astype(F32)
    col = jnp.tile(jnp.arange(GRID_W), rows).astype(F32)
    n_freq = HEAD_DIM // 4
    inv_freq = ROPE_THETA ** (-jnp.arange(n_freq, dtype=F32) / n_freq)
    ang_r, ang_c = row[:, None] * inv_freq, col[:, None] * inv_freq
    cos = jnp.concatenate([jnp.cos(ang_r), jnp.cos(ang_r), jnp.cos(ang_c), jnp.cos(ang_c)], axis=1)
    sin = jnp.concatenate([-jnp.sin(ang_r), jnp.sin(ang_r), -jnp.sin(ang_c), jnp.sin(ang_c)], axis=1)
    cos = jnp.concatenate([cos, jnp.ones((tm, HEAD_DIM), F32)], axis=0)
    sin = jnp.concatenate([sin, jnp.zeros((tm, HEAD_DIM), F32)], axis=0)
    return cos, sin


def _qkv_prep_body(q_ref, k_ref, v_ref, cos_ref, sin_ref, qg_ref, kg_ref, qo_ref, ko_ref, vo_ref):
    cos, sin = cos_ref[...], sin_ref[...]
    lane = lax.broadcasted_iota(jnp.int32, cos.shape, 1)
    first_half = (lane % (HEAD_DIM // 2)) < (HEAD_DIM // 4)
    partner_lane = jnp.where(first_half, lane + HEAD_DIM // 4, lane - HEAD_DIM // 4)

    def norm_rope(x, gain):
        y = x * lax.rsqrt(jnp.mean(x * x, axis=-1, keepdims=True) + EPS) * gain
        partner = jnp.take_along_axis(y, partner_lane, axis=1)
        return y * cos + partner * sin

    qg = qg_ref[...] * (HEAD_DIM ** -0.5 * LOG2_E)
    for h in range(ATT_Q_HEADS):
        sl = slice(h * HEAD_DIM, (h + 1) * HEAD_DIM)
        qo_ref[:, sl] = norm_rope(q_ref[:, sl], qg).astype(BF16)
    for h in range(ATT_KV_HEADS):
        sl = slice(h * HEAD_DIM, (h + 1) * HEAD_DIM)
        ko_ref[:, sl] = norm_rope(k_ref[:, sl], kg_ref[...]).astype(BF16)
        vo_ref[:, 2 * h * HEAD_DIM:(2 * h + 1) * HEAD_DIM] = v_ref[:, sl].astype(BF16)
        vo_ref[:, (2 * h + 1) * HEAD_DIM:(2 * h + 2) * HEAD_DIM] = jnp.where(lane == 0, 1.0, 0.0).astype(BF16)


def _qkv_prep(p, cos, sin, q_gain, k_gain, bsz, n, ctx_len):
    dm = _dims()
    rows = p.shape[0]
    tm = ROW_TILE
    tps, tpc = n // tm, ctx_len // tm
    nlat = bsz * tps
    att_w, kv_w = dm["att_w"], dm["kv_w"]
    ntot = n + ctx_len

    def table_map(i):
        return (jnp.where(i < nlat, i % tps, tps), 0)

    def kv_map(i):
        j = i - nlat
        return (jnp.where(i < nlat, i // tps, j // tpc), jnp.where(i < nlat, tpc + i % tps, j % tpc), 0)

    return pl.pallas_call(
        _qkv_prep_body,
        grid=(rows // tm,),
        in_specs=[pl.BlockSpec((tm, att_w), lambda i: (i, dm["offs"]["att_q"] // att_w)),
                  pl.BlockSpec((tm, kv_w), lambda i: (i, dm["offs"]["att_k"] // kv_w)),
                  pl.BlockSpec((tm, kv_w), lambda i: (i, dm["offs"]["att_v"] // kv_w)),
                  pl.BlockSpec((tm, HEAD_DIM), table_map),
                  pl.BlockSpec((tm, HEAD_DIM), table_map),
                  pl.BlockSpec((1, HEAD_DIM), lambda i: (0, 0)),
                  pl.BlockSpec((1, HEAD_DIM), lambda i: (0, 0))],
        out_specs=[pl.BlockSpec((tm, att_w), lambda i: (i, 0)),
                   pl.BlockSpec((None, tm, kv_w), kv_map),
                   pl.BlockSpec((None, tm, 2 * kv_w), kv_map)],
        out_shape=[jax.ShapeDtypeStruct((rows, att_w), BF16),
                   jax.ShapeDtypeStruct((bsz, ntot, kv_w), BF16),
                   jax.ShapeDtypeStruct((bsz, ntot, 2 * kv_w), BF16)],
        compiler_params=_params("parallel"),
        name="qkv_prep",
    )(p, p, p, cos, sin, q_gain.reshape(1, HEAD_DIM), k_gain.reshape(1, HEAD_DIM))


def _attn_heads(q_ref, k, v, o_ref):
    group = ATT_Q_HEADS // ATT_KV_HEADS

    def scores(h):
        return lax.dot_general(q_ref[:, h * HEAD_DIM:(h + 1) * HEAD_DIM], k, (((1,), (1,)), ((), ())),
                               preferred_element_type=F32)

    s = scores(0)
    for h in range(group):
        s_next = scores(h + 1) if h + 1 < group else None
        e = jnp.exp2(s - jnp.max(s, axis=-1, keepdims=True))
        o = jnp.dot(e.astype(BF16), v, preferred_element_type=F32)
        o_ref[:, h * HEAD_DIM:(h + 1) * HEAD_DIM] = (o[:, 0:HEAD_DIM] / o[:, HEAD_DIM:HEAD_DIM + 1]).astype(BF16)
        s = s_next


def _attn_body(q_ref, k_ref, v_ref, o_ref, *, lat_q_tiles, ctx_len):
    qi = pl.program_id(2)

    @pl.when(qi < lat_q_tiles)
    def _():
        _attn_heads(q_ref, k_ref[...], v_ref[...], o_ref)

    @pl.when(qi >= lat_q_tiles)
    def _():
        _attn_heads(q_ref, k_ref[0:ctx_len, :], v_ref[0:ctx_len, :], o_ref)


def _attention(qb, kb, vb, bsz, n, ctx_len, with_ctx):
    rows_all = qb.shape[0]
    tq = ROW_TILE
    tps, tpc = n // tq, ctx_len // tq
    nlat = bsz * tps
    group_w = (ATT_Q_HEADS // ATT_KV_HEADS) * HEAD_DIM
    ntot = n + ctx_len
    q_tiles = tps + (tpc if with_ctx else 0)
    rows = rows_all if with_ctx else bsz * n

    def q_map(b, g, qi):
        return (jnp.where(qi < tps, b * tps + qi, nlat + b * tpc + (qi - tps)), g)

    return pl.pallas_call(
        functools.partial(_attn_body, lat_q_tiles=tps, ctx_len=ctx_len),
        grid=(bsz, ATT_KV_HEADS, q_tiles),
        in_specs=[pl.BlockSpec((tq, group_w), q_map),
                  pl.BlockSpec((None, ntot, HEAD_DIM), lambda b, g, qi: (b, 0, g)),
                  pl.BlockSpec((None, ntot, 2 * HEAD_DIM), lambda b, g, qi: (b, 0, g))],
        out_specs=pl.BlockSpec((tq, group_w), q_map),
        out_shape=jax.ShapeDtypeStruct((rows, ATT_Q_HEADS * HEAD_DIM), BF16),
        compiler_params=_params("parallel", "parallel", "parallel"),
        name="attention",
    )(qb, kb, vb)


def _gmlp_body(u_ref, v_ref, g_ref, b_ref, ws_ref, bs_ref, o_ref):
    u = _gelu_tanh(u_ref[...])
    v = (_norm_rows(_gelu_tanh(v_ref[...])) * g_ref[...] + b_ref[...]).astype(BF16)
    gd = v.shape[1] // GMLP_GROUPS
    for c in range(v.shape[0] // GMLP_CHUNK):
        rs = slice(c * GMLP_CHUNK, (c + 1) * GMLP_CHUNK)
        for g in range(GMLP_GROUPS):
            cs = slice(g * gd, (g + 1) * gd)
            sv = jnp.dot(ws_ref[g].astype(BF16), v[rs, cs], preferred_element_type=F32) + bs_ref[:, g:g + 1]
            o_ref[rs, cs] = (u[rs, cs] * sv).astype(BF16)


def _gmlp(p, ln_g, ln_b, w_s, b_s):
    dm = _dims()
    rows = p.shape[0]
    tm = ROW_TILE
    gm_w = dm["gm_w"]
    return pl.pallas_call(
        _gmlp_body,
        grid=(rows // tm,),
        in_specs=[pl.BlockSpec((tm, gm_w), lambda i: (i, dm["offs"]["gm_u"] // gm_w)),
                  pl.BlockSpec((tm, gm_w), lambda i: (i, dm["offs"]["gm_v"] // gm_w)),
                  pl.BlockSpec((1, gm_w), lambda i: (0, 0)),
                  pl.BlockSpec((1, gm_w), lambda i: (0, 0)),
                  pl.BlockSpec((GMLP_GROUPS, GMLP_CHUNK, GMLP_CHUNK), lambda i: (0, 0, 0)),
                  pl.BlockSpec((GMLP_CHUNK, GMLP_GROUPS), lambda i: (0, 0))],
        out_specs=pl.BlockSpec((tm, gm_w), lambda i: (i, 0)),
        out_shape=jax.ShapeDtypeStruct((rows, gm_w), BF16),
        compiler_params=_params("parallel"),
        name="gmlp",
    )(p, p, ln_g.reshape(1, gm_w), ln_b.reshape(1, gm_w), w_s, b_s.T)


def _mlstm_chain_step(q_ref, k_ref, v_ref, g_ref, o_ref, c_ref, n_ref, m_ref, direction, head, cols, masks):
    seen, seen_t, eye = masks
    heads = MLSTM_HEADS

    def to_col(x_row):
        return jnp.sum(jnp.where(eye, x_row, 0.0), axis=1, keepdims=True)

    gi = 2 * direction * heads + head
    i_row = g_ref[gi:gi + 1, :]
    f_raw = g_ref[gi + heads:gi + heads + 1, :]
    lf_row = jnp.minimum(f_raw, 0.0) - jnp.log1p(jnp.exp(-jnp.abs(f_raw)))
    lf_col, i_col = to_col(lf_row), to_col(i_row)
    bcum_col = jnp.sum(jnp.where(seen, lf_row, 0.0), axis=1, keepdims=True)
    bcum_row = jnp.sum(jnp.where(seen_t, lf_col, 0.0), axis=0, keepdims=True)
    b_last = jnp.sum(lf_row, axis=1, keepdims=True)

    m_old = m_ref[direction, head]
    g_row = b_last - bcum_row + i_row
    g_col = b_last - bcum_col + i_col
    m_new = jnp.maximum(b_last + m_old, jnp.max(g_row, axis=1, keepdims=True))
    w_prev = jnp.exp(b_last + m_old - m_new)

    q = q_ref[:, cols]
    k = k_ref[:, cols] * ((cols.stop - cols.start) ** -0.5)
    vb = v_ref[:, cols].astype(BF16)
    qb, kb = q.astype(BF16), k.astype(BF16)
    c_old, n_old = c_ref[direction, head], n_ref[direction, head]

    a_col = bcum_col + m_old
    dlog = jnp.where(seen, bcum_col - bcum_row + i_row, -jnp.inf)
    mt = jnp.maximum(a_col, jnp.max(dlog, axis=1, keepdims=True))
    s = lax.dot_general(qb, kb, (((1,), (1,)), ((), ())), preferred_element_type=F32) * jnp.exp(dlog - mt)
    wa = jnp.exp(a_col - mt)
    num = (wa * jnp.dot(qb, c_old.astype(BF16), preferred_element_type=F32)
           + jnp.dot(s.astype(BF16), vb, preferred_element_type=F32))
    den = wa * jnp.sum(q * n_old, axis=1, keepdims=True) + jnp.sum(s, axis=1, keepdims=True)
    o_ref[:, cols] = num / jnp.maximum(jnp.abs(den), jnp.exp(-mt))

    kw = k * jnp.exp(g_col - m_new)
    c_ref[direction, head] = w_prev * c_old + lax.dot_general(kw.astype(BF16), vb, (((0,), (0,)), ((), ())),
                                                              preferred_element_type=F32)
    n_ref[direction, head] = w_prev * n_old + jnp.sum(kw, axis=0, keepdims=True)
    m_ref[direction, head] = m_new


def _mlstm_body(qf_ref, kf_ref, vf_ref, gf_ref, qb_ref, kb_ref, vb_ref, gb_ref, of_ref, ob_ref,
                c_ref, n_ref, m_ref):
    @pl.when(pl.program_id(1) == 0)
    def _():
        c_ref[...] = jnp.zeros_like(c_ref)
        n_ref[...] = jnp.zeros_like(n_ref)
        m_ref[...] = jnp.zeros_like(m_ref)

    chunk = qf_ref.shape[0]
    dh = qf_ref.shape[1] // MLSTM_HEADS
    row = lax.broadcasted_iota(jnp.int32, (chunk, chunk), 0)
    col = lax.broadcasted_iota(jnp.int32, (chunk, chunk), 1)
    eye = row == col
    fwd_masks = (col <= row, row <= col, eye)
    bwd_masks = (col >= row, row >= col, eye)
    for head in range(MLSTM_HEADS):
        cols = slice(head * dh, (head + 1) * dh)
        _mlstm_chain_step(qf_ref, kf_ref, vf_ref, gf_ref, of_ref, c_ref, n_ref, m_ref, 0, head, cols, fwd_masks)
        _mlstm_chain_step(qb_ref, kb_ref, vb_ref, gb_ref, ob_ref, c_ref, n_ref, m_ref, 1, head, cols, bwd_masks)


def _mlstm(p, gates_rows, bsz, n, ctx_len):
    dm = _dims()
    rows = p.shape[0]
    lc = MLSTM_CHUNK
    dh, ml_w = dm["ml_dh"], dm["ml_w"]
    heads = MLSTM_HEADS
    ncc, nlc = ctx_len // lc, n // lc
    steps = ncc + nlc
    lat_chunks = bsz * nlc

    def seq_chunk(dr, j):
        return j if dr == 0 else jnp.where(j < ncc, ncc - 1 - j, ncc + nlc - 1 - (j - ncc))

    def row_chunk(b, dr, j):
        c = seq_chunk(dr, j)
        return jnp.where(c < ncc, lat_chunks + b * ncc + c, b * nlc + (c - ncc))

    def proj_spec(name, dr):
        return pl.BlockSpec((lc, ml_w), lambda b, j: (row_chunk(b, dr, j), dm["offs"][name] // ml_w))

    def dir_specs(dr):
        return [proj_spec("ml_q", dr), proj_spec("ml_k", dr), proj_spec("ml_v", dr),
                pl.BlockSpec((None, 4 * heads, lc), lambda b, j: (b, 0, seq_chunk(dr, j)))]

    return pl.pallas_call(
        _mlstm_body,
        grid=(bsz, steps),
        in_specs=dir_specs(0) + dir_specs(1),
        out_specs=[pl.BlockSpec((lc, ml_w), lambda b, j: (row_chunk(b, 0, j), 0)),
                   pl.BlockSpec((lc, ml_w), lambda b, j: (row_chunk(b, 1, j), 0))],
        out_shape=[jax.ShapeDtypeStruct((rows, ml_w), F32), jax.ShapeDtypeStruct((rows, ml_w), F32)],
        scratch_shapes=[pltpu.VMEM((2, heads, dh, dh), F32), pltpu.VMEM((2, heads, 1, dh), F32),
                        pltpu.VMEM((2, heads, 1, 1), F32)],
        compiler_params=_params("parallel", "arbitrary"),
        name="mlstm_scan",
    )(p, p, p, gates_rows, p, p, p, gates_rows)


def _mlstm_out_body(hf_ref, hb_ref, o_ref, gain_ref, out_ref):
    dh = out_ref.shape[1] // MLSTM_HEADS
    for h in range(MLSTM_HEADS):
        sl = slice(h * dh, (h + 1) * dh)
        hs = hf_ref[:, sl] + hb_ref[:, sl]
        hn = hs * lax.rsqrt(jnp.mean(hs * hs, axis=-1, keepdims=True) + EPS) * gain_ref[:, sl]
        out_ref[:, sl] = (_sigmoid(o_ref[:, sl]) * hn).astype(BF16)


def _mlstm_out(hf, hb, p, gain, rows):
    dm = _dims()
    tm = ROW_TILE
    ml_w = dm["ml_w"]
    blk = pl.BlockSpec((tm, ml_w), lambda i: (i, 0))
    return pl.pallas_call(
        _mlstm_out_body,
        grid=(rows // tm,),
        in_specs=[blk, blk,
                  pl.BlockSpec((tm, ml_w), lambda i: (i, dm["offs"]["ml_o"] // ml_w)),
                  pl.BlockSpec((1, ml_w), lambda i: (0, 0))],
        out_specs=blk,
        out_shape=jax.ShapeDtypeStruct((rows, ml_w), BF16),
        compiler_params=_params("parallel"),
        name="mlstm_out",
    )(hf, hb, p, gain.reshape(1, ml_w))


def _pack_halves(y):
    half = y.shape[1] // 2
    lo = lax.bitcast_convert_type(y[:, :half].astype(BF16).astype(F32), jnp.uint32) >> 16
    hi = lax.bitcast_convert_type(y[:, half:].astype(BF16).astype(F32), jnp.uint32) & jnp.uint32(0xFFFF0000)
    return lo | hi


def _unpack_halves(w):
    lo = lax.bitcast_convert_type(w << 16, F32).astype(BF16)
    hi = lax.bitcast_convert_type(w & jnp.uint32(0xFFFF0000), F32).astype(BF16)
    return lo, hi


def _outproj_body(att_ref, gm_ref, ml_ref, w1_ref, w2_ref, w3_ref, h_ref, h_ctx_ref, gate_ref, y_ref, *,
                  alpha, lat_tiles):
    mix = (jnp.dot(att_ref[...], w1_ref[...], preferred_element_type=F32)
           + jnp.dot(gm_ref[...], w2_ref[...], preferred_element_type=F32)
           + jnp.dot(ml_ref[...], w3_ref[...], preferred_element_type=F32))
    h = jnp.where(pl.program_id(0) < lat_tiles, h_ref[...], h_ctx_ref[...])
    y_ref[...] = alpha * h + gate_ref[...] * mix


def _stream_specs(block, lat_tiles, col=None):
    def lat_map(i, *rest):
        return (jnp.minimum(i, lat_tiles - 1), 0 if col is None else rest[col])

    def ctx_map(i, *rest):
        return (jnp.maximum(i - lat_tiles, 0), 0 if col is None else rest[col])

    return [pl.BlockSpec(block, lat_map), pl.BlockSpec(block, ctx_map)]


def _outproj(att, gm, ml, w_out, layer, h_parts, mod, rows, bsz, n, alpha):
    dm = _dims()
    d = dm["d"]
    tm = _tile(n, (1024, 512, 256))
    tn = _tile(d, (512, 256, 128))
    tps = n // tm
    att_w, gm_w, ml_w = dm["att_w"], dm["gm_w"], dm["ml_w"]
    return pl.pallas_call(
        functools.partial(_outproj_body, alpha=alpha, lat_tiles=bsz * tps),
        grid=(rows // tm, d // tn),
        in_specs=[pl.BlockSpec((tm, att_w), lambda i, j: (i, 0)),
                  pl.BlockSpec((tm, gm_w), lambda i, j: (i, 0)),
                  pl.BlockSpec((tm, ml_w), lambda i, j: (i, 0)),
                  pl.BlockSpec((None, att_w, tn), lambda i, j: (layer, 0, j)),
                  pl.BlockSpec((None, gm_w, tn), lambda i, j: (layer, att_w // gm_w, j)),
                  pl.BlockSpec((None, ml_w, tn), lambda i, j: (layer, (att_w + gm_w) // ml_w, j)),
                  *_stream_specs((tm, tn), bsz * tps, col=0),
                  pl.BlockSpec((None, None, 1, tn), lambda i, j: (jnp.minimum(i // tps, bsz), 2, 0, j))],
        out_specs=pl.BlockSpec((tm, tn), lambda i, j: (i, j)),
        out_shape=jax.ShapeDtypeStruct((rows, d), F32),
        compiler_params=_params("parallel", "parallel"),
        name="outproj",
    )(att, gm, ml, w_out, w_out, w_out, *h_parts, mod)


def _post_ln_body(y_ref, g_ref, b_ref, shift_ref, scale_ref, wr_ref, hn_ref, act_ref, lt_ref):
    hn = _norm_rows(y_ref[...]) * g_ref[...] + b_ref[...]
    hn_ref[...] = hn
    act = _norm_rows(hn) * (1.0 + scale_ref[...]) + shift_ref[...]
    act_ref[...] = _pack_halves(act)
    lt_ref[...] = lax.dot_general(wr_ref[...], act.astype(BF16), (((1,), (1,)), ((), ())),
                                  preferred_element_type=F32)


def _post_ln(y, mod, ln_g, ln_b, w_r, bsz, n):
    rows, d = y.shape
    tm = ROW_TILE
    tps = n // tm
    vec = pl.BlockSpec((1, d), lambda i: (0, 0))
    return pl.pallas_call(
        _post_ln_body,
        grid=(rows // tm,),
        in_specs=[pl.BlockSpec((tm, d), lambda i: (i, 0)), vec, vec, _mod_spec(3, tps, bsz), _mod_spec(4, tps, bsz),
                  pl.BlockSpec((HEAD_DIM, d), lambda i: (0, 0))],
        out_specs=[pl.BlockSpec((tm, d), lambda i: (i, 0)), pl.BlockSpec((tm, d // 2), lambda i: (i, 0)),
                   pl.BlockSpec((HEAD_DIM, tm), lambda i: (0, i))],
        out_shape=[jax.ShapeDtypeStruct((rows, d), F32), jax.ShapeDtypeStruct((rows, d // 2), jnp.uint32),
                   jax.ShapeDtypeStruct((HEAD_DIM, rows), F32)],
        compiler_params=_params("parallel"),
        name="post_ln",
    )(y, ln_g.reshape(1, d), ln_b.reshape(1, d), mod, mod, w_r)


def _tile_part_copies(w_refs, stage_ref, sems, layer, tile, part, slot, n_inner, parts):
    rows, cols = stage_ref.shape[2] // parts, stage_ref.shape[3]
    rs = pl.ds(pl.multiple_of(part * rows, rows), rows)
    cs = pl.ds(pl.multiple_of((tile % n_inner) * cols, cols), cols)
    return [pltpu.make_async_copy(w.at[layer, tile // n_inner, rs, cs], stage_ref.at[slot, m, rs, :], sems.at[slot])
            for m, w in enumerate(w_refs)]


def _stream_weight_tile(w_refs, stage_ref, sems, layer, n_inner):
    t, b = pl.program_id(0), pl.program_id(1)
    parts = pl.num_programs(1)
    args = (w_refs, stage_ref, sems, layer)

    @pl.when((t == 0) & (b == 0))
    def _():
        for part in range(parts):
            for copy in _tile_part_copies(*args, 0, part, 0, n_inner, parts):
                copy.start()

    @pl.when(t + 1 < pl.num_programs(0))
    def _():
        for copy in _tile_part_copies(*args, t + 1, b, (t + 1) % 2, n_inner, parts):
            copy.start()

    @pl.when(b == 0)
    def _():
        for part in range(parts):
            for copy in _tile_part_copies(*args, t, part, t % 2, n_inner, parts):
                copy.wait()

    return t % 2


def _expert_up_body(x_ref, wg_ref, wu_ref, o_ref, stage_ref, sems, *, layer, nf):
    slot = _stream_weight_tile((wg_ref, wu_ref), stage_ref, sems, layer, nf)
    group, cap, d = x_ref.shape
    x = x_ref[...].reshape(group * cap, d)
    g = jnp.dot(x, stage_ref[slot, 0].astype(BF16), preferred_element_type=F32)
    u = jnp.dot(x, stage_ref[slot, 1].astype(BF16), preferred_element_type=F32)
    o_ref[...] = (g * _sigmoid(g) * u).astype(BF16).reshape(o_ref.shape)


def _sample_group(bsz):
    return 2 if bsz % 2 == 0 else 1


def _expert_up(xe, wg, wu, layer):
    bsz, ne, cap, d = xe.shape
    ff = wg.shape[3]
    tf = _tile(ff, (256, 128))
    nf = ff // tf
    group = _sample_group(bsz)
    return pl.pallas_call(
        functools.partial(_expert_up_body, layer=layer, nf=nf),
        grid=(ne * nf, bsz // group),
        in_specs=[pl.BlockSpec((group, None, cap, d), lambda t, b: (b, t // nf, 0, 0)),
                  pl.BlockSpec(memory_space=pl.ANY), pl.BlockSpec(memory_space=pl.ANY)],
        out_specs=pl.BlockSpec((group, None, cap, tf), lambda t, b: (b, t // nf, 0, t % nf)),
        out_shape=jax.ShapeDtypeStruct((bsz, ne, cap, ff), BF16),
        scratch_shapes=[pltpu.VMEM((2, 2, d, tf), F32), pltpu.SemaphoreType.DMA((2,))],
        compiler_params=_params("arbitrary", "arbitrary"),
        name="expert_up",
    )(xe, wg, wu)


def _expert_down_body(x_ref, w_ref, gate_ref, o_ref, stage_ref, sems, *, layer, nj):
    slot = _stream_weight_tile((w_ref,), stage_ref, sems, layer, nj)
    group, cap, ff = x_ref.shape
    y = jnp.dot(x_ref[...].reshape(group * cap, ff), stage_ref[slot, 0].astype(BF16), preferred_element_type=F32)
    o_ref[...] = (y * gate_ref[...].reshape(group * cap, 1)).astype(BF16).reshape(o_ref.shape)


def _expert_down(hid, wd, gate, layer):
    bsz, ne, cap, ff = hid.shape
    d = wd.shape[3]
    tn = _tile(d, (2048, 1024, 512))
    nj = d // tn
    group = _sample_group(bsz)
    return pl.pallas_call(
        functools.partial(_expert_down_body, layer=layer, nj=nj),
        grid=(ne * nj, bsz // group),
        in_specs=[pl.BlockSpec((group, None, cap, ff), lambda t, b: (b, t // nj, 0, 0)),
                  pl.BlockSpec(memory_space=pl.ANY),
                  pl.BlockSpec((group, None, cap, 1), lambda t, b: (b, t // nj, 0, 0))],
        out_specs=pl.BlockSpec((group, None, cap, tn), lambda t, b: (b, t // nj, 0, t % nj)),
        out_shape=jax.ShapeDtypeStruct((bsz, ne, cap, d), BF16),
        scratch_shapes=[pltpu.VMEM((2, 1, ff, tn), F32), pltpu.SemaphoreType.DMA((2,))],
        compiler_params=_params("arbitrary", "arbitrary"),
        name="expert_down",
    )(hid, wd, gate)


def _prefix_excl(mask):
    lanes = 128
    x = jnp.where(mask, 1.0, 0.0)
    r = lax.broadcasted_iota(jnp.int32, (lanes, lanes), 0)
    c = lax.broadcasted_iota(jnp.int32, (lanes, lanes), 1)
    tri = jnp.where(r <= c, 1.0, 0.0).astype(BF16)
    carry = jnp.zeros((x.shape[0], 1), F32)
    out = []
    for t in range(x.shape[1] // lanes):
        xt = x[:, t * lanes:(t + 1) * lanes]
        inc = jnp.dot(xt.astype(BF16), tri, preferred_element_type=F32)
        out.append(inc - xt + carry)
        carry = carry + inc[:, lanes - 1:lanes]
    return jnp.concatenate(out, axis=1)


def _route_select_body(lg_ref, pos_ref, idx_ref, gate_ref, ts_ref, *, cap, chunk, tile):
    lg = lg_ref[...]
    ne, n = lg.shape
    ex = jnp.exp(lg - jnp.max(lg, axis=0, keepdims=True))
    aff = ex / jnp.sum(ex, axis=0, keepdims=True)
    bits = lax.bitcast_convert_type(aff, jnp.int32)

    thr = jnp.zeros((ne, 1), jnp.int32)
    for bit in range(30, -1, -1):
        cand = thr | (1 << bit)
        count = jnp.sum(jnp.where(bits >= cand, 1.0, 0.0), axis=1, keepdims=True)
        thr = jnp.where(count >= cap, cand, thr)
    above = bits > thr
    tied = bits == thr
    need = cap - jnp.sum(jnp.where(above, 1.0, 0.0), axis=1, keepdims=True)
    chosen = above | (tied & (_prefix_excl(tied) < need))
    before = _prefix_excl(chosen)
    pos = jnp.where(chosen, before, -1.0).astype(jnp.int32)
    pos_ref[...] = pos
    for t in range(n // tile):
        ts_ref[:, t:t + 1] = before[:, t * tile:t * tile + 1].astype(jnp.int32)

    tok = lax.broadcasted_iota(jnp.int32, (1, n), 1)
    slot = lax.broadcasted_iota(jnp.int32, (chunk, n), 0)
    tok_digits = [(tok // 64).astype(F32).astype(BF16), (tok % 64).astype(F32).astype(BF16)]
    for e in range(ne):
        a = aff[e:e + 1, :]
        a_hi = a.astype(BF16)
        a_rest = a - a_hi.astype(F32)
        a_mid = a_rest.astype(BF16)
        a_lo = (a_rest - a_mid.astype(F32)).astype(BF16)
        vals = jnp.concatenate(tok_digits + [a_hi, a_mid, a_lo, jnp.zeros((3, n), BF16)], axis=0)
        for c0 in range(0, cap, chunk):
            hit = jnp.where(pos[e:e + 1, :] == slot + c0, 1.0, 0.0).astype(BF16)
            r = lax.dot_general(hit, vals, (((1,), (1,)), ((), ())), preferred_element_type=F32)
            idx_ref[e, c0:c0 + chunk, :] = (r[:, 0:1] * 64.0 + r[:, 1:2]).astype(jnp.int32)
            gate_ref[e, c0:c0 + chunk, :] = r[:, 2:3] + r[:, 3:4] + r[:, 4:5]


def _route_select(lt, bsz, n, col_block0):
    ne = N_EXPERTS
    cap = CAPACITY_FACTOR * n // ne
    chunk = min(cap, 128)
    nt = n // ROW_TILE
    return pl.pallas_call(
        functools.partial(_route_select_body, cap=cap, chunk=chunk, tile=ROW_TILE),
        grid=(bsz,),
        in_specs=[pl.BlockSpec((ne, n), lambda b: (0, col_block0 + b))],
        out_specs=[pl.BlockSpec((None, ne, n), lambda b: (b, 0, 0)),
                   pl.BlockSpec((None, ne, cap, 1), lambda b: (b, 0, 0, 0)),
                   pl.BlockSpec((None, ne, cap, 1), lambda b: (b, 0, 0, 0)),
                   pl.BlockSpec((None, ne, nt), lambda b: (b, 0, 0))],
        out_shape=[jax.ShapeDtypeStruct((bsz, ne, n), jnp.int32),
                   jax.ShapeDtypeStruct((bsz, ne, cap, 1), jnp.int32),
                   jax.ShapeDtypeStruct((bsz, ne, cap, 1), F32),
                   jax.ShapeDtypeStruct((bsz, ne, nt), jnp.int32)],
        compiler_params=_params("parallel"),
        name="route_select",
    )(lt)


def _gather_body(idx_ref, src_ref, o_ref, buf_ref, sem):
    b, e = pl.program_id(0), pl.program_id(1)
    cap = buf_ref.shape[0]

    def row_copy(s):
        return pltpu.make_async_copy(src_ref.at[pl.ds(idx_ref[b, e, s], 1)], buf_ref.at[pl.ds(s, 1)], sem)

    def start(s, carry):
        row_copy(s).start()
        return carry

    def wait(s, carry):
        row_copy(s).wait()
        return carry

    unroll = 8 if cap % 8 == 0 else 1
    lax.fori_loop(0, cap, start, 0, unroll=unroll)
    lax.fori_loop(0, cap, wait, 0, unroll=unroll)
    lo, hi = _unpack_halves(buf_ref[...])
    half = lo.shape[1]
    o_ref[:, 0:half] = lo
    o_ref[:, half:] = hi


def _gather(idx, actp):
    bsz, ne, cap = idx.shape
    half = actp.shape[1]
    return pl.pallas_call(
        _gather_body,
        grid_spec=pltpu.PrefetchScalarGridSpec(
            num_scalar_prefetch=1,
            grid=(bsz, ne),
            in_specs=[pl.BlockSpec(memory_space=pl.ANY)],
            out_specs=pl.BlockSpec((None, None, cap, 2 * half), lambda b, e, idx_ref: (b, e, 0, 0)),
            scratch_shapes=[pltpu.VMEM((cap, half), jnp.uint32), pltpu.SemaphoreType.DMA(())]),
        out_shape=jax.ShapeDtypeStruct((bsz, ne, cap, 2 * half), BF16),
        compiler_params=_params("arbitrary", "arbitrary"),
        name="moe_gather",
    )(idx, actp)


def _combine_body(ts_ref, pos_ref, ye_ref, h_ref, gate_ref, g_ref, b_ref, *rest, alpha, window, chunk, tiles,
                  with_next):
    if with_next:
        shift_ref, scale_ref, hn_ref, act_ref, stage_ref, slow_ref, sems, slow_sem = rest
    else:
        hn_ref, stage_ref, slow_ref, sems, slow_sem = rest
    ne, tt = pos_ref.shape
    cap_total = ye_ref.shape[2]
    i = pl.program_id(0)
    buf = i % 2

    def window_start(step, e):
        b = step // tiles
        s = jnp.minimum((ts_ref[b, e, step % tiles] // 16) * 16, cap_total - window)
        return b, pl.multiple_of(s, 16)

    def window_copy(step, e, slot):
        b, s = window_start(step, e)
        return pltpu.make_async_copy(ye_ref.at[b, e, pl.ds(s, window)],
                                     stage_ref.at[slot, pl.ds(e * window, window)], sems.at[slot])

    @pl.when(i == 0)
    def _():
        for e in range(ne):
            window_copy(i, e, buf).start()

    @pl.when(i + 1 < pl.num_programs(0))
    def _():
        for e in range(ne):
            window_copy(i + 1, e, 1 - buf).start()

    for e in range(ne):
        window_copy(i, e, buf).wait()

    pos = pos_ref[...]
    slot_iota = lax.broadcasted_iota(jnp.int32, (window, tt), 0)
    starts, blocks, late = [], [], None
    for e in range(ne):
        starts.append(window_start(i, e)[1])
        rel = pos[e:e + 1, :] - starts[e]
        blocks.append(jnp.where(rel == slot_iota, 1.0, 0.0).astype(BF16))
        late = (rel >= window) if late is None else (late | (rel >= window))
    onehot = jnp.concatenate(blocks, axis=0)
    hn_ref[...] = lax.dot_general(onehot, stage_ref[buf], (((0,), (0,)), ((), ())), preferred_element_type=F32)

    @pl.when(jnp.max(jnp.where(late, 1, 0)) > 0)
    def _():
        chunk_iota = lax.broadcasted_iota(jnp.int32, (chunk, tt), 0)
        for e in range(ne):
            beyond = pos[e:e + 1, :] - starts[e] >= window
            for c0 in range(0, cap_total, chunk):
                copy = pltpu.make_async_copy(ye_ref.at[i // tiles, e, pl.ds(c0, chunk)], slow_ref, slow_sem)
                copy.start()
                copy.wait()
                hit = jnp.where(beyond & (pos[e:e + 1, :] - c0 == chunk_iota), 1.0, 0.0).astype(BF16)
                hn_ref[...] += lax.dot_general(hit, slow_ref[...], (((0,), (0,)), ((), ())),
                                               preferred_element_type=F32)

    sub = min(tt, 128)
    for r in range(tt // sub):
        rs = slice(r * sub, (r + 1) * sub)
        hn = _norm_rows(alpha * h_ref[rs, :] + gate_ref[...] * hn_ref[rs, :]) * g_ref[...] + b_ref[...]
        hn_ref[rs, :] = hn
        if with_next:
            act_ref[rs, :] = (_norm_rows(hn) * (1.0 + scale_ref[...]) + shift_ref[...]).astype(BF16)


def _slow_chunk(cap_total):
    for c in range(min(cap_total, 272) // 16 * 16, 0, -16):
        if cap_total % c == 0:
            return c
    raise ValueError(cap_total)


def _combine_resid(ts, pos, ye, h, mod, next_mod, ln_g, ln_b, bsz, n, ctx_len, alpha):
    ne, cap_total, d = ye.shape[1:]
    tt = ROW_TILE
    nt, ntc = n // tt, ctx_len // tt
    tiles = nt + ntc
    with_next = next_mod is not None
    window = min(COMBINE_WINDOW, cap_total)

    def sample(i):
        return i // tiles

    def row_tile(i):
        b, t = i // tiles, i % tiles
        return jnp.where(t < nt, b * nt + t, bsz * nt + b * ntc + (t - nt))

    def mod_spec(which):
        return pl.BlockSpec((None, None, 1, d),
                            lambda i, ts_ref: (jnp.where(i % tiles < nt, sample(i), bsz), which, 0, 0))

    vec = pl.BlockSpec((1, d), lambda i, ts_ref: (0, 0))
    blk = pl.BlockSpec((tt, d), lambda i, ts_ref: (row_tile(i), 0))
    in_specs = [pl.BlockSpec((None, ne, tt), lambda i, ts_ref: (sample(i), 0, i % tiles)),
                pl.BlockSpec(memory_space=pl.ANY), blk, mod_spec(5), vec, vec]
    args = [pos, ye, h, mod, ln_g.reshape(1, d), ln_b.reshape(1, d)]
    rows = bsz * tiles * tt
    out_specs, out_shape = [blk], [jax.ShapeDtypeStruct((rows, d), F32)]
    if with_next:
        in_specs += [mod_spec(0), mod_spec(1)]
        args += [next_mod, next_mod]
        out_specs.append(blk)
        out_shape.append(jax.ShapeDtypeStruct((rows, d), BF16))
    return pl.pallas_call(
        functools.partial(_combine_body, alpha=alpha, window=window, chunk=_slow_chunk(cap_total), tiles=tiles,
                          with_next=with_next),
        grid_spec=pltpu.PrefetchScalarGridSpec(
            num_scalar_prefetch=1,
            grid=(bsz * tiles,),
            in_specs=in_specs,
            out_specs=out_specs,
            scratch_shapes=[pltpu.VMEM((2, ne * window, d), BF16), pltpu.VMEM((_slow_chunk(cap_total), d), BF16),
                            pltpu.SemaphoreType.DMA((2,)), pltpu.SemaphoreType.DMA(())]),
        out_shape=out_shape,
        compiler_params=_params("arbitrary"),
        name="moe_combine",
    )(ts, *args)


def kernel(x, c, ctx, c_ctx, w_mod, b_mod, w_in, q_gain, k_gain, gm_ln_g, gm_ln_b, w_spatial, b_spatial, b_gates,
           ml_gain, w_out, ln1_g, ln1_b, w_router, w_e_gate, w_e_up, w_e_down, ln2_g, ln2_b):
    dm = _dims()
    bsz, n, d = x.shape
    ctx_len = ctx.shape[1]
    depth = w_mod.shape[0]
    alpha = (2 * depth) ** 0.25
    lat_rows = bsz * n
    main_w = dm["main_w"]
    n_gates = 4 * MLSTM_HEADS
    cap_l = CAPACITY_FACTOR * n // N_EXPERTS
    sample = jnp.arange(bsz, dtype=jnp.int32)[:, None, None]

    cc = jnp.concatenate([c, c_ctx[None], jnp.zeros((MOD_ROWS - bsz - 1, d), F32)], axis=0)
    mods = _mod_vectors(cc, w_mod, b_mod).reshape(depth, MOD_ROWS, N_MOD, 1, d)
    cos, sin = _rope_tables(n, ROW_TILE)
    w_in_t = jnp.swapaxes(w_in, 1, 2)
    w_out_b = w_out.astype(BF16)

    h_parts = (x.reshape(lat_rows, d), ctx.reshape(bsz * ctx_len, d))
    act = _ln_mod(*h_parts, mods[0], bsz, n)
    for l in range(depth):
        last = l == depth - 1
        rows = lat_rows if last else lat_rows + bsz * ctx_len
        mod = mods[l]

        p = _in_proj(act, w_in_t, l, main_w)
        gates = _gate_proj(act, w_in_t, l, main_w, n_gates, b_gates[l])
        gates_rows = jnp.swapaxes(jnp.concatenate([gates[:, lat_rows:].reshape(n_gates, bsz, ctx_len),
                                                   gates[:, :lat_rows].reshape(n_gates, bsz, n)], axis=2), 0, 1)

        qb, kb, vb = _qkv_prep(p, cos, sin, q_gain[l], k_gain[l], bsz, n, ctx_len)
        att = _attention(qb, kb, vb, bsz, n, ctx_len, not last)
        gm = _gmlp(p, gm_ln_g[l], gm_ln_b[l], w_spatial[l], b_spatial[l])
        hf, hb = _mlstm(p, gates_rows, bsz, n, ctx_len)
        ml = _mlstm_out(hf, hb, p, ml_gain[l], rows)
        y = _outproj(att, gm, ml, w_out_b, l, h_parts, mod, rows, bsz, n, alpha)
        w_r = jnp.pad(w_router[l].T.astype(BF16), ((0, HEAD_DIM - N_EXPERTS), (0, 0)))
        h, actp, lt = _post_ln(y, mod, ln1_g[l], ln1_b[l], w_r, bsz, n)

        pos, idx_l, gate, ts = _route_select(lt, bsz, n, 0)
        idx = idx_l[..., 0] + n * sample
        if not last:
            pos_c, idx_c, gate_c, ts_c = _route_select(lt, bsz, ctx_len, lat_rows // ctx_len)
            idx = jnp.concatenate([idx, idx_c[..., 0] + lat_rows + ctx_len * sample], axis=2)
            gate = jnp.concatenate([gate, gate_c], axis=2)
            pos = jnp.concatenate([pos, jnp.where(pos_c >= 0, pos_c + cap_l, -1)], axis=2)
            ts = jnp.concatenate([ts, ts_c + cap_l], axis=2)
        xe = _gather(idx, actp)
        hid = _expert_up(xe, w_e_gate, w_e_up, l)
        ye = _expert_down(hid, w_e_down, gate, l)
        if not last:
            h, act = _combine_resid(ts, pos, ye, h, mod, mods[l + 1], ln2_g[l], ln2_b[l], bsz, n, ctx_len, alpha)
            h_parts = (h, h)
        else:
            h, = _combine_resid(ts, pos, ye, h, mod, None, ln2_g[l], ln2_b[l], bsz, n, 0, alpha)
    return h.reshape(bsz, n, d)
```

```python
import functools

import jax
import jax.numpy as jnp
from jax import lax
from jax.experimental import pallas as pl
from jax.experimental.pallas import tpu as pltpu

D_MODEL = 4096
BATCH = 4
SEQ = 4096
DEPTH = 2
CTX_LEN = 256
GRID_W = 64
HEAD_DIM = 128
ATT_Q_HEADS = 16
ATT_KV_HEADS = 4
ROPE_THETA = 10000.0
GMLP_CHUNK = 128
GMLP_GROUPS = 8
MLSTM_HEADS = 4
MLSTM_CHUNK = 128
N_EXPERTS = 16
CAPACITY_FACTOR = 2
N_MOD = 6
EPS = 1e-6
LOG2_E = 1.4426950408889634

F32 = jnp.float32
BF16 = jnp.bfloat16

ROW_TILE = 256
MOD_ROWS = 8
COMBINE_WINDOW = 64
VMEM_LIMIT = 56 * 1024 * 1024


def _dims():
    d = D_MODEL
    att_w = ATT_Q_HEADS * HEAD_DIM
    kv_w = ATT_KV_HEADS * HEAD_DIM
    gm_w = d // 4
    ml_w = d // 4
    offs, start = {}, 0
    for name, width in (("att_q", att_w), ("att_k", kv_w), ("att_v", kv_w), ("gm_u", gm_w), ("gm_v", gm_w),
                        ("ml_q", ml_w), ("ml_k", ml_w), ("ml_v", ml_w), ("ml_o", ml_w),
                        ("ml_gates", 4 * MLSTM_HEADS)):
        offs[name] = start
        start += width
    return dict(d=d, att_w=att_w, kv_w=kv_w, gm_w=gm_w, ml_w=ml_w, offs=offs, proj_w=start,
                main_w=offs["ml_gates"], ml_dh=ml_w // MLSTM_HEADS, ff=d // 4)


def _tile(n, prefs):
    for t in prefs:
        if n % t == 0:
            return t
    return n


def _params(*sem):
    return pltpu.CompilerParams(dimension_semantics=sem, vmem_limit_bytes=VMEM_LIMIT)


def _sigmoid(x):
    return 1.0 / (1.0 + jnp.exp(-x))


def _norm_rows(x):
    mu = jnp.mean(x, axis=-1, keepdims=True)
    xc = x - mu
    var = jnp.mean(xc * xc, axis=-1, keepdims=True)
    return xc * lax.rsqrt(var + EPS)


def _gelu_tanh(x):
    return 0.5 * x * (1.0 + jnp.tanh(0.7978845608028654 * (x + 0.044715 * (x * x * x))))


def _mod_body(x_ref, w_ref, b_ref, o_ref):
    x = x_ref[...]
    xs = (x * _sigmoid(x)).astype(BF16)
    o_ref[...] = jnp.dot(xs, w_ref[...].astype(BF16), preferred_element_type=F32) + b_ref[...]


def _mod_vectors(cc, w_mod, b_mod):
    depth, d, n = w_mod.shape
    tn = _tile(n, (512, 256, 128))
    return pl.pallas_call(
        _mod_body,
        grid=(depth, n // tn),
        in_specs=[pl.BlockSpec((MOD_ROWS, d), lambda l, j: (0, 0)),
                  pl.BlockSpec((None, d, tn), lambda l, j: (l, 0, j)),
                  pl.BlockSpec((None, 1, tn), lambda l, j: (l, 0, j))],
        out_specs=pl.BlockSpec((None, MOD_ROWS, tn), lambda l, j: (l, 0, j)),
        out_shape=jax.ShapeDtypeStruct((depth, MOD_ROWS, n), F32),
        compiler_params=_params("parallel", "parallel"),
        name="mod_vectors",
    )(cc, w_mod, b_mod.reshape(depth, 1, n))


def _mod_spec(which, tiles_per_sample, bsz):
    d = D_MODEL
    return pl.BlockSpec((None, None, 1, d),
                        lambda i, *_: (jnp.minimum(i // tiles_per_sample, bsz), which, 0, 0))


def _ln_mod_body(h_ref, h_ctx_ref, shift_ref, scale_ref, o_ref, *, lat_tiles):
    h = jnp.where(pl.program_id(0) < lat_tiles, h_ref[...], h_ctx_ref[...])
    o_ref[...] = (_norm_rows(h) * (1.0 + scale_ref[...]) + shift_ref[...]).astype(BF16)


def _ln_mod(h_lat, h_ctx, mod, bsz, n):
    d = h_lat.shape[1]
    rows = h_lat.shape[0] + h_ctx.shape[0]
    tm = ROW_TILE
    tps = n // tm
    return pl.pallas_call(
        functools.partial(_ln_mod_body, lat_tiles=bsz * tps),
        grid=(rows // tm,),
        in_specs=[*_stream_specs((tm, d), bsz * tps), _mod_spec(0, tps, bsz), _mod_spec(1, tps, bsz)],
        out_specs=pl.BlockSpec((tm, d), lambda i: (i, 0)),
        out_shape=jax.ShapeDtypeStruct((rows, d), BF16),
        compiler_params=_params("parallel"),
        name="ln_mod",
    )(h_lat, h_ctx, mod, mod)


def _in_proj_body(a_ref, w_ref, o_ref):
    o_ref[...] = lax.dot_general(a_ref[...], w_ref[...].astype(BF16), (((1,), (1,)), ((), ())),
                                 preferred_element_type=F32)


def _in_proj(a, w_in_t, layer, n_out):
    m, k = a.shape
    tm = _tile(m, (1024, 512, 256))
    tn = _tile(n_out, (1024, 512, 256, 128))
    return pl.pallas_call(
        _in_proj_body,
        grid=(n_out // tn, m // tm),
        in_specs=[pl.BlockSpec((tm, k), lambda j, i: (i, 0)),
                  pl.BlockSpec((None, tn, k), lambda j, i: (layer, j, 0), pipeline_mode=pl.Buffered(1))],
        out_specs=pl.BlockSpec((tm, tn), lambda j, i: (i, j)),
        out_shape=jax.ShapeDtypeStruct((m, n_out), F32),
        compiler_params=_params("parallel", "parallel"),
        name="in_proj",
    )(a, w_in_t)


def _gate_proj_body(a_ref, w_ref, b_ref, o_ref):
    o_ref[...] = lax.dot_general(w_ref[...].astype(BF16), a_ref[...], (((1,), (1,)), ((), ())),
                                 preferred_element_type=F32) + b_ref[...]


def _gate_proj(a, w_in_t, layer, row0, n_gates, bias):
    m, k = a.shape
    tm = _tile(m, (1024, 512, 256))
    return pl.pallas_call(
        _gate_proj_body,
        grid=(m // tm,),
        in_specs=[pl.BlockSpec((tm, k), lambda i: (i, 0)),
                  pl.BlockSpec((None, n_gates, k), lambda i: (layer, row0 // n_gates, 0)),
                  pl.BlockSpec((n_gates, 1), lambda i: (0, 0))],
        out_specs=pl.BlockSpec((n_gates, tm), lambda i: (0, i)),
        out_shape=jax.ShapeDtypeStruct((n_gates, m), F32),
        compiler_params=_params("parallel"),
        name="gate_proj",
    )(a, w_in_t, bias.reshape(n_gates, 1))


def _rope_tables(n, tm):
    rows = n // GRID_W
    row = jnp.repeat(jnp.arange(rows), GRID_W).astype(F32)
    col = jnp.tile(jnp.arange(GRID_W), rows).astype(F32)
    n_freq = HEAD_DIM // 4
    inv_freq = ROPE_THETA ** (-jnp.arange(n_freq, dtype=F32) / n_freq)
    ang_r, ang_c = row[:, None] * inv_freq, col[:, None] * inv_freq
    cos = jnp.concatenate([jnp.cos(ang_r), jnp.cos(ang_r), jnp.cos(ang_c), jnp.cos(ang_c)], axis=1)
    sin = jnp.concatenate([-jnp.sin(ang_r), jnp.sin(ang_r), -jnp.sin(ang_c), jnp.sin(ang_c)], axis=1)
    cos = jnp.concatenate([cos, jnp.ones((tm, HEAD_DIM), F32)], axis=0)
    sin = jnp.concatenate([sin, jnp.zeros((tm, HEAD_DIM), F32)], axis=0)
    return cos, sin


def _qkv_prep_body(q_ref, k_ref, v_ref, cos_ref, sin_ref, qg_ref, kg_ref, qo_ref, ko_ref, vo_ref):
    cos, sin = cos_ref[...], sin_ref[...]
    lane = lax.broadcasted_iota(jnp.int32, cos.shape, 1)
    first_half = (lane % (HEAD_DIM // 2)) < (HEAD_DIM // 4)
    partner_lane = jnp.where(first_half, lane + HEAD_DIM // 4, lane - HEAD_DIM // 4)

    def norm_rope(x, gain):
        y = x * lax.rsqrt(jnp.mean(x * x, axis=-1, keepdims=True) + EPS) * gain
        partner = jnp.take_along_axis(y, partner_lane, axis=1)
        return y * cos + partner * sin

    qg = qg_ref[...] * (HEAD_DIM ** -0.5 * LOG2_E)
    for h in range(ATT_Q_HEADS):
        sl = slice(h * HEAD_DIM, (h + 1) * HEAD_DIM)
        qo_ref[:, sl] = norm_rope(q_ref[:, sl], qg).astype(BF16)
    for h in range(ATT_KV_HEADS):
        sl = slice(h * HEAD_DIM, (h + 1) * HEAD_DIM)
        ko_ref[:, sl] = norm_rope(k_ref[:, sl], kg_ref[...]).astype(BF16)
        vo_ref[:, 2 * h * HEAD_DIM:(2 * h + 1) * HEAD_DIM] = v_ref[:, sl].astype(BF16)
        vo_ref[:, (2 * h + 1) * HEAD_DIM:(2 * h + 2) * HEAD_DIM] = jnp.where(lane == 0, 1.0, 0.0).astype(BF16)


def _qkv_prep(p, cos, sin, q_gain, k_gain, bsz, n, ctx_len):
    dm = _dims()
    rows = p.shape[0]
    tm = ROW_TILE
    tps, tpc = n // tm, ctx_len // tm
    nlat = bsz * tps
    att_w, kv_w = dm["att_w"], dm["kv_w"]
    ntot = n + ctx_len

    def table_map(i):
        return (jnp.where(i < nlat, i % tps, tps), 0)

    def kv_map(i):
        j = i - nlat
        return (jnp.where(i < nlat, i // tps, j // tpc), jnp.where(i < nlat, tpc + i % tps, j % tpc), 0)

    return pl.pallas_call(
        _qkv_prep_body,
        grid=(rows // tm,),
        in_specs=[pl.BlockSpec((tm, att_w), lambda i: (i, dm["offs"]["att_q"] // att_w)),
                  pl.BlockSpec((tm, kv_w), lambda i: (i, dm["offs"]["att_k"] // kv_w)),
                  pl.BlockSpec((tm, kv_w), lambda i: (i, dm["offs"]["att_v"] // kv_w)),
                  pl.BlockSpec((tm, HEAD_DIM), table_map),
                  pl.BlockSpec((tm, HEAD_DIM), table_map),
                  pl.BlockSpec((1, HEAD_DIM), lambda i: (0, 0)),
                  pl.BlockSpec((1, HEAD_DIM), lambda i: (0, 0))],
        out_specs=[pl.BlockSpec((tm, att_w), lambda i: (i, 0)),
                   pl.BlockSpec((None, tm, kv_w), kv_map),
                   pl.BlockSpec((None, tm, 2 * kv_w), kv_map)],
        out_shape=[jax.ShapeDtypeStruct((rows, att_w), BF16),
                   jax.ShapeDtypeStruct((bsz, ntot, kv_w), BF16),
                   jax.ShapeDtypeStruct((bsz, ntot, 2 * kv_w), BF16)],
        compiler_params=_params("parallel"),
        name="qkv_prep",
    )(p, p, p, cos, sin, q_gain.reshape(1, HEAD_DIM), k_gain.reshape(1, HEAD_DIM))


def _attn_heads(q_ref, k, v, o_ref):
    group = ATT_Q_HEADS // ATT_KV_HEADS

    def scores(h):
        return lax.dot_general(q_ref[:, h * HEAD_DIM:(h + 1) * HEAD_DIM], k, (((1,), (1,)), ((), ())),
                               preferred_element_type=F32)

    s = scores(0)
    for h in range(group):
        s_next = scores(h + 1) if h + 1 < group else None
        e = jnp.exp2(s - jnp.max(s, axis=-1, keepdims=True))
        o = jnp.dot(e.astype(BF16), v, preferred_element_type=F32)
        o_ref[:, h * HEAD_DIM:(h + 1) * HEAD_DIM] = (o[:, 0:HEAD_DIM] / o[:, HEAD_DIM:HEAD_DIM + 1]).astype(BF16)
        s = s_next


def _attn_body(q_ref, k_ref, v_ref, o_ref, *, lat_q_tiles, ctx_len):
    qi = pl.program_id(2)

    @pl.when(qi < lat_q_tiles)
    def _():
        _attn_heads(q_ref, k_ref[...], v_ref[...], o_ref)

    @pl.when(qi >= lat_q_tiles)
    def _():
        _attn_heads(q_ref, k_ref[0:ctx_len, :], v_ref[0:ctx_len, :], o_ref)


def _attention(qb, kb, vb, bsz, n, ctx_len, with_ctx):
    rows_all = qb.shape[0]
    tq = ROW_TILE
    tps, tpc = n // tq, ctx_len // tq
    nlat = bsz * tps
    group_w = (ATT_Q_HEADS // ATT_KV_HEADS) * HEAD_DIM
    ntot = n + ctx_len
    q_tiles = tps + (tpc if with_ctx else 0)
    rows = rows_all if with_ctx else bsz * n

    def q_map(b, g, qi):
        return (jnp.where(qi < tps, b * tps + qi, nlat + b * tpc + (qi - tps)), g)

    return pl.pallas_call(
        functools.partial(_attn_body, lat_q_tiles=tps, ctx_len=ctx_len),
        grid=(bsz, ATT_KV_HEADS, q_tiles),
        in_specs=[pl.BlockSpec((tq, group_w), q_map),
                  pl.BlockSpec((None, ntot, HEAD_DIM), lambda b, g, qi: (b, 0, g)),
                  pl.BlockSpec((None, ntot, 2 * HEAD_DIM), lambda b, g, qi: (b, 0, g))],
        out_specs=pl.BlockSpec((tq, group_w), q_map),
        out_shape=jax.ShapeDtypeStruct((rows, ATT_Q_HEADS * HEAD_DIM), BF16),
        compiler_params=_params("parallel", "parallel", "parallel"),
        name="attention",
    )(qb, kb, vb)


def _gmlp_body(u_ref, v_ref, g_ref, b_ref, ws_ref, bs_ref, o_ref):
    u = _gelu_tanh(u_ref[...])
    v = (_norm_rows(_gelu_tanh(v_ref[...])) * g_ref[...] + b_ref[...]).astype(BF16)
    gd = v.shape[1] // GMLP_GROUPS
    for c in range(v.shape[0] // GMLP_CHUNK):
        rs = slice(c * GMLP_CHUNK, (c + 1) * GMLP_CHUNK)
        for g in range(GMLP_GROUPS):
            cs = slice(g * gd, (g + 1) * gd)
            sv = jnp.dot(ws_ref[g].astype(BF16), v[rs, cs], preferred_element_type=F32) + bs_ref[:, g:g + 1]
            o_ref[rs, cs] = (u[rs, cs] * sv).astype(BF16)


def _gmlp(p, ln_g, ln_b, w_s, b_s):
    dm = _dims()
    rows = p.shape[0]
    tm = ROW_TILE
    gm_w = dm["gm_w"]
    return pl.pallas_call(
        _gmlp_body,
        grid=(rows // tm,),
        in_specs=[pl.BlockSpec((tm, gm_w), lambda i: (i, dm["offs"]["gm_u"] // gm_w)),
                  pl.BlockSpec((tm, gm_w), lambda i: (i, dm["offs"]["gm_v"] // gm_w)),
                  pl.BlockSpec((1, gm_w), lambda i: (0, 0)),
                  pl.BlockSpec((1, gm_w), lambda i: (0, 0)),
                  pl.BlockSpec((GMLP_GROUPS, GMLP_CHUNK, GMLP_CHUNK), lambda i: (0, 0, 0)),
                  pl.BlockSpec((GMLP_CHUNK, GMLP_GROUPS), lambda i: (0, 0))],
        out_specs=pl.BlockSpec((tm, gm_w), lambda i: (i, 0)),
        out_shape=jax.ShapeDtypeStruct((rows, gm_w), BF16),
        compiler_params=_params("parallel"),
        name="gmlp",
    )(p, p, ln_g.reshape(1, gm_w), ln_b.reshape(1, gm_w), w_s, b_s.T)


def _mlstm_chain_step(q_ref, k_ref, v_ref, g_ref, o_ref, c_ref, n_ref, m_ref, direction, head, cols, masks):
    seen, seen_t, eye = masks
    heads = MLSTM_HEADS

    def to_col(x_row):
        return jnp.sum(jnp.where(eye, x_row, 0.0), axis=1, keepdims=True)

    gi = 2 * direction * heads + head
    i_row = g_ref[gi:gi + 1, :]
    f_raw = g_ref[gi + heads:gi + heads + 1, :]
    lf_row = jnp.minimum(f_raw, 0.0) - jnp.log1p(jnp.exp(-jnp.abs(f_raw)))
    lf_col, i_col = to_col(lf_row), to_col(i_row)
    bcum_col = jnp.sum(jnp.where(seen, lf_row, 0.0), axis=1, keepdims=True)
    bcum_row = jnp.sum(jnp.where(seen_t, lf_col, 0.0), axis=0, keepdims=True)
    b_last = jnp.sum(lf_row, axis=1, keepdims=True)

    m_old = m_ref[direction, head]
    g_row = b_last - bcum_row + i_row
    g_col = b_last - bcum_col + i_col
    m_new = jnp.maximum(b_last + m_old, jnp.max(g_row, axis=1, keepdims=True))
    w_prev = jnp.exp(b_last + m_old - m_new)

    q = q_ref[:, cols]
    k = k_ref[:, cols] * ((cols.stop - cols.start) ** -0.5)
    vb = v_ref[:, cols].astype(BF16)
    qb, kb = q.astype(BF16), k.astype(BF16)
    c_old, n_old = c_ref[direction, head], n_ref[direction, head]

    a_col = bcum_col + m_old
    dlog = jnp.where(seen, bcum_col - bcum_row + i_row, -jnp.inf)
    mt = jnp.maximum(a_col, jnp.max(dlog, axis=1, keepdims=True))
    s = lax.dot_general(qb, kb, (((1,), (1,)), ((), ())), preferred_element_type=F32) * jnp.exp(dlog - mt)
    wa = jnp.exp(a_col - mt)
    num = (wa * jnp.dot(qb, c_old.astype(BF16), preferred_element_type=F32)
           + jnp.dot(s.astype(BF16), vb, preferred_element_type=F32))
    den = wa * jnp.sum(q * n_old, axis=1, keepdims=True) + jnp.sum(s, axis=1, keepdims=True)
    o_ref[:, cols] = num / jnp.maximum(jnp.abs(den), jnp.exp(-mt))

    kw = k * jnp.exp(g_col - m_new)
    c_ref[direction, head] = w_prev * c_old + lax.dot_general(kw.astype(BF16), vb, (((0,), (0,)), ((), ())),
                                                              preferred_element_type=F32)
    n_ref[direction, head] = w_prev * n_old + jnp.sum(kw, axis=0, keepdims=True)
    m_ref[direction, head] = m_new


def _mlstm_body(qf_ref, kf_ref, vf_ref, gf_ref, qb_ref, kb_ref, vb_ref, gb_ref, of_ref, ob_ref,
                c_ref, n_ref, m_ref):
    @pl.when(pl.program_id(1) == 0)
    def _():
        c_ref[...] = jnp.zeros_like(c_ref)
        n_ref[...] = jnp.zeros_like(n_ref)
        m_ref[...] = jnp.zeros_like(m_ref)

    chunk = qf_ref.shape[0]
    dh = qf_ref.shape[1] // MLSTM_HEADS
    row = lax.broadcasted_iota(jnp.int32, (chunk, chunk), 0)
    col = lax.broadcasted_iota(jnp.int32, (chunk, chunk), 1)
    eye = row == col
    fwd_masks = (col <= row, row <= col, eye)
    bwd_masks = (col >= row, row >= col, eye)
    for head in range(MLSTM_HEADS):
        cols = slice(head * dh, (head + 1) * dh)
        _mlstm_chain_step(qf_ref, kf_ref, vf_ref, gf_ref, of_ref, c_ref, n_ref, m_ref, 0, head, cols, fwd_masks)
        _mlstm_chain_step(qb_ref, kb_ref, vb_ref, gb_ref, ob_ref, c_ref, n_ref, m_ref, 1, head, cols, bwd_masks)


def _mlstm(p, gates_rows, bsz, n, ctx_len):
    dm = _dims()
    rows = p.shape[0]
    lc = MLSTM_CHUNK
    dh, ml_w = dm["ml_dh"], dm["ml_w"]
    heads = MLSTM_HEADS
    ncc, nlc = ctx_len // lc, n // lc
    steps = ncc + nlc
    lat_chunks = bsz * nlc

    def seq_chunk(dr, j):
        return j if dr == 0 else jnp.where(j < ncc, ncc - 1 - j, ncc + nlc - 1 - (j - ncc))

    def row_chunk(b, dr, j):
        c = seq_chunk(dr, j)
        return jnp.where(c < ncc, lat_chunks + b * ncc + c, b * nlc + (c - ncc))

    def proj_spec(name, dr):
        return pl.BlockSpec((lc, ml_w), lambda b, j: (row_chunk(b, dr, j), dm["offs"][name] // ml_w))

    def dir_specs(dr):
        return [proj_spec("ml_q", dr), proj_spec("ml_k", dr), proj_spec("ml_v", dr),
                pl.BlockSpec((None, 4 * heads, lc), lambda b, j: (b, 0, seq_chunk(dr, j)))]

    return pl.pallas_call(
        _mlstm_body,
        grid=(bsz, steps),
        in_specs=dir_specs(0) + dir_specs(1),
        out_specs=[pl.BlockSpec((lc, ml_w), lambda b, j: (row_chunk(b, 0, j), 0)),
                   pl.BlockSpec((lc, ml_w), lambda b, j: (row_chunk(b, 1, j), 0))],
        out_shape=[jax.ShapeDtypeStruct((rows, ml_w), F32), jax.ShapeDtypeStruct((rows, ml_w), F32)],
        scratch_shapes=[pltpu.VMEM((2, heads, dh, dh), F32), pltpu.VMEM((2, heads, 1, dh), F32),
                        pltpu.VMEM((2, heads, 1, 1), F32)],
        compiler_params=_params("parallel", "arbitrary"),
        name="mlstm_scan",
    )(p, p, p, gates_rows, p, p, p, gates_rows)


def _mlstm_out_body(hf_ref, hb_ref, o_ref, gain_ref, out_ref):
    dh = out_ref.shape[1] // MLSTM_HEADS
    for h in range(MLSTM_HEADS):
        sl = slice(h * dh, (h + 1) * dh)
        hs = hf_ref[:, sl] + hb_ref[:, sl]
        hn = hs * lax.rsqrt(jnp.mean(hs * hs, axis=-1, keepdims=True) + EPS) * gain_ref[:, sl]
        out_ref[:, sl] = (_sigmoid(o_ref[:, sl]) * hn).astype(BF16)


def _mlstm_out(hf, hb, p, gain, rows):
    dm = _dims()
    tm = ROW_TILE
    ml_w = dm["ml_w"]
    blk = pl.BlockSpec((tm, ml_w), lambda i: (i, 0))
    return pl.pallas_call(
        _mlstm_out_body,
        grid=(rows // tm,),
        in_specs=[blk, blk,
                  pl.BlockSpec((tm, ml_w), lambda i: (i, dm["offs"]["ml_o"] // ml_w)),
                  pl.BlockSpec((1, ml_w), lambda i: (0, 0))],
        out_specs=blk,
        out_shape=jax.ShapeDtypeStruct((rows, ml_w), BF16),
        compiler_params=_params("parallel"),
        name="mlstm_out",
    )(hf, hb, p, gain.reshape(1, ml_w))


def _pack_halves(y):
    half = y.shape[1] // 2
    lo = lax.bitcast_convert_type(y[:, :half].astype(BF16).astype(F32), jnp.uint32) >> 16
    hi = lax.bitcast_convert_type(y[:, half:].astype(BF16).astype(F32), jnp.uint32) & jnp.uint32(0xFFFF0000)
    return lo | hi


def _unpack_halves(w):
    lo = lax.bitcast_convert_type(w << 16, F32).astype(BF16)
    hi = lax.bitcast_convert_type(w & jnp.uint32(0xFFFF0000), F32).astype(BF16)
    return lo, hi


def _outproj_body(att_ref, gm_ref, ml_ref, w1_ref, w2_ref, w3_ref, h_ref, h_ctx_ref, gate_ref, y_ref, *,
                  alpha, lat_tiles):
    mix = (jnp.dot(att_ref[...], w1_ref[...], preferred_element_type=F32)
           + jnp.dot(gm_ref[...], w2_ref[...], preferred_element_type=F32)
           + jnp.dot(ml_ref[...], w3_ref[...], preferred_element_type=F32))
    h = jnp.where(pl.program_id(0) < lat_tiles, h_ref[...], h_ctx_ref[...])
    y_ref[...] = alpha * h + gate_ref[...] * mix


def _stream_specs(block, lat_tiles, col=None):
    def lat_map(i, *rest):
        return (jnp.minimum(i, lat_tiles - 1), 0 if col is None else rest[col])

    def ctx_map(i, *rest):
        return (jnp.maximum(i - lat_tiles, 0), 0 if col is None else rest[col])

    return [pl.BlockSpec(block, lat_map), pl.BlockSpec(block, ctx_map)]


def _outproj(att, gm, ml, w_out, layer, h_parts, mod, rows, bsz, n, alpha):
    dm = _dims()
    d = dm["d"]
    tm = _tile(n, (1024, 512, 256))
    tn = _tile(d, (512, 256, 128))
    tps = n // tm
    att_w, gm_w, ml_w = dm["att_w"], dm["gm_w"], dm["ml_w"]
    return pl.pallas_call(
        functools.partial(_outproj_body, alpha=alpha, lat_tiles=bsz * tps),
        grid=(rows // tm, d // tn),
        in_specs=[pl.BlockSpec((tm, att_w), lambda i, j: (i, 0)),
                  pl.BlockSpec((tm, gm_w), lambda i, j: (i, 0)),
                  pl.BlockSpec((tm, ml_w), lambda i, j: (i, 0)),
                  pl.BlockSpec((None, att_w, tn), lambda i, j: (layer, 0, j)),
                  pl.BlockSpec((None, gm_w, tn), lambda i, j: (layer, att_w // gm_w, j)),
                  pl.BlockSpec((None, ml_w, tn), lambda i, j: (layer, (att_w + gm_w) // ml_w, j)),
                  *_stream_specs((tm, tn), bsz * tps, col=0),
                  pl.BlockSpec((None, None, 1, tn), lambda i, j: (jnp.minimum(i // tps, bsz), 2, 0, j))],
        out_specs=pl.BlockSpec((tm, tn), lambda i, j: (i, j)),
        out_shape=jax.ShapeDtypeStruct((rows, d), F32),
        compiler_params=_params("parallel", "parallel"),
        name="outproj",
    )(att, gm, ml, w_out, w_out, w_out, *h_parts, mod)


def _post_ln_body(y_ref, g_ref, b_ref, shift_ref, scale_ref, wr_ref, hn_ref, act_ref, lt_ref):
    hn = _norm_rows(y_ref[...]) * g_ref[...] + b_ref[...]
    hn_ref[...] = hn
    act = _norm_rows(hn) * (1.0 + scale_ref[...]) + shift_ref[...]
    act_ref[...] = _pack_halves(act)
    lt_ref[...] = lax.dot_general(wr_ref[...], act.astype(BF16), (((1,), (1,)), ((), ())),
                                  preferred_element_type=F32)


def _post_ln(y, mod, ln_g, ln_b, w_r, bsz, n):
    rows, d = y.shape
    tm = ROW_TILE
    tps = n // tm
    vec = pl.BlockSpec((1, d), lambda i: (0, 0))
    return pl.pallas_call(
        _post_ln_body,
        grid=(rows // tm,),
        in_specs=[pl.BlockSpec((tm, d), lambda i: (i, 0)), vec, vec, _mod_spec(3, tps, bsz), _mod_spec(4, tps, bsz),
                  pl.BlockSpec((HEAD_DIM, d), lambda i: (0, 0))],
        out_specs=[pl.BlockSpec((tm, d), lambda i: (i, 0)), pl.BlockSpec((tm, d // 2), lambda i: (i, 0)),
                   pl.BlockSpec((HEAD_DIM, tm), lambda i: (0, i))],
        out_shape=[jax.ShapeDtypeStruct((rows, d), F32), jax.ShapeDtypeStruct((rows, d // 2), jnp.uint32),
                   jax.ShapeDtypeStruct((HEAD_DIM, rows), F32)],
        compiler_params=_params("parallel"),
        name="post_ln",
    )(y, ln_g.reshape(1, d), ln_b.reshape(1, d), mod, mod, w_r)


def _tile_part_copies(w_refs, stage_ref, sems, layer, tile, part, slot, n_inner, parts):
    rows, cols = stage_ref.shape[2] // parts, stage_ref.shape[3]
    rs = pl.ds(pl.multiple_of(part * rows, rows), rows)
    cs = pl.ds(pl.multiple_of((tile % n_inner) * cols, cols), cols)
    return [pltpu.make_async_copy(w.at[layer, tile // n_inner, rs, cs], stage_ref.at[slot, m, rs, :], sems.at[slot])
            for m, w in enumerate(w_refs)]


def _stream_weight_tile(w_refs, stage_ref, sems, layer, n_inner):
    t, b = pl.program_id(0), pl.program_id(1)
    parts = pl.num_programs(1)
    args = (w_refs, stage_ref, sems, layer)

    @pl.when((t == 0) & (b == 0))
    def _():
        for part in range(parts):
            for copy in _tile_part_copies(*args, 0, part, 0, n_inner, parts):
                copy.start()

    @pl.when(t + 1 < pl.num_programs(0))
    def _():
        for copy in _tile_part_copies(*args, t + 1, b, (t + 1) % 2, n_inner, parts):
            copy.start()

    @pl.when(b == 0)
    def _():
        for part in range(parts):
            for copy in _tile_part_copies(*args, t, part, t % 2, n_inner, parts):
                copy.wait()

    return t % 2


def _expert_up_body(x_ref, wg_ref, wu_ref, o_ref, stage_ref, sems, *, layer, nf):
    slot = _stream_weight_tile((wg_ref, wu_ref), stage_ref, sems, layer, nf)
    group, cap, half = x_ref.shape
    lo, hi = _unpack_halves(x_ref[...].reshape(group * cap, half))
    wg, wu = stage_ref[slot, 0].astype(BF16), stage_ref[slot, 1].astype(BF16)
    g = jnp.dot(lo, wg[0:half], preferred_element_type=F32) + jnp.dot(hi, wg[half:], preferred_element_type=F32)
    u = jnp.dot(lo, wu[0:half], preferred_element_type=F32) + jnp.dot(hi, wu[half:], preferred_element_type=F32)
    o_ref[...] = (g * _sigmoid(g) * u).astype(BF16).reshape(o_ref.shape)


def _sample_group(bsz):
    return 2 if bsz % 2 == 0 else 1


def _expert_up(xe, wg, wu, layer):
    bsz, ne, cap, half = xe.shape
    d = 2 * half
    ff = wg.shape[3]
    tf = _tile(ff, (256, 128))
    nf = ff // tf
    group = _sample_group(bsz)
    return pl.pallas_call(
        functools.partial(_expert_up_body, layer=layer, nf=nf),
        grid=(ne * nf, bsz // group),
        in_specs=[pl.BlockSpec((group, None, cap, half), lambda t, b: (b, t // nf, 0, 0)),
                  pl.BlockSpec(memory_space=pl.ANY), pl.BlockSpec(memory_space=pl.ANY)],
        out_specs=pl.BlockSpec((group, None, cap, tf), lambda t, b: (b, t // nf, 0, t % nf)),
        out_shape=jax.ShapeDtypeStruct((bsz, ne, cap, ff), BF16),
        scratch_shapes=[pltpu.VMEM((2, 2, d, tf), F32), pltpu.SemaphoreType.DMA((2,))],
        compiler_params=_params("arbitrary", "arbitrary"),
        name="expert_up",
    )(xe, wg, wu)


def _expert_down_body(x_ref, w_ref, gate_ref, o_ref, stage_ref, sems, *, layer, nj):
    slot = _stream_weight_tile((w_ref,), stage_ref, sems, layer, nj)
    group, cap, ff = x_ref.shape
    y = jnp.dot(x_ref[...].reshape(group * cap, ff), stage_ref[slot, 0].astype(BF16), preferred_element_type=F32)
    o_ref[...] = (y * gate_ref[...].reshape(group * cap, 1)).astype(BF16).reshape(o_ref.shape)


def _expert_down(hid, wd, gate, layer):
    bsz, ne, cap, ff = hid.shape
    d = wd.shape[3]
    tn = _tile(d, (2048, 1024, 512))
    nj = d // tn
    group = _sample_group(bsz)
    return pl.pallas_call(
        functools.partial(_expert_down_body, layer=layer, nj=nj),
        grid=(ne * nj, bsz // group),
        in_specs=[pl.BlockSpec((group, None, cap, ff), lambda t, b: (b, t // nj, 0, 0)),
                  pl.BlockSpec(memory_space=pl.ANY),
                  pl.BlockSpec((group, None, cap, 1), lambda t, b: (b, t // nj, 0, 0))],
        out_specs=pl.BlockSpec((group, None, cap, tn), lambda t, b: (b, t // nj, 0, t % nj)),
        out_shape=jax.ShapeDtypeStruct((bsz, ne, cap, d), BF16),
        scratch_shapes=[pltpu.VMEM((2, 1, ff, tn), F32), pltpu.SemaphoreType.DMA((2,))],
        compiler_params=_params("arbitrary", "arbitrary"),
        name="expert_down",
    )(hid, wd, gate)


def _prefix_excl(mask):
    lanes = 128
    x = jnp.where(mask, 1.0, 0.0)
    r = lax.broadcasted_iota(jnp.int32, (lanes, lanes), 0)
    c = lax.broadcasted_iota(jnp.int32, (lanes, lanes), 1)
    tri = jnp.where(r <= c, 1.0, 0.0).astype(BF16)
    carry = jnp.zeros((x.shape[0], 1), F32)
    out = []
    for t in range(x.shape[1] // lanes):
        xt = x[:, t * lanes:(t + 1) * lanes]
        inc = jnp.dot(xt.astype(BF16), tri, preferred_element_type=F32)
        out.append(inc - xt + carry)
        carry = carry + inc[:, lanes - 1:lanes]
    return jnp.concatenate(out, axis=1)


def _route_select_body(lg_ref, pos_ref, idx_ref, gate_ref, ts_ref, *, cap, chunk, tile):
    lg = lg_ref[...]
    ne, n = lg.shape
    ex = jnp.exp(lg - jnp.max(lg, axis=0, keepdims=True))
    aff = ex / jnp.sum(ex, axis=0, keepdims=True)
    bits = lax.bitcast_convert_type(aff, jnp.int32)

    thr = jnp.zeros((ne, 1), jnp.int32)
    for bit in range(30, -1, -1):
        cand = thr | (1 << bit)
        count = jnp.sum(jnp.where(bits >= cand, 1.0, 0.0), axis=1, keepdims=True)
        thr = jnp.where(count >= cap, cand, thr)
    above = bits > thr
    tied = bits == thr
    need = cap - jnp.sum(jnp.where(above, 1.0, 0.0), axis=1, keepdims=True)
    chosen = above | (tied & (_prefix_excl(tied) < need))
    before = _prefix_excl(chosen)
    pos = jnp.where(chosen, before, -1.0).astype(jnp.int32)
    pos_ref[...] = pos
    for t in range(n // tile):
        ts_ref[:, t:t + 1] = before[:, t * tile:t * tile + 1].astype(jnp.int32)

    tok = lax.broadcasted_iota(jnp.int32, (1, n), 1)
    slot = lax.broadcasted_iota(jnp.int32, (chunk, n), 0)
    tok_digits = [(tok // 64).astype(F32).astype(BF16), (tok % 64).astype(F32).astype(BF16)]
    for e in range(ne):
        a = aff[e:e + 1, :]
        a_hi = a.astype(BF16)
        a_rest = a - a_hi.astype(F32)
        a_mid = a_rest.astype(BF16)
        a_lo = (a_rest - a_mid.astype(F32)).astype(BF16)
        vals = jnp.concatenate(tok_digits + [a_hi, a_mid, a_lo, jnp.zeros((3, n), BF16)], axis=0)
        for c0 in range(0, cap, chunk):
            hit = jnp.where(pos[e:e + 1, :] == slot + c0, 1.0, 0.0).astype(BF16)
            r = lax.dot_general(hit, vals, (((1,), (1,)), ((), ())), preferred_element_type=F32)
            idx_ref[e, c0:c0 + chunk, :] = (r[:, 0:1] * 64.0 + r[:, 1:2]).astype(jnp.int32)
            gate_ref[e, c0:c0 + chunk, :] = r[:, 2:3] + r[:, 3:4] + r[:, 4:5]


def _route_select(lt, bsz, n, col_block0):
    ne = N_EXPERTS
    cap = CAPACITY_FACTOR * n // ne
    chunk = min(cap, 128)
    nt = n // ROW_TILE
    return pl.pallas_call(
        functools.partial(_route_select_body, cap=cap, chunk=chunk, tile=ROW_TILE),
        grid=(bsz,),
        in_specs=[pl.BlockSpec((ne, n), lambda b: (0, col_block0 + b))],
        out_specs=[pl.BlockSpec((None, ne, n), lambda b: (b, 0, 0)),
                   pl.BlockSpec((None, ne, cap, 1), lambda b: (b, 0, 0, 0)),
                   pl.BlockSpec((None, ne, cap, 1), lambda b: (b, 0, 0, 0)),
                   pl.BlockSpec((None, ne, nt), lambda b: (b, 0, 0))],
        out_shape=[jax.ShapeDtypeStruct((bsz, ne, n), jnp.int32),
                   jax.ShapeDtypeStruct((bsz, ne, cap, 1), jnp.int32),
                   jax.ShapeDtypeStruct((bsz, ne, cap, 1), F32),
                   jax.ShapeDtypeStruct((bsz, ne, nt), jnp.int32)],
        compiler_params=_params("parallel"),
        name="route_select",
    )(lt)


def _gather_body(idx_ref, src_ref, o_ref, buf_ref, sem):
    b, e = pl.program_id(0), pl.program_id(1)
    cap = buf_ref.shape[0]

    def row_copy(s):
        return pltpu.make_async_copy(src_ref.at[pl.ds(idx_ref[b, e, s], 1)], buf_ref.at[pl.ds(s, 1)], sem)

    def start(s, carry):
        row_copy(s).start()
        return carry

    def wait(s, carry):
        row_copy(s).wait()
        return carry

    unroll = 8 if cap % 8 == 0 else 1
    lax.fori_loop(0, cap, start, 0, unroll=unroll)
    lax.fori_loop(0, cap, wait, 0, unroll=unroll)
    o_ref[...] = buf_ref[...]


def _gather(idx, actp):
    bsz, ne, cap = idx.shape
    half = actp.shape[1]
    return pl.pallas_call(
        _gather_body,
        grid_spec=pltpu.PrefetchScalarGridSpec(
            num_scalar_prefetch=1,
            grid=(bsz, ne),
            in_specs=[pl.BlockSpec(memory_space=pl.ANY)],
            out_specs=pl.BlockSpec((None, None, cap, half), lambda b, e, idx_ref: (b, e, 0, 0)),
            scratch_shapes=[pltpu.VMEM((cap, half), jnp.uint32), pltpu.SemaphoreType.DMA(())]),
        out_shape=jax.ShapeDtypeStruct((bsz, ne, cap, half), jnp.uint32),
        compiler_params=_params("arbitrary", "arbitrary"),
        name="moe_gather",
    )(idx, actp)


def _combine_body(ts_ref, pos_ref, ye_ref, h_ref, gate_ref, g_ref, b_ref, *rest, alpha, window, chunk, tiles,
                  with_next):
    if with_next:
        shift_ref, scale_ref, hn_ref, act_ref, stage_ref, slow_ref, sems, slow_sem = rest
    else:
        hn_ref, stage_ref, slow_ref, sems, slow_sem = rest
    ne, tt = pos_ref.shape
    cap_total = ye_ref.shape[2]
    i = pl.program_id(0)
    buf = i % 2

    def window_start(step, e):
        b = step // tiles
        s = jnp.minimum((ts_ref[b, e, step % tiles] // 16) * 16, cap_total - window)
        return b, pl.multiple_of(s, 16)

    def window_copy(step, e, slot):
        b, s = window_start(step, e)
        return pltpu.make_async_copy(ye_ref.at[b, e, pl.ds(s, window)],
                                     stage_ref.at[slot, pl.ds(e * window, window)], sems.at[slot])

    @pl.when(i == 0)
    def _():
        for e in range(ne):
            window_copy(i, e, buf).start()

    @pl.when(i + 1 < pl.num_programs(0))
    def _():
        for e in range(ne):
            window_copy(i + 1, e, 1 - buf).start()

    for e in range(ne):
        window_copy(i, e, buf).wait()

    pos = pos_ref[...]
    slot_iota = lax.broadcasted_iota(jnp.int32, (window, tt), 0)
    starts, blocks, late = [], [], None
    for e in range(ne):
        starts.append(window_start(i, e)[1])
        rel = pos[e:e + 1, :] - starts[e]
        blocks.append(jnp.where(rel == slot_iota, 1.0, 0.0).astype(BF16))
        late = (rel >= window) if late is None else (late | (rel >= window))
    onehot = jnp.concatenate(blocks, axis=0)
    hn_ref[...] = lax.dot_general(onehot, stage_ref[buf], (((0,), (0,)), ((), ())), preferred_element_type=F32)

    @pl.when(jnp.max(jnp.where(late, 1, 0)) > 0)
    def _():
        chunk_iota = lax.broadcasted_iota(jnp.int32, (chunk, tt), 0)
        for e in range(ne):
            beyond = pos[e:e + 1, :] - starts[e] >= window
            for c0 in range(0, cap_total, chunk):
                copy = pltpu.make_async_copy(ye_ref.at[i // tiles, e, pl.ds(c0, chunk)], slow_ref, slow_sem)
                copy.start()
                copy.wait()
                hit = jnp.where(beyond & (pos[e:e + 1, :] - c0 == chunk_iota), 1.0, 0.0).astype(BF16)
                hn_ref[...] += lax.dot_general(hit, slow_ref[...], (((0,), (0,)), ((), ())),
                                               preferred_element_type=F32)

    sub = min(tt, 128)
    for r in range(tt // sub):
        rs = slice(r * sub, (r + 1) * sub)
        hn = _norm_rows(alpha * h_ref[rs, :] + gate_ref[...] * hn_ref[rs, :]) * g_ref[...] + b_ref[...]
        hn_ref[rs, :] = hn
        if with_next:
            act_ref[rs, :] = (_norm_rows(hn) * (1.0 + scale_ref[...]) + shift_ref[...]).astype(BF16)


def _slow_chunk(cap_total):
    for c in range(min(cap_total, 272) // 16 * 16, 0, -16):
        if cap_total % c == 0:
            return c
    raise ValueError(cap_total)


def _combine_resid(ts, pos, ye, h, mod, next_mod, ln_g, ln_b, bsz, n, ctx_len, alpha):
    ne, cap_total, d = ye.shape[1:]
    tt = ROW_TILE
    nt, ntc = n // tt, ctx_len // tt
    tiles = nt + ntc
    with_next = next_mod is not None
    window = min(COMBINE_WINDOW, cap_total)

    def sample(i):
        return i // tiles

    def row_tile(i):
        b, t = i // tiles, i % tiles
        return jnp.where(t < nt, b * nt + t, bsz * nt + b * ntc + (t - nt))

    def mod_spec(which):
        return pl.BlockSpec((None, None, 1, d),
                            lambda i, ts_ref: (jnp.where(i % tiles < nt, sample(i), bsz), which, 0, 0))

    vec = pl.BlockSpec((1, d), lambda i, ts_ref: (0, 0))
    blk = pl.BlockSpec((tt, d), lambda i, ts_ref: (row_tile(i), 0))
    in_specs = [pl.BlockSpec((None, ne, tt), lambda i, ts_ref: (sample(i), 0, i % tiles)),
                pl.BlockSpec(memory_space=pl.ANY), blk, mod_spec(5), vec, vec]
    args = [pos, ye, h, mod, ln_g.reshape(1, d), ln_b.reshape(1, d)]
    rows = bsz * tiles * tt
    out_specs, out_shape = [blk], [jax.ShapeDtypeStruct((rows, d), F32)]
    if with_next:
        in_specs += [mod_spec(0), mod_spec(1)]
        args += [next_mod, next_mod]
        out_specs.append(blk)
        out_shape.append(jax.ShapeDtypeStruct((rows, d), BF16))
    return pl.pallas_call(
        functools.partial(_combine_body, alpha=alpha, window=window, chunk=_slow_chunk(cap_total), tiles=tiles,
                          with_next=with_next),
        grid_spec=pltpu.PrefetchScalarGridSpec(
            num_scalar_prefetch=1,
            grid=(bsz * tiles,),
            in_specs=in_specs,
            out_specs=out_specs,
            scratch_shapes=[pltpu.VMEM((2, ne * window, d), BF16), pltpu.VMEM((_slow_chunk(cap_total), d), BF16),
                            pltpu.SemaphoreType.DMA((2,)), pltpu.SemaphoreType.DMA(())]),
        out_shape=out_shape,
        compiler_params=_params("arbitrary"),
        name="moe_combine",
    )(ts, *args)


def kernel(x, c, ctx, c_ctx, w_mod, b_mod, w_in, q_gain, k_gain, gm_ln_g, gm_ln_b, w_spatial, b_spatial, b_gates,
           ml_gain, w_out, ln1_g, ln1_b, w_router, w_e_gate, w_e_up, w_e_down, ln2_g, ln2_b):
    dm = _dims()
    bsz, n, d = x.shape
    ctx_len = ctx.shape[1]
    depth = w_mod.shape[0]
    alpha = (2 * depth) ** 0.25
    lat_rows = bsz * n
    main_w = dm["main_w"]
    n_gates = 4 * MLSTM_HEADS
    cap_l = CAPACITY_FACTOR * n // N_EXPERTS
    sample = jnp.arange(bsz, dtype=jnp.int32)[:, None, None]

    cc = jnp.concatenate([c, c_ctx[None], jnp.zeros((MOD_ROWS - bsz - 1, d), F32)], axis=0)
    mods = _mod_vectors(cc, w_mod, b_mod).reshape(depth, MOD_ROWS, N_MOD, 1, d)
    cos, sin = _rope_tables(n, ROW_TILE)
    w_in_t = jnp.swapaxes(w_in, 1, 2)
    w_out_b = w_out.astype(BF16)

    h_parts = (x.reshape(lat_rows, d), ctx.reshape(bsz * ctx_len, d))
    act = _ln_mod(*h_parts, mods[0], bsz, n)
    for l in range(depth):
        last = l == depth - 1
        rows = lat_rows if last else lat_rows + bsz * ctx_len
        mod = mods[l]

        p = _in_proj(act, w_in_t, l, main_w)
        gates = _gate_proj(act, w_in_t, l, main_w, n_gates, b_gates[l])
        gates_rows = jnp.swapaxes(jnp.concatenate([gates[:, lat_rows:].reshape(n_gates, bsz, ctx_len),
                                                   gates[:, :lat_rows].reshape(n_gates, bsz, n)], axis=2), 0, 1)

        qb, kb, vb = _qkv_prep(p, cos, sin, q_gain[l], k_gain[l], bsz, n, ctx_len)
        att = _attention(qb, kb, vb, bsz, n, ctx_len, not last)
        gm = _gmlp(p, gm_ln_g[l], gm_ln_b[l], w_spatial[l], b_spatial[l])
        hf, hb = _mlstm(p, gates_rows, bsz, n, ctx_len)
        ml = _mlstm_out(hf, hb, p, ml_gain[l], rows)
        y = _outproj(att, gm, ml, w_out_b, l, h_parts, mod, rows, bsz, n, alpha)
        w_r = jnp.pad(w_router[l].T.astype(BF16), ((0, HEAD_DIM - N_EXPERTS), (0, 0)))
        h, actp, lt = _post_ln(y, mod, ln1_g[l], ln1_b[l], w_r, bsz, n)

        pos, idx_l, gate, ts = _route_select(lt, bsz, n, 0)
        idx = idx_l[..., 0] + n * sample
        if not last:
            pos_c, idx_c, gate_c, ts_c = _route_select(lt, bsz, ctx_len, lat_rows // ctx_len)
            idx = jnp.concatenate([idx, idx_c[..., 0] + lat_rows + ctx_len * sample], axis=2)
            gate = jnp.concatenate([gate, gate_c], axis=2)
            pos = jnp.concatenate([pos, jnp.where(pos_c >= 0, pos_c + cap_l, -1)], axis=2)
            ts = jnp.concatenate([ts, ts_c + cap_l], axis=2)
        xe = _gather(idx, actp)
        hid = _expert_up(xe, w_e_gate, w_e_up, l)
        ye = _expert_down(hid, w_e_down, gate, l)
        if not last:
            h, act = _combine_resid(ts, pos, ye, h, mod, mods[l + 1], ln2_g[l], ln2_b[l], bsz, n, ctx_len, alpha)
            h_parts = (h, h)
        else:
            h, = _combine_resid(ts, pos, ye, h, mod, None, ln2_g[l], ln2_b[l], bsz, n, 0, alpha)
    return h.reshape(bsz, n, d)
```
